```python
import jax, jax.numpy as jnp
from jax import lax
import numpy as np

D_MODEL = 4096
BATCH = 4
SEQ = 2048
DEPTH = 2
DEC_BATCH = 32
DEC_SEQ = 16
PAST_LEN = 1024

CHUNK = 64
RET_HEADS = 8
RET_DK = 256
RET_DV = 256
RET_QK = RET_HEADS * RET_DK
RET_V = RET_HEADS * RET_DV
SGU_WIDTH = D_MODEL // 2
SGU_GROUPS = 8
SGU_GROUP_DIM = SGU_WIDTH // SGU_GROUPS
SGU_CHUNK = 128
D_FF = 11008
CONV_W = 3
ROPE_BASE = 10000.0
LN_EPS = 1e-5
GN_EPS = 1e-6
DEEPNORM_ALPHA = (2 * DEPTH) ** 0.25
DEEPNORM_BETA = (8 * DEPTH) ** -0.25
SPLIT_IDX = (RET_QK,
             2 * RET_QK,
             2 * RET_QK + RET_V,
             2 * RET_QK + 2 * RET_V,
             2 * RET_QK + 2 * RET_V + SGU_WIDTH,
             2 * RET_QK + 2 * RET_V + 2 * SGU_WIDTH,
             2 * RET_QK + 2 * RET_V + 2 * SGU_WIDTH + D_MODEL)
IN_COLS = 2 * RET_QK + 2 * RET_V + 2 * SGU_WIDTH + 2 * D_MODEL

kernel_name = "retnet_gmlp_convffn_streaming_encoder_step"


def layer_norm(x, g, b):
    xf = x.astype(jnp.float32)
    mu = jnp.mean(xf, axis=-1, keepdims=True)
    var = jnp.mean(jnp.square(xf - mu), axis=-1, keepdims=True)
    return ((xf - mu) * lax.rsqrt(var + LN_EPS)).astype(x.dtype) * g + b


def head_norm(o):
    mu = jnp.mean(o, axis=-1, keepdims=True)
    var = jnp.mean(jnp.square(o - mu), axis=-1, keepdims=True)
    return (o - mu) * lax.rsqrt(var + GN_EPS)


def rope(t, pos):
    half = t.shape[-1] // 2
    freqs = ROPE_BASE ** (-jnp.arange(half, dtype=jnp.float32) / half)
    ang = pos[:, None] * freqs[None, :]
    cos = jnp.cos(ang)[None, :, None, :]
    sin = jnp.sin(ang)[None, :, None, :]
    t1, t2 = t[..., :half], t[..., half:]
    return jnp.concatenate([t1 * cos - t2 * sin, t1 * sin + t2 * cos], axis=-1)


def retention(q, k, v, s0, log_gamma):
    B, L, H, _ = q.shape
    c = min(CHUNK, L)
    n = L // c
    idx = jnp.arange(c, dtype=jnp.float32)
    diff = idx[:, None] - idx[None, :]
    decay = jnp.where((diff >= 0)[None], jnp.exp(jnp.maximum(diff, 0.0)[None] * log_gamma[:, None, None]), 0.0)
    q_decay = jnp.exp((idx[None, :] + 1.0) * log_gamma[:, None])
    k_decay = jnp.exp((c - 1.0 - idx[None, :]) * log_gamma[:, None])
    chunk_decay = jnp.exp(c * log_gamma)

    def to_chunks(t):
        return t.reshape(B, n, c, H, t.shape[-1]).transpose(1, 0, 3, 2, 4)

    qc, kc, vc = to_chunks(q), to_chunks(k), to_chunks(v)

    def step(s, xs):
        qi, ki, vi = xs
        scores = jnp.einsum('bhid,bhjd->bhij', qi, ki) * decay[None]
        o = (jnp.einsum('bhij,bhje->bhie', scores, vi)
             + jnp.einsum('bhid,bhde->bhie', qi * q_decay[None, :, :, None], s))
        s = (s * chunk_decay[None, :, None, None]
             + jnp.einsum('bhjd,bhje->bhde', ki * k_decay[None, :, :, None], vi))
        return s, o

    s, o = lax.scan(step, s0, (qc, kc, vc))
    o = o.transpose(1, 0, 3, 2, 4).reshape(B, L, H, v.shape[-1])
    return o, s


def spatial_gating(su, sv, w_s, b_s):
    B, L, _ = sv.shape
    c = min(SGU_CHUNK, L)
    n = L // c
    mask = jnp.tril(jnp.ones((c, c), dtype=bool))
    w = jnp.where(mask[None], w_s[:, :c, :c], jnp.zeros((), w_s.dtype))
    v = sv.reshape(B, n, c, SGU_GROUPS, SGU_GROUP_DIM)
    mixed = jnp.einsum('gij,bnjgc->bnigc', w, v) + b_s[:, :c].T[None, None, :, :, None]
    return su * mixed.reshape(B, L, SGU_WIDTH)


def layer(x, pos0, ret_s0, conv_s0, w_in, w_a, w_b, w_o, ln1_g, ln1_b, sgu_ln_g, sgu_ln_b,
          sgu_w, sgu_b, w_up, conv_w, conv_b, w_down, ln2_g, ln2_b):
    B, L, _ = x.shape
    log_gamma = jnp.log(1.0 - 2.0 ** (-5.0 - jnp.arange(RET_HEADS, dtype=jnp.float32)))
    h = x @ w_in
    q, k, v, g, su, sv, ga, gb = jnp.split(h, SPLIT_IDX, axis=-1)
    pos = pos0 + jnp.arange(L, dtype=jnp.float32)
    q = rope(q.reshape(B, L, RET_HEADS, RET_DK).astype(jnp.float32), pos)
    k = rope(k.reshape(B, L, RET_HEADS, RET_DK).astype(jnp.float32), pos) * (RET_DK ** -0.5)
    v = v.reshape(B, L, RET_HEADS, RET_DV).astype(jnp.float32)
    o, ret_s = retention(q, k, v, ret_s0.astype(jnp.float32), log_gamma)
    o = head_norm(o).astype(x.dtype).reshape(B, L, RET_V)
    branch_a = (jax.nn.silu(g) * o) @ w_a
    su = jax.nn.gelu(su)
    sv = layer_norm(jax.nn.gelu(sv), sgu_ln_g, sgu_ln_b)
    branch_b = spatial_gating(su, sv, sgu_w, sgu_b) @ w_b
    mix = (jax.nn.sigmoid(ga) * branch_a + jax.nn.sigmoid(gb) * branch_b) @ w_o
    x = layer_norm(DEEPNORM_ALPHA * x + mix, ln1_g, ln1_b)
    up = x @ w_up
    gate_pre, val = jnp.split(up, [D_FF], axis=-1)
    buf = jnp.concatenate([conv_s0.astype(gate_pre.dtype), gate_pre], axis=1)
    conv = conv_b + buf[:, 0:L] * conv_w[0]
    for j in range(1, CONV_W):
        conv = conv + buf[:, j:j + L] * conv_w[j]
    ffn = (jax.nn.gelu(conv) * val) @ w_down
    x = layer_norm(DEEPNORM_ALPHA * x + ffn, ln2_g, ln2_b)
    return x, ret_s.astype(ret_s0.dtype), buf[:, L:], sv


def setup_inputs(seed: int = 0) -> dict:
    key = jax.random.key(seed)
    ks = jax.random.split(key, 24)
    f32 = jnp.float32
    beta = DEEPNORM_BETA

    def nrm(k, shape, scale):
        return jax.random.normal(k, shape, f32) * scale

    return {
        "x_prompt": nrm(ks[0], (BATCH, SEQ, D_MODEL), 1.0),
        "x_sample": nrm(ks[1], (DEC_BATCH, DEC_SEQ, D_MODEL), 1.0),
        "state_ret": nrm(ks[2], (DEPTH, DEC_BATCH, RET_HEADS, RET_DK, RET_DV), 1.0),
        "state_conv": nrm(ks[3], (DEPTH, DEC_BATCH, CONV_W - 1, D_FF), 0.5),
        "w_in": nrm(ks[4], (DEPTH, D_MODEL, IN_COLS), D_MODEL ** -0.5),
        "w_a": nrm(ks[5], (DEPTH, RET_V, D_MODEL), beta * RET_V ** -0.5),
        "w_b": nrm(ks[6], (DEPTH, SGU_WIDTH, D_MODEL), beta * SGU_WIDTH ** -0.5),
        "w_o": nrm(ks[7], (DEPTH, D_MODEL, D_MODEL), beta * D_MODEL ** -0.5),
        "ln1_g": 1.0 + nrm(ks[8], (DEPTH, D_MODEL), 0.02),
        "ln1_b": nrm(ks[9], (DEPTH, D_MODEL), 0.02),
        "sgu_ln_g": 1.0 + nrm(ks[10], (DEPTH, SGU_WIDTH), 0.02),
        "sgu_ln_b": nrm(ks[11], (DEPTH, SGU_WIDTH), 0.02),
        "sgu_w": nrm(ks[12], (DEPTH, SGU_GROUPS, SGU_CHUNK, SGU_CHUNK), SGU_CHUNK ** -0.5),
        "sgu_b": 1.0 + nrm(ks[13], (DEPTH, SGU_GROUPS, SGU_CHUNK), 0.1),
        "w_up": nrm(ks[14], (DEPTH, D_MODEL, 2 * D_FF), beta * D_MODEL ** -0.5),
        "conv_w": nrm(ks[15], (DEPTH, CONV_W, D_FF), CONV_W ** -0.5),
        "conv_b": nrm(ks[16], (DEPTH, D_FF), 0.02),
        "w_down": nrm(ks[17], (DEPTH, D_FF, D_MODEL), beta * D_FF ** -0.5),
        "ln2_g": 1.0 + nrm(ks[18], (DEPTH, D_MODEL), 0.02),
        "ln2_b": nrm(ks[19], (DEPTH, D_MODEL), 0.02),
    }


def reference(x_prompt, x_sample, state_ret, state_conv, w_in, w_a, w_b, w_o, ln1_g, ln1_b,
              sgu_ln_g, sgu_ln_b, sgu_w, sgu_b, w_up, conv_w, conv_b, w_down, ln2_g, ln2_b):
    bp = x_prompt.shape[0]
    zero_ret = jnp.zeros((bp, RET_HEADS, RET_DK, RET_DV), state_ret.dtype)
    zero_conv = jnp.zeros((bp, CONV_W - 1, D_FF), state_conv.dtype)
    yp, ys = x_prompt, x_sample
    ret_p, conv_p, ret_s, conv_s, sgu_v_s = [], [], [], [], []
    for l in range(DEPTH):
        params = (w_in[l], w_a[l], w_b[l], w_o[l], ln1_g[l], ln1_b[l], sgu_ln_g[l], sgu_ln_b[l],
                  sgu_w[l], sgu_b[l], w_up[l], conv_w[l], conv_b[l], w_down[l], ln2_g[l], ln2_b[l])
        yp, rp, cp, _ = layer(yp, 0.0, zero_ret, zero_conv, *params)
        ys, rs, cs, vs = layer(ys, float(PAST_LEN), state_ret[l], state_conv[l], *params)
        ret_p.append(rp)
        conv_p.append(cp)
        ret_s.append(rs)
        conv_s.append(cs)
        sgu_v_s.append(vs)
    new_state_ret_prompt = jnp.stack(ret_p)
    new_state_conv_prompt = jnp.stack(conv_p)
    new_state_ret_sample = jnp.stack(ret_s)
    new_state_conv_sample = jnp.stack(conv_s)
    new_sgu_v_sample = jnp.stack(sgu_v_s)
    return (yp, ys, new_state_ret_prompt, new_state_conv_prompt, new_state_ret_sample, new_state_conv_sample, new_sgu_v_sample)
```

```python
import functools

import jax
import jax.numpy as jnp
from jax import lax
from jax.experimental import pallas as pl
from jax.experimental.pallas import tpu as pltpu

F32 = jnp.float32
BF16 = jnp.bfloat16

RET_HEADS = 8
RET_DK = 256
RET_DV = 256
RET_QK = RET_HEADS * RET_DK
RET_V = RET_HEADS * RET_DV
SGU_GROUPS = 8
SGU_CHUNK = 128
CONV_W = 3
PAST_LEN = 1024
ROPE_BASE = 10000.0
LN_EPS = 1e-5
GN_EPS = 1e-6

V7X_VMEM_BYTES = 64 * 1024 * 1024
LANES = 128
RET_BLOCK = 256


def _vmem_limit(estimate_bytes):
    return int(min(V7X_VMEM_BYTES - (4 << 20), max(estimate_bytes * 5 // 4 + (2 << 20), 16 << 20)))


def _largest_divisor(n, limit, multiple):
    best = None
    d = multiple
    while d <= min(n, limit):
        if n % d == 0:
            best = d
        d += multiple
    assert best is not None, (n, limit, multiple)
    return best


def _mm_body(x_ref, w_ref, o_ref, wbf_ref):
    @pl.when(pl.program_id(1) == 0)
    def _():
        wbf_ref[...] = w_ref[...].astype(BF16)

    o_ref[...] = jnp.dot(x_ref[...], wbf_ref[...], preferred_element_type=F32).astype(o_ref.dtype)


def _matmul(x, w, layer, *, tm, tn, out_dtype, name):
    m, k = x.shape
    n = w.shape[2]
    est = 2 * tm * k * 2 + 2 * k * tn * 4 + k * tn * 2 + 2 * tm * tn * jnp.dtype(out_dtype).itemsize
    return pl.pallas_call(
        _mm_body,
        grid=(n // tn, m // tm),
        in_specs=[pl.BlockSpec((tm, k), lambda j, i: (i, 0)),
                  pl.BlockSpec((None, k, tn), lambda j, i: (layer, 0, j))],
        out_specs=pl.BlockSpec((tm, tn), lambda j, i: (i, j)),
        out_shape=jax.ShapeDtypeStruct((m, n), out_dtype),
        scratch_shapes=[pltpu.VMEM((k, tn), BF16)],
        compiler_params=pltpu.CompilerParams(
            dimension_semantics=("arbitrary", "arbitrary"), vmem_limit_bytes=_vmem_limit(est)),
        name=name,
    )(x, w)


def _merge_body(za_ref, zb_ref, ga_ref, gb_ref, wa_ref, wb_ref, o_ref, wabf_ref, wbbf_ref):
    @pl.when(pl.program_id(1) == 0)
    def _():
        wabf_ref[...] = wa_ref[...].astype(BF16)
        wbbf_ref[...] = wb_ref[...].astype(BF16)

    a = jnp.dot(za_ref[...], wabf_ref[...], preferred_element_type=F32)
    b = jnp.dot(zb_ref[...], wbbf_ref[...], preferred_element_type=F32)
    o_ref[...] = (jax.nn.sigmoid(ga_ref[...]) * a + jax.nn.sigmoid(gb_ref[...]) * b).astype(o_ref.dtype)


def _merge(za, zb, h, w_a, w_b, layer, *, ga_off, gb_off, tm, tn):
    m, ka = za.shape
    kb = zb.shape[1]
    n = w_a.shape[2]
    ga_blk, gb_blk = ga_off // tn, gb_off // tn
    est = (2 * tm * (ka + kb) * 2 + 2 * (ka + kb) * tn * 4 + (ka + kb) * tn * 2
           + 2 * 2 * tm * tn * 4 + 2 * tm * tn * 2)
    return pl.pallas_call(
        _merge_body,
        grid=(n // tn, m // tm),
        in_specs=[pl.BlockSpec((tm, ka), lambda j, i: (i, 0)),
                  pl.BlockSpec((tm, kb), lambda j, i: (i, 0)),
                  pl.BlockSpec((tm, tn), lambda j, i: (i, ga_blk + j)),
                  pl.BlockSpec((tm, tn), lambda j, i: (i, gb_blk + j)),
                  pl.BlockSpec((None, ka, tn), lambda j, i: (layer, 0, j)),
                  pl.BlockSpec((None, kb, tn), lambda j, i: (layer, 0, j))],
        out_specs=pl.BlockSpec((tm, tn), lambda j, i: (i, j)),
        out_shape=jax.ShapeDtypeStruct((m, n), BF16),
        scratch_shapes=[pltpu.VMEM((ka, tn), BF16), pltpu.VMEM((kb, tn), BF16)],
        compiler_params=pltpu.CompilerParams(
            dimension_semantics=("arbitrary", "arbitrary"), vmem_limit_bytes=_vmem_limit(est)),
        name="merge_mm",
    )(za, zb, h, h, w_a, w_b)


def _ln_body(x_ref, y_ref, g_ref, b_ref, o_ref, obf_ref, *, alpha):
    t = alpha * x_ref[...] + y_ref[...]
    mu = jnp.mean(t, axis=-1, keepdims=True)
    d = t - mu
    var = jnp.mean(d * d, axis=-1, keepdims=True)
    out = d * lax.rsqrt(var + LN_EPS) * g_ref[...] + b_ref[...]
    o_ref[...] = out
    obf_ref[...] = out.astype(BF16)


def _add_ln(x, y, g, b, *, alpha, tm):
    m, d = x.shape
    est = 2 * 2 * tm * d * 4 + 2 * tm * d * 4 + 2 * tm * d * 2
    row = pl.BlockSpec((tm, d), lambda i: (i, 0))
    vec = pl.BlockSpec((1, d), lambda i: (0, 0))
    return pl.pallas_call(
        functools.partial(_ln_body, alpha=alpha),
        grid=(m // tm,),
        in_specs=[row, row, vec, vec],
        out_specs=[row, row],
        out_shape=[jax.ShapeDtypeStruct((m, d), F32), jax.ShapeDtypeStruct((m, d), BF16)],
        compiler_params=pltpu.CompilerParams(
            dimension_semantics=("arbitrary",), vmem_limit_bytes=_vmem_limit(est)),
        name="add_ln",
    )(x, y, g.reshape(1, d), b.reshape(1, d))


def _ret_body(q_ref, k_ref, v_ref, g_ref, cos_ref, sin_ref, lg_ref, s0_ref, za_ref, sout_ref,
              s_ref, d_ref, qd_ref, kd_ref, *, c):
    b = pl.program_id(1)
    ci = pl.program_id(2)
    lg = lg_ref[...]

    @pl.when((b == 0) & (ci == 0))
    def _():
        ri = lax.broadcasted_iota(jnp.int32, (c, c), 0)
        cj = lax.broadcasted_iota(jnp.int32, (c, c), 1)
        diff = (ri - cj).astype(F32)
        d_ref[...] = jnp.where(diff >= 0, jnp.exp(jnp.maximum(diff, 0.0) * lg[:, :c]), 0.0)
        r = lax.broadcasted_iota(jnp.int32, (c, RET_DK), 0).astype(F32)
        qd_ref[...] = jnp.exp((r + 1.0) * lg)
        kd_ref[...] = jnp.exp((c - 1.0 - r) * lg)

    @pl.when(ci == 0)
    def _():
        s_ref[...] = s0_ref[...]

    cos = cos_ref[...]
    sin = sin_ref[...]
    half = RET_DK // 2

    def rope(t):
        t1, t2 = t[:, :half], t[:, half:]
        return jnp.concatenate([t1 * cos - t2 * sin, t1 * sin + t2 * cos], axis=-1)

    q = rope(q_ref[...])
    k = rope(k_ref[...]) * (RET_DK ** -0.5)
    vb = v_ref[...].astype(BF16)
    s = s_ref[...]
    scores = lax.dot_general(q.astype(BF16), k.astype(BF16), (((1,), (1,)), ((), ())),
                             preferred_element_type=F32) * d_ref[...]
    o = (jnp.dot(scores.astype(BF16), vb, preferred_element_type=F32)
         + jnp.dot((q * qd_ref[...]).astype(BF16), s.astype(BF16), preferred_element_type=F32))
    kv = lax.dot_general((k * kd_ref[...]).astype(BF16), vb, (((0,), (0,)), ((), ())),
                         preferred_element_type=F32)
    s_new = s * jnp.exp(float(c) * lg) + kv
    s_ref[...] = s_new

    mu = jnp.mean(o, axis=-1, keepdims=True)
    od = o - mu
    var = jnp.mean(od * od, axis=-1, keepdims=True)
    on = od * lax.rsqrt(var + GN_EPS)
    za_ref[...] = (jax.nn.silu(g_ref[...]) * on).astype(BF16)

    @pl.when(ci == pl.num_programs(2) - 1)
    def _():
        sout_ref[...] = s_new


def _retention(h, s0, *, row0, nseq, seqlen, pos0):
    c = min(RET_BLOCK, seqlen)
    nc = seqlen // c
    rb0 = row0 // c
    hq = RET_QK // RET_DK
    half = RET_DK // 2
    pos = pos0 + jnp.arange(seqlen, dtype=F32)
    freqs = ROPE_BASE ** (-jnp.arange(half, dtype=F32) / half)
    ang = pos[:, None] * freqs[None, :]
    cos, sin = jnp.cos(ang), jnp.sin(ang)
    log_gamma = jnp.log(1.0 - 2.0 ** (-5.0 - jnp.arange(RET_HEADS, dtype=F32)))
    lg = jnp.broadcast_to(log_gamma[:, None, None], (RET_HEADS, 1, RET_DK))

    def hcol(off):
        return pl.BlockSpec((c, RET_DK), lambda hh, b, ci: (rb0 + b * nc + ci, off + hh))

    tab = pl.BlockSpec((c, half), lambda hh, b, ci: (ci, 0))
    state = pl.BlockSpec((None, None, RET_DK, RET_DV), lambda hh, b, ci: (b, hh, 0, 0))
    return pl.pallas_call(
        functools.partial(_ret_body, c=c),
        grid=(RET_HEADS, nseq, nc),
        in_specs=[hcol(0), hcol(hq), hcol(2 * hq), hcol(3 * hq), tab, tab,
                  pl.BlockSpec((None, 1, RET_DK), lambda hh, b, ci: (hh, 0, 0)), state],
        out_specs=[pl.BlockSpec((c, RET_DV), lambda hh, b, ci: (b * nc + ci, hh)), state],
        out_shape=[jax.ShapeDtypeStruct((nseq * seqlen, RET_V), BF16),
                   jax.ShapeDtypeStruct((nseq, RET_HEADS, RET_DK, RET_DV), F32)],
        scratch_shapes=[pltpu.VMEM((RET_DK, RET_DV), F32), pltpu.VMEM((c, c), F32),
                        pltpu.VMEM((c, RET_DK), F32), pltpu.VMEM((c, RET_DK), F32)],
        compiler_params=pltpu.CompilerParams(
            dimension_semantics=("arbitrary", "arbitrary", "arbitrary")),
        name="retention",
    )(h, h, h, h, cos, sin, lg, s0)


def _sgu_body(su_ref, sv_ref, lng_ref, lnb_ref, w_ref, bias_ref, zb_ref, *rest, c, tm, gdim, emit_v):
    sv = jax.nn.gelu(sv_ref[...])
    mu = jnp.mean(sv, axis=-1, keepdims=True)
    d = sv - mu
    var = jnp.mean(d * d, axis=-1, keepdims=True)
    svn = d * lax.rsqrt(var + LN_EPS) * lng_ref[...] + lnb_ref[...]
    if emit_v:
        rest[0][...] = svn
    svb = svn.astype(BF16)
    ri = lax.broadcasted_iota(jnp.int32, (SGU_CHUNK, SGU_CHUNK), 0)
    cj = lax.broadcasted_iota(jnp.int32, (SGU_CHUNK, SGU_CHUNK), 1)
    mask = (ri >= cj) & ((ri // c) == (cj // c))
    for g in range(SGU_GROUPS):
        wm = jnp.where(mask, w_ref[g], 0.0).astype(BF16)
        cols = slice(g * gdim, (g + 1) * gdim)
        for r in range(tm // SGU_CHUNK):
            rows = slice(r * SGU_CHUNK, (r + 1) * SGU_CHUNK)
            mixed = jnp.dot(wm, svb[rows, cols], preferred_element_type=F32) + bias_ref[:, cols]
            zb_ref[rows, cols] = (jax.nn.gelu(su_ref[rows, cols]) * mixed).astype(BF16)


def _spatial_gating(h, ln_g, ln_b, w_s, b_s, *, row0, nrows, seqlen, su_off, width, emit_v):
    c = min(SGU_CHUNK, seqlen)
    rep = SGU_CHUNK // c
    gdim = width // SGU_GROUPS
    w_blk = jnp.tile(w_s[:, :c, :c], (1, rep, rep))
    bias = jnp.repeat(jnp.tile(b_s[:, :c], (1, rep)).T, gdim, axis=1)
    tm = _largest_divisor(nrows, 256, SGU_CHUNK)
    rb0 = row0 // tm
    su_blk = su_off // width
    row = pl.BlockSpec((tm, width), lambda i: (i, 0))
    vec = pl.BlockSpec((1, width), lambda i: (0, 0))
    out_specs = [row]
    out_shape = [jax.ShapeDtypeStruct((nrows, width), BF16)]
    if emit_v:
        out_specs.append(row)
        out_shape.append(jax.ShapeDtypeStruct((nrows, width), F32))
    est = 2 * 2 * tm * width * 4 + 2 * tm * width * (2 + 4) + 8 * tm * width * 4
    res = pl.pallas_call(
        functools.partial(_sgu_body, c=c, tm=tm, gdim=gdim, emit_v=emit_v),
        grid=(nrows // tm,),
        in_specs=[pl.BlockSpec((tm, width), lambda i: (rb0 + i, su_blk)),
                  pl.BlockSpec((tm, width), lambda i: (rb0 + i, su_blk + 1)),
                  vec, vec,
                  pl.BlockSpec((SGU_GROUPS, SGU_CHUNK, SGU_CHUNK), lambda i: (0, 0, 0)),
                  pl.BlockSpec((SGU_CHUNK, width), lambda i: (0, 0))],
        out_specs=out_specs,
        out_shape=out_shape,
        compiler_params=pltpu.CompilerParams(
            dimension_semantics=("arbitrary",), vmem_limit_bytes=_vmem_limit(est)),
        name="spatial_gating",
    )(h, h, ln_g.reshape(1, width), ln_b.reshape(1, width), w_blk, bias)
    return res if emit_v else (res[0], None)


def _convgate_body(gate_ref, val_ref, s0_ref, cw_ref, cb_ref, z_ref, buf_ref, *, tr):
    r = pl.program_id(2)

    @pl.when(r == 0)
    def _():
        buf_ref[8 - (CONV_W - 1):8, :] = s0_ref[...]

    @pl.when(r > 0)
    def _():
        buf_ref[0:8, :] = buf_ref[tr:tr + 8, :]

    g = gate_ref[...]
    buf_ref[8:tr + 8, :] = g
    g1 = buf_ref[7:tr + 7, :]
    g2 = buf_ref[6:tr + 6, :]
    conv = cb_ref[...] + g2 * cw_ref[0:1, :] + g1 * cw_ref[1:2, :] + g * cw_ref[2:3, :]
    z_ref[...] = (jax.nn.gelu(conv) * val_ref[...]).astype(BF16)


def _convgate(up, s0, conv_w, conv_b, *, row0, nseq, seqlen, dff):
    tn = _largest_divisor(dff, 5504, LANES)
    tr = _largest_divisor(seqlen, 128, 8)
    nr = seqlen // tr
    rb0 = row0 // tr
    vblk = dff // tn
    est = 2 * 2 * tr * tn * 4 + 2 * tr * tn * 2 + (tr + 8) * tn * 4 + 6 * tr * tn * 4
    return pl.pallas_call(
        functools.partial(_convgate_body, tr=tr),
        grid=(nseq, dff // tn, nr),
        in_specs=[pl.BlockSpec((tr, tn), lambda b, j, r: (rb0 + b * nr + r, j)),
                  pl.BlockSpec((tr, tn), lambda b, j, r: (rb0 + b * nr + r, vblk + j)),
                  pl.BlockSpec((None, CONV_W - 1, tn), lambda b, j, r: (b, 0, j)),
                  pl.BlockSpec((CONV_W, tn), lambda b, j, r: (0, j)),
                  pl.BlockSpec((1, tn), lambda b, j, r: (0, j))],
        out_specs=pl.BlockSpec((tr, tn), lambda b, j, r: (b * nr + r, j)),
        out_shape=jax.ShapeDtypeStruct((nseq * seqlen, dff), BF16),
        scratch_shapes=[pltpu.VMEM((tr + 8, tn), F32)],
        compiler_params=pltpu.CompilerParams(
            dimension_semantics=("arbitrary", "arbitrary", "arbitrary"),
            vmem_limit_bytes=_vmem_limit(est)),
        name="convgate",
    )(up, up, s0, conv_w, conv_b.reshape(1, dff))


def _layer(layer, x, xb, ret_s0_s, conv_s0_s, big, small, *, bp, lp, bs, ls, alpha):
    w_in, w_a, w_b, w_o, w_up, w_down = big
    ln1_g, ln1_b, sgu_ln_g, sgu_ln_b, sgu_w, sgu_b, conv_w, conv_b, ln2_g, ln2_b = small
    m, d = x.shape
    mp = bp * lp
    ms = bs * ls
    width = d // 2
    dff = w_down.shape[1]
    su_off = 2 * RET_QK + 2 * RET_V
    ga_off = su_off + 2 * width
    gb_off = ga_off + d
    tm_big = _largest_divisor(m, 1088, 16)

    h = _matmul(xb, w_in, layer, tm=tm_big, tn=512, out_dtype=F32, name="mm_in")

    zero_ret = jnp.zeros((bp, RET_HEADS, RET_DK, RET_DV), F32)
    za_p, ret_p = _retention(h, zero_ret, row0=0, nseq=bp, seqlen=lp, pos0=0.0)
    za_s, ret_s = _retention(h, ret_s0_s, row0=mp, nseq=bs, seqlen=ls, pos0=float(PAST_LEN))
    zb_p, _ = _spatial_gating(h, sgu_ln_g, sgu_ln_b, sgu_w, sgu_b, row0=0, nrows=mp, seqlen=lp,
                              su_off=su_off, width=width, emit_v=False)
    zb_s, sv_s = _spatial_gating(h, sgu_ln_g, sgu_ln_b, sgu_w, sgu_b, row0=mp, nrows=ms, seqlen=ls,
                                 su_off=su_off, width=width, emit_v=True)
    za = jnp.concatenate([za_p, za_s], axis=0)
    zb = jnp.concatenate([zb_p, zb_s], axis=0)

    merged = _merge(za, zb, h, w_a, w_b, layer, ga_off=ga_off, gb_off=gb_off,
                    tm=_largest_divisor(m, 544, 16), tn=512)
    mix = _matmul(merged, w_o, layer, tm=tm_big, tn=512, out_dtype=F32, name="mm_o")
    x1, x1b = _add_ln(x, mix, ln1_g, ln1_b, alpha=alpha, tm=_largest_divisor(m, 256, 8))

    up = _matmul(x1b, w_up, layer, tm=tm_big, tn=512, out_dtype=F32, name="mm_up")
    zero_conv = jnp.zeros((bp, CONV_W - 1, dff), F32)
    z_p = _convgate(up, zero_conv, conv_w, conv_b, row0=0, nseq=bp, seqlen=lp, dff=dff)
    z_s = _convgate(up, conv_s0_s, conv_w, conv_b, row0=mp, nseq=bs, seqlen=ls, dff=dff)
    z = jnp.concatenate([z_p, z_s], axis=0)
    ffn = _matmul(z, w_down, layer, tm=_largest_divisor(m, 272, 16), tn=256, out_dtype=F32,
                  name="mm_down")
    x2, x2b = _add_ln(x1, ffn, ln2_g, ln2_b, alpha=alpha, tm=_largest_divisor(m, 256, 8))

    conv_p = up[:mp].reshape(bp, lp, -1)[:, lp - (CONV_W - 1):, :dff]
    conv_s = up[mp:].reshape(bs, ls, -1)[:, ls - (CONV_W - 1):, :dff]
    return x2, x2b, ret_p, conv_p, ret_s, conv_s, sv_s.reshape(bs, ls, width)


def kernel(x_prompt, x_sample, state_ret, state_conv, w_in, w_a, w_b, w_o, ln1_g, ln1_b,
           sgu_ln_g, sgu_ln_b, sgu_w, sgu_b, w_up, conv_w, conv_b, w_down, ln2_g, ln2_b):
    bp, lp, d = x_prompt.shape
    bs, ls, _ = x_sample.shape
    depth = w_in.shape[0]
    alpha = float((2 * depth) ** 0.25)
    x = jnp.concatenate([x_prompt.reshape(bp * lp, d), x_sample.reshape(bs * ls, d)], axis=0)
    xb = x.astype(BF16)
    big = (w_in, w_a, w_b, w_o, w_up, w_down)
    outs = [[] for _ in range(5)]
    for l in range(depth):
        small = (ln1_g[l], ln1_b[l], sgu_ln_g[l], sgu_ln_b[l], sgu_w[l], sgu_b[l],
                 conv_w[l], conv_b[l], ln2_g[l], ln2_b[l])
        x, xb, *states = _layer(l, x, xb, state_ret[l], state_conv[l], big, small,
                                bp=bp, lp=lp, bs=bs, ls=ls, alpha=alpha)
        for acc, s in zip(outs, states):
            acc.append(s)
    y_prompt = x[:bp * lp].reshape(bp, lp, d)
    y_sample = x[bp * lp:].reshape(bs, ls, d)
    return (y_prompt, y_sample) + tuple(jnp.stack(o) for o in outs)
```

```python
import functools

import jax
import jax.numpy as jnp
from jax import lax
from jax.experimental import pallas as pl
from jax.experimental.pallas import tpu as pltpu

F32 = jnp.float32
BF16 = jnp.bfloat16

RET_HEADS = 8
RET_DK = 256
RET_DV = 256
RET_QK = RET_HEADS * RET_DK
RET_V = RET_HEADS * RET_DV
SGU_GROUPS = 8
SGU_CHUNK = 128
CONV_W = 3
PAST_LEN = 1024
ROPE_BASE = 10000.0
LN_EPS = 1e-5
GN_EPS = 1e-6

V7X_VMEM_BYTES = 64 * 1024 * 1024
LANES = 128
RET_BLOCK = 256


def _vmem_limit(estimate_bytes):
    return int(min(V7X_VMEM_BYTES - (4 << 20), max(estimate_bytes * 5 // 4 + (2 << 20), 16 << 20)))


def _largest_divisor(n, limit, multiple):
    best = None
    d = multiple
    while d <= min(n, limit):
        if n % d == 0:
            best = d
        d += multiple
    assert best is not None, (n, limit, multiple)
    return best


def _mm_body(x_ref, w_ref, o_ref, wbf_ref):
    @pl.when(pl.program_id(1) == 0)
    def _():
        wbf_ref[...] = w_ref[...].astype(BF16)

    o_ref[...] = jnp.dot(x_ref[...], wbf_ref[...], preferred_element_type=F32).astype(o_ref.dtype)


def _matmul(x, w, layer, *, tm, tn, out_dtype, name, w_buffers=2):
    m, k = x.shape
    n = w.shape[2]
    est = (2 * tm * k * 2 + w_buffers * k * tn * 4 + k * tn * 2
           + 2 * tm * tn * jnp.dtype(out_dtype).itemsize)
    return pl.pallas_call(
        _mm_body,
        grid=(n // tn, m // tm),
        in_specs=[pl.BlockSpec((tm, k), lambda j, i: (i, 0)),
                  pl.BlockSpec((None, k, tn), lambda j, i: (layer, 0, j),
                               pipeline_mode=pl.Buffered(w_buffers))],
        out_specs=pl.BlockSpec((tm, tn), lambda j, i: (i, j)),
        out_shape=jax.ShapeDtypeStruct((m, n), out_dtype),
        scratch_shapes=[pltpu.VMEM((k, tn), BF16)],
        compiler_params=pltpu.CompilerParams(
            dimension_semantics=("arbitrary", "arbitrary"), vmem_limit_bytes=_vmem_limit(est)),
        name=name,
    )(x, w)


def _merge_body(za_ref, zb_ref, ga_ref, gb_ref, wa_ref, wb_ref, o_ref, wabf_ref, wbbf_ref):
    @pl.when(pl.program_id(1) == 0)
    def _():
        wabf_ref[...] = wa_ref[...].astype(BF16)
        wbbf_ref[...] = wb_ref[...].astype(BF16)

    a = jnp.dot(za_ref[...], wabf_ref[...], preferred_element_type=F32)
    b = jnp.dot(zb_ref[...], wbbf_ref[...], preferred_element_type=F32)
    o_ref[...] = (jax.nn.sigmoid(ga_ref[...]) * a + jax.nn.sigmoid(gb_ref[...]) * b).astype(o_ref.dtype)


def _merge(za, zb, h, w_a, w_b, layer, *, ga_off, gb_off, tm, tn):
    m, ka = za.shape
    kb = zb.shape[1]
    n = w_a.shape[2]
    ga_blk, gb_blk = ga_off // tn, gb_off // tn
    est = (2 * tm * (ka + kb) * 2 + 2 * (ka + kb) * tn * 4 + (ka + kb) * tn * 2
           + 2 * 2 * tm * tn * 4 + 2 * tm * tn * 2)
    return pl.pallas_call(
        _merge_body,
        grid=(n // tn, m // tm),
        in_specs=[pl.BlockSpec((tm, ka), lambda j, i: (i, 0)),
                  pl.BlockSpec((tm, kb), lambda j, i: (i, 0)),
                  pl.BlockSpec((tm, tn), lambda j, i: (i, ga_blk + j)),
                  pl.BlockSpec((tm, tn), lambda j, i: (i, gb_blk + j)),
                  pl.BlockSpec((None, ka, tn), lambda j, i: (layer, 0, j)),
                  pl.BlockSpec((None, kb, tn), lambda j, i: (layer, 0, j))],
        out_specs=pl.BlockSpec((tm, tn), lambda j, i: (i, j)),
        out_shape=jax.ShapeDtypeStruct((m, n), BF16),
        scratch_shapes=[pltpu.VMEM((ka, tn), BF16), pltpu.VMEM((kb, tn), BF16)],
        compiler_params=pltpu.CompilerParams(
            dimension_semantics=("arbitrary", "arbitrary"), vmem_limit_bytes=_vmem_limit(est)),
        name="merge_mm",
    )(za, zb, h, h, w_a, w_b)


def _ln_body(x_ref, y_ref, g_ref, b_ref, o_ref, obf_ref, *, alpha):
    t = alpha * x_ref[...] + y_ref[...]
    mu = jnp.mean(t, axis=-1, keepdims=True)
    d = t - mu
    var = jnp.mean(d * d, axis=-1, keepdims=True)
    out = d * lax.rsqrt(var + LN_EPS) * g_ref[...] + b_ref[...]
    o_ref[...] = out
    obf_ref[...] = out.astype(BF16)


def _add_ln(x, y, g, b, *, alpha, tm):
    m, d = x.shape
    est = 2 * 2 * tm * d * 4 + 2 * tm * d * 4 + 2 * tm * d * 2
    row = pl.BlockSpec((tm, d), lambda i: (i, 0))
    vec = pl.BlockSpec((1, d), lambda i: (0, 0))
    return pl.pallas_call(
        functools.partial(_ln_body, alpha=alpha),
        grid=(m // tm,),
        in_specs=[row, row, vec, vec],
        out_specs=[row, row],
        out_shape=[jax.ShapeDtypeStruct((m, d), F32), jax.ShapeDtypeStruct((m, d), BF16)],
        compiler_params=pltpu.CompilerParams(
            dimension_semantics=("arbitrary",), vmem_limit_bytes=_vmem_limit(est)),
        name="add_ln",
    )(x, y, g.reshape(1, d), b.reshape(1, d))


def _ret_body(q_ref, k_ref, v_ref, g_ref, cos_ref, sin_ref, lg_ref, s0_ref, za_ref, sout_ref,
              s_ref, d_ref, qd_ref, kd_ref, *, c, hb):
    b = pl.program_id(1)
    ci = pl.program_id(2)

    @pl.when((b == 0) & (ci == 0))
    def _():
        ri = lax.broadcasted_iota(jnp.int32, (c, c), 0)
        cj = lax.broadcasted_iota(jnp.int32, (c, c), 1)
        diff = (ri - cj).astype(F32)
        r = lax.broadcasted_iota(jnp.int32, (c, RET_DK), 0).astype(F32)
        for hh in range(hb):
            lg = lg_ref[hh]
            d_ref[hh] = jnp.where(diff >= 0, jnp.exp(jnp.maximum(diff, 0.0) * lg[:, :c]), 0.0)
            qd_ref[hh] = jnp.exp((r + 1.0) * lg)
            kd_ref[hh] = jnp.exp((c - 1.0 - r) * lg)

    @pl.when(ci == 0)
    def _():
        s_ref[...] = s0_ref[...]

    cos = cos_ref[...]
    sin = sin_ref[...]
    half = RET_DK // 2

    def rope(t):
        t1, t2 = t[:, :half], t[:, half:]
        return jnp.concatenate([t1 * cos - t2 * sin, t1 * sin + t2 * cos], axis=-1)

    for hh in range(hb):
        qc = slice(hh * RET_DK, (hh + 1) * RET_DK)
        vc = slice(hh * RET_DV, (hh + 1) * RET_DV)
        q = rope(q_ref[:, qc])
        k = rope(k_ref[:, qc]) * (RET_DK ** -0.5)
        vb = v_ref[:, vc].astype(BF16)
        s = s_ref[hh]
        scores = lax.dot_general(q.astype(BF16), k.astype(BF16), (((1,), (1,)), ((), ())),
                                 preferred_element_type=F32) * d_ref[hh]
        o = (jnp.dot(scores.astype(BF16), vb, preferred_element_type=F32)
             + jnp.dot((q * qd_ref[hh]).astype(BF16), s.astype(BF16), preferred_element_type=F32))
        kv = lax.dot_general((k * kd_ref[hh]).astype(BF16), vb, (((0,), (0,)), ((), ())),
                             preferred_element_type=F32)
        s_ref[hh] = s * jnp.exp(float(c) * lg_ref[hh]) + kv

        mu = jnp.mean(o, axis=-1, keepdims=True)
        od = o - mu
        var = jnp.mean(od * od, axis=-1, keepdims=True)
        on = od * lax.rsqrt(var + GN_EPS)
        za_ref[:, vc] = (jax.nn.silu(g_ref[:, vc]) * on).astype(BF16)

    @pl.when(ci == pl.num_programs(2) - 1)
    def _():
        sout_ref[...] = s_ref[...]


def _retention(h, s0, *, row0, nseq, seqlen, pos0, hb):
    c = min(RET_BLOCK, seqlen)
    nc = seqlen // c
    rb0 = row0 // c
    hq = RET_HEADS // hb
    half = RET_DK // 2
    pos = pos0 + jnp.arange(seqlen, dtype=F32)
    freqs = ROPE_BASE ** (-jnp.arange(half, dtype=F32) / half)
    ang = pos[:, None] * freqs[None, :]
    cos, sin = jnp.cos(ang), jnp.sin(ang)
    log_gamma = jnp.log(1.0 - 2.0 ** (-5.0 - jnp.arange(RET_HEADS, dtype=F32)))
    lg = jnp.broadcast_to(log_gamma[:, None, None], (RET_HEADS, 1, RET_DK))

    def hcol(off):
        return pl.BlockSpec((c, hb * RET_DK), lambda hh, b, ci: (rb0 + b * nc + ci, off + hh))

    tab = pl.BlockSpec((c, half), lambda hh, b, ci: (ci, 0))
    state = pl.BlockSpec((None, hb, RET_DK, RET_DV), lambda hh, b, ci: (b, hh, 0, 0))
    est = (2 * 4 * c * hb * RET_DK * 4 + 2 * 2 * hb * RET_DK * RET_DV * 4 + hb * RET_DK * RET_DV * 4
           + hb * (c * c + 2 * c * RET_DK) * 4 + 2 * c * hb * RET_DV * 2 + 16 * c * RET_DK * 4)
    return pl.pallas_call(
        functools.partial(_ret_body, c=c, hb=hb),
        grid=(hq, nseq, nc),
        in_specs=[hcol(0), hcol(hq), hcol(2 * hq), hcol(3 * hq), tab, tab,
                  pl.BlockSpec((hb, 1, RET_DK), lambda hh, b, ci: (hh, 0, 0)), state],
        out_specs=[pl.BlockSpec((c, hb * RET_DV), lambda hh, b, ci: (b * nc + ci, hh)), state],
        out_shape=[jax.ShapeDtypeStruct((nseq * seqlen, RET_V), BF16),
                   jax.ShapeDtypeStruct((nseq, RET_HEADS, RET_DK, RET_DV), F32)],
        scratch_shapes=[pltpu.VMEM((hb, RET_DK, RET_DV), F32), pltpu.VMEM((hb, c, c), F32),
                        pltpu.VMEM((hb, c, RET_DK), F32), pltpu.VMEM((hb, c, RET_DK), F32)],
        compiler_params=pltpu.CompilerParams(
            dimension_semantics=("arbitrary", "arbitrary", "arbitrary"),
            vmem_limit_bytes=_vmem_limit(est)),
        name="retention",
    )(h, h, h, h, cos, sin, lg, s0)


def _sgu_body(su_ref, sv_ref, lng_ref, lnb_ref, w_ref, bias_ref, zb_ref, *rest, c, tm, gdim, emit_v):
    sv = jax.nn.gelu(sv_ref[...])
    mu = jnp.mean(sv, axis=-1, keepdims=True)
    d = sv - mu
    var = jnp.mean(d * d, axis=-1, keepdims=True)
    svn = d * lax.rsqrt(var + LN_EPS) * lng_ref[...] + lnb_ref[...]
    if emit_v:
        rest[0][...] = svn
    svb = svn.astype(BF16)
    ri = lax.broadcasted_iota(jnp.int32, (SGU_CHUNK, SGU_CHUNK), 0)
    cj = lax.broadcasted_iota(jnp.int32, (SGU_CHUNK, SGU_CHUNK), 1)
    mask = (ri >= cj) & ((ri // c) == (cj // c))
    for g in range(SGU_GROUPS):
        wm = jnp.where(mask, w_ref[g], 0.0).astype(BF16)
        cols = slice(g * gdim, (g + 1) * gdim)
        for r in range(tm // SGU_CHUNK):
            rows = slice(r * SGU_CHUNK, (r + 1) * SGU_CHUNK)
            mixed = jnp.dot(wm, svb[rows, cols], preferred_element_type=F32) + bias_ref[:, cols]
            zb_ref[rows, cols] = (jax.nn.gelu(su_ref[rows, cols]) * mixed).astype(BF16)


def _spatial_gating(h, ln_g, ln_b, w_s, b_s, *, row0, nrows, seqlen, su_off, width, emit_v):
    c = min(SGU_CHUNK, seqlen)
    rep = SGU_CHUNK // c
    gdim = width // SGU_GROUPS
    w_blk = jnp.tile(w_s[:, :c, :c], (1, rep, rep))
    bias = jnp.repeat(jnp.tile(b_s[:, :c], (1, rep)).T, gdim, axis=1)
    tm = _largest_divisor(nrows, 256, SGU_CHUNK)
    rb0 = row0 // tm
    su_blk = su_off // width
    row = pl.BlockSpec((tm, width), lambda i: (i, 0))
    vec = pl.BlockSpec((1, width), lambda i: (0, 0))
    out_specs = [row]
    out_shape = [jax.ShapeDtypeStruct((nrows, width), BF16)]
    if emit_v:
        out_specs.append(row)
        out_shape.append(jax.ShapeDtypeStruct((nrows, width), F32))
    est = 2 * 2 * tm * width * 4 + 2 * tm * width * (2 + 4) + 8 * tm * width * 4
    res = pl.pallas_call(
        functools.partial(_sgu_body, c=c, tm=tm, gdim=gdim, emit_v=emit_v),
        grid=(nrows // tm,),
        in_specs=[pl.BlockSpec((tm, width), lambda i: (rb0 + i, su_blk)),
                  pl.BlockSpec((tm, width), lambda i: (rb0 + i, su_blk + 1)),
                  vec, vec,
                  pl.BlockSpec((SGU_GROUPS, SGU_CHUNK, SGU_CHUNK), lambda i: (0, 0, 0)),
                  pl.BlockSpec((SGU_CHUNK, width), lambda i: (0, 0))],
        out_specs=out_specs,
        out_shape=out_shape,
        compiler_params=pltpu.CompilerParams(
            dimension_semantics=("arbitrary",), vmem_limit_bytes=_vmem_limit(est)),
        name="spatial_gating",
    )(h, h, ln_g.reshape(1, width), ln_b.reshape(1, width), w_blk, bias)
    return res if emit_v else (res[0], None)


def _rows(ref, slab, start, n, stride):
    if n == 1:
        return ref[slab, start:start + 1, :]
    return ref[slab, pl.ds(start, n, stride=stride), :]


def _set_rows(ref, slab, start, n, stride, value):
    if n == 1:
        ref[slab, start:start + 1, :] = value
    else:
        ref[slab, pl.ds(start, n, stride=stride), :] = value


def _upgate_body(x_ref, wg_ref, wv_ref, cw_ref, cb_ref, st_ref, z_ref, tail_ref,
                 wbf_ref, buf_ref, conv_ref, val_ref, *, tm, tn, starts, ends):
    i = pl.program_id(1)
    slabs = tn // LANES

    @pl.when(i == 0)
    def _():
        wbf_ref[:, :tn] = wg_ref[...].astype(BF16)
        wbf_ref[:, tn:] = wv_ref[...].astype(BF16)
        buf_ref[:, 0:8, :] = jnp.zeros((slabs, 8, LANES), F32)

    @pl.when(i > 0)
    def _():
        buf_ref[:, 0:8, :] = buf_ref[:, tm:tm + 8, :]

    gv = jnp.dot(x_ref[...], wbf_ref[...], preferred_element_type=F32)
    val_ref[...] = gv[:, tn:]
    for s in range(slabs):
        lanes = slice(s * LANES, (s + 1) * LANES)
        buf_ref[s, 8:tm + 8, :] = gv[:, lanes]
        conv_ref[s] = (cb_ref[:, lanes] + buf_ref[s, 6:tm + 6, :] * cw_ref[0:1, lanes]
                       + buf_ref[s, 7:tm + 7, :] * cw_ref[1:2, lanes]
                       + gv[:, lanes] * cw_ref[2:3, lanes])

    for tile, off, n, stride, slot in starts:
        @pl.when(i == tile)
        def _(off=off, n=n, stride=stride, slot=slot):
            for s in range(slabs):
                lanes = slice(s * LANES, (s + 1) * LANES)
                w0, w1, w2 = cw_ref[0:1, lanes], cw_ref[1:2, lanes], cw_ref[2:3, lanes]
                cb = cb_ref[:, lanes]
                p0 = _rows(buf_ref, s, 8 + off, n, stride)
                p1 = _rows(buf_ref, s, 8 + off + 1, n, stride)
                if slot is None:
                    c0 = cb + p0 * w2
                    c1 = cb + p0 * w1 + p1 * w2
                else:
                    s0 = st_ref[0, slot:slot + n, lanes]
                    s1 = st_ref[1, slot:slot + n, lanes]
                    c0 = cb + s0 * w0 + s1 * w1 + p0 * w2
                    c1 = cb + s1 * w0 + p0 * w1 + p1 * w2
                _set_rows(conv_ref, s, off, n, stride, c0)
                _set_rows(conv_ref, s, off + 1, n, stride, c1)

    for tile, off, n, stride, slot in ends:
        @pl.when(i == tile)
        def _(off=off, n=n, stride=stride, slot=slot):
            for s in range(slabs):
                lanes = slice(s * LANES, (s + 1) * LANES)
                tail_ref[0, slot:slot + n, lanes] = _rows(buf_ref, s, 8 + off - 1, n, stride)
                tail_ref[1, slot:slot + n, lanes] = _rows(buf_ref, s, 8 + off, n, stride)

    for s in range(slabs):
        lanes = slice(s * LANES, (s + 1) * LANES)
        z_ref[:, lanes] = (jax.nn.gelu(conv_ref[s]) * val_ref[:, lanes]).astype(BF16)


def _seq_groups(seq_rows, seqlen, tm, within):
    groups = []
    for idx, r0 in enumerate(seq_rows):
        row = r0 + within
        tile, off = divmod(row, tm)
        if groups and groups[-1][0] == tile and seq_rows[idx - 1] + seqlen == r0:
            t, o, n, st, first = groups[-1]
            groups[-1] = (t, o, n + 1, st, first)
        else:
            groups.append((tile, off, 1, seqlen, idx))
    return groups


def _up_gate(xb, w_up, layer, conv_w, conv_b, conv_s0_s, *, bp, lp, bs, ls, tm, tn):
    m, k = xb.shape
    dff = w_up.shape[2] // 2
    vblk = dff // tn
    nseq = bp + bs
    prompt_rows = [b * lp for b in range(bp)]
    sample_rows = [bp * lp + b * ls for b in range(bs)]
    starts, ends = [], []
    for rows, sl, has_state, slot0 in ((sample_rows, ls, True, 0), (prompt_rows, lp, False, bs)):
        for t, o, n, st, first in _seq_groups(rows, sl, tm, 0):
            assert o + (n - 1) * st + 1 < tm, "a sequence's first two rows must share a row tile"
            starts.append((t, o, n, st, first if has_state else None))
        for t, o, n, st, first in _seq_groups(rows, sl, tm, sl - 1):
            assert o >= 1, "a sequence's last two rows must share a row tile"
            ends.append((t, o, n, st, slot0 + first))
    st_in = jnp.transpose(conv_s0_s, (1, 0, 2))
    est = (2 * tm * k * 2 + 2 * 2 * k * tn * 4 + 2 * k * tn * 2 + 2 * (tm + 8) * tn * 4
           + 2 * tm * tn * 2 + 2 * 2 * (bs + nseq) * tn * 4 + 6 * tm * tn * 4)
    return pl.pallas_call(
        functools.partial(_upgate_body, tm=tm, tn=tn, starts=tuple(starts), ends=tuple(ends)),
        grid=(dff // tn, m // tm),
        in_specs=[pl.BlockSpec((tm, k), lambda j, i: (i, 0)),
                  pl.BlockSpec((None, k, tn), lambda j, i: (layer, 0, j)),
                  pl.BlockSpec((None, k, tn), lambda j, i: (layer, 0, vblk + j)),
                  pl.BlockSpec((CONV_W, tn), lambda j, i: (0, j)),
                  pl.BlockSpec((1, tn), lambda j, i: (0, j)),
                  pl.BlockSpec((CONV_W - 1, bs, tn), lambda j, i: (0, 0, j))],
        out_specs=[pl.BlockSpec((tm, tn), lambda j, i: (i, j)),
                   pl.BlockSpec((CONV_W - 1, nseq, tn), lambda j, i: (0, 0, j))],
        out_shape=[jax.ShapeDtypeStruct((m, dff), BF16),
                   jax.ShapeDtypeStruct((CONV_W - 1, nseq, dff), F32)],
        scratch_shapes=[pltpu.VMEM((k, 2 * tn), BF16),
                        pltpu.VMEM((tn // LANES, tm + 8, LANES), F32),
                        pltpu.VMEM((tn // LANES, tm, LANES), F32), pltpu.VMEM((tm, tn), F32)],
        compiler_params=pltpu.CompilerParams(
            dimension_semantics=("arbitrary", "arbitrary"), vmem_limit_bytes=_vmem_limit(est)),
        name="up_gate",
    )(xb, w_up, w_up, conv_w, conv_b.reshape(1, dff), st_in)


def _layer(layer, x, xb, ret_s0_s, conv_s0_s, big, small, *, bp, lp, bs, ls, alpha):
    w_in, w_a, w_b, w_o, w_up, w_down = big
    ln1_g, ln1_b, sgu_ln_g, sgu_ln_b, sgu_w, sgu_b, conv_w, conv_b, ln2_g, ln2_b = small
    m, d = x.shape
    mp = bp * lp
    ms = bs * ls
    width = d // 2
    dff = w_down.shape[1]
    su_off = 2 * RET_QK + 2 * RET_V
    ga_off = su_off + 2 * width
    gb_off = ga_off + d
    tm_big = _largest_divisor(m, 1088, 16)

    h = _matmul(xb, w_in, layer, tm=tm_big, tn=512, out_dtype=F32, name="mm_in")

    zero_ret = jnp.zeros((bp, RET_HEADS, RET_DK, RET_DV), F32)
    za_p, ret_p = _retention(h, zero_ret, row0=0, nseq=bp, seqlen=lp, pos0=0.0, hb=2)
    za_s, ret_s = _retention(h, ret_s0_s, row0=mp, nseq=bs, seqlen=ls, pos0=float(PAST_LEN),
                             hb=RET_HEADS)
    zb_p, _ = _spatial_gating(h, sgu_ln_g, sgu_ln_b, sgu_w, sgu_b, row0=0, nrows=mp, seqlen=lp,
                              su_off=su_off, width=width, emit_v=False)
    zb_s, sv_s = _spatial_gating(h, sgu_ln_g, sgu_ln_b, sgu_w, sgu_b, row0=mp, nrows=ms, seqlen=ls,
                                 su_off=su_off, width=width, emit_v=True)
    za = jnp.concatenate([za_p, za_s], axis=0)
    zb = jnp.concatenate([zb_p, zb_s], axis=0)

    merged = _merge(za, zb, h, w_a, w_b, layer, ga_off=ga_off, gb_off=gb_off,
                    tm=_largest_divisor(m, 544, 16), tn=512)
    mix = _matmul(merged, w_o, layer, tm=tm_big, tn=512, out_dtype=F32, name="mm_o")
    x1, x1b = _add_ln(x, mix, ln1_g, ln1_b, alpha=alpha, tm=_largest_divisor(m, 256, 8))

    z, tails = _up_gate(x1b, w_up, layer, conv_w, conv_b, conv_s0_s, bp=bp, lp=lp, bs=bs, ls=ls,
                        tm=tm_big, tn=_largest_divisor(dff, 256, LANES))
    ffn = _matmul(z, w_down, layer, tm=_largest_divisor(m, 272, 16), tn=512, out_dtype=F32,
                  name="mm_down", w_buffers=1)
    x2, x2b = _add_ln(x1, ffn, ln2_g, ln2_b, alpha=alpha, tm=_largest_divisor(m, 256, 8))

    conv_s = jnp.transpose(tails[:, :bs], (1, 0, 2))
    conv_p = jnp.transpose(tails[:, bs:], (1, 0, 2))
    return x2, x2b, ret_p, conv_p, ret_s, conv_s, sv_s.reshape(bs, ls, width)


def kernel(x_prompt, x_sample, state_ret, state_conv, w_in, w_a, w_b, w_o, ln1_g, ln1_b,
           sgu_ln_g, sgu_ln_b, sgu_w, sgu_b, w_up, conv_w, conv_b, w_down, ln2_g, ln2_b):
    bp, lp, d = x_prompt.shape
    bs, ls, _ = x_sample.shape
    depth = w_in.shape[0]
    alpha = float((2 * depth) ** 0.25)
    x = jnp.concatenate([x_prompt.reshape(bp * lp, d), x_sample.reshape(bs * ls, d)], axis=0)
    xb = x.astype(BF16)
    big = (w_in, w_a, w_b, w_o, w_up, w_down)
    outs = [[] for _ in range(5)]
    for l in range(depth):
        small = (ln1_g[l], ln1_b[l], sgu_ln_g[l], sgu_ln_b[l], sgu_w[l], sgu_b[l],
                 conv_w[l], conv_b[l], ln2_g[l], ln2_b[l])
        x, xb, *states = _layer(l, x, xb, state_ret[l], state_conv[l], big, small,
                                bp=bp, lp=lp, bs=bs, ls=ls, alpha=alpha)
        for acc, s in zip(outs, states):
            acc.append(s)
    y_prompt = x[:bp * lp].reshape(bp, lp, d)
    y_sample = x[bp * lp:].reshape(bs, ls, d)
    return (y_prompt, y_sample) + tuple(jnp.stack(o) for o in outs)
```

```python
import functools
import math

import jax
import jax.numpy as jnp
from jax import lax
from jax.experimental import pallas as pl
from jax.experimental.pallas import tpu as pltpu

F32 = jnp.float32
BF16 = jnp.bfloat16

RET_HEADS = 8
RET_DK = 256
RET_DV = 256
RET_QK = RET_HEADS * RET_DK
RET_V = RET_HEADS * RET_DV
SGU_GROUPS = 8
SGU_CHUNK = 128
CONV_W = 3
PAST_LEN = 1024
ROPE_BASE = 10000.0
LN_EPS = 1e-5
GN_EPS = 1e-6

V7X_VMEM_BYTES = 64 * 1024 * 1024
LANES = 128
RET_BLOCK = 256


def _vmem_limit(estimate_bytes):
    return int(min(V7X_VMEM_BYTES - (4 << 20), max(estimate_bytes * 5 // 4 + (2 << 20), 16 << 20)))


def _largest_divisor(n, limit, multiple):
    best = None
    d = multiple
    while d <= min(n, limit):
        if n % d == 0:
            best = d
        d += multiple
    assert best is not None, (n, limit, multiple)
    return best


def _mm_body(x_ref, w_ref, *rest, res_scale):
    o_ref, wbf_ref = rest[-2], rest[-1]

    @pl.when(pl.program_id(1) == 0)
    def _():
        wbf_ref[...] = w_ref[...].astype(BF16)

    acc = jnp.dot(x_ref[...], wbf_ref[...], preferred_element_type=F32)
    if res_scale is not None:
        acc = res_scale * rest[0][...] + acc
    o_ref[...] = acc.astype(o_ref.dtype)


def _matmul(x, w, layer, *, tm, tn, out_dtype, name, w_buffers=2, residual=None, res_scale=None):
    m, k = x.shape
    n = w.shape[2]
    est = (2 * tm * k * 2 + w_buffers * k * tn * 4 + k * tn * 2
           + 2 * tm * tn * jnp.dtype(out_dtype).itemsize)
    in_specs = [pl.BlockSpec((tm, k), lambda j, i: (i, 0)),
                pl.BlockSpec((None, k, tn), lambda j, i: (layer, 0, j),
                             pipeline_mode=pl.Buffered(w_buffers))]
    args = [x, w]
    if residual is not None:
        in_specs.append(pl.BlockSpec((tm, tn), lambda j, i: (i, j)))
        args.append(residual)
        est += 2 * tm * tn * 4
    return pl.pallas_call(
        functools.partial(_mm_body, res_scale=res_scale if residual is not None else None),
        grid=(n // tn, m // tm),
        in_specs=in_specs,
        out_specs=pl.BlockSpec((tm, tn), lambda j, i: (i, j)),
        out_shape=jax.ShapeDtypeStruct((m, n), out_dtype),
        scratch_shapes=[pltpu.VMEM((k, tn), BF16)],
        compiler_params=pltpu.CompilerParams(
            dimension_semantics=("arbitrary", "arbitrary"), vmem_limit_bytes=_vmem_limit(est)),
        name=name,
    )(*args)


def _merge_body(za_ref, zb_ref, ga_ref, gb_ref, wa_ref, wb_ref, o_ref, wabf_ref, wbbf_ref):
    @pl.when(pl.program_id(1) == 0)
    def _():
        wabf_ref[...] = wa_ref[...].astype(BF16)
        wbbf_ref[...] = wb_ref[...].astype(BF16)

    a = jnp.dot(za_ref[...], wabf_ref[...], preferred_element_type=F32)
    b = jnp.dot(zb_ref[...], wbbf_ref[...], preferred_element_type=F32)
    o_ref[...] = (jax.nn.sigmoid(ga_ref[...]) * a + jax.nn.sigmoid(gb_ref[...]) * b).astype(o_ref.dtype)


def _merge(za, zb, h, w_a, w_b, layer, *, ga_off, gb_off, tm, tn):
    m, ka = za.shape
    kb = zb.shape[1]
    n = w_a.shape[2]
    ga_blk, gb_blk = ga_off // tn, gb_off // tn
    est = (2 * tm * (ka + kb) * 2 + 2 * (ka + kb) * tn * 4 + (ka + kb) * tn * 2
           + 2 * 2 * tm * tn * 4 + 2 * tm * tn * 2)
    return pl.pallas_call(
        _merge_body,
        grid=(n // tn, m // tm),
        in_specs=[pl.BlockSpec((tm, ka), lambda j, i: (i, 0)),
                  pl.BlockSpec((tm, kb), lambda j, i: (i, 0)),
                  pl.BlockSpec((tm, tn), lambda j, i: (i, ga_blk + j)),
                  pl.BlockSpec((tm, tn), lambda j, i: (i, gb_blk + j)),
                  pl.BlockSpec((None, ka, tn), lambda j, i: (layer, 0, j)),
                  pl.BlockSpec((None, kb, tn), lambda j, i: (layer, 0, j))],
        out_specs=pl.BlockSpec((tm, tn), lambda j, i: (i, j)),
        out_shape=jax.ShapeDtypeStruct((m, n), BF16),
        scratch_shapes=[pltpu.VMEM((ka, tn), BF16), pltpu.VMEM((kb, tn), BF16)],
        compiler_params=pltpu.CompilerParams(
            dimension_semantics=("arbitrary", "arbitrary"), vmem_limit_bytes=_vmem_limit(est)),
        name="merge_mm",
    )(za, zb, h, h, w_a, w_b)


def _ln_body(t_ref, g_ref, b_ref, o_ref, *maybe_obf_ref):
    t = t_ref[...]
    mu = jnp.mean(t, axis=-1, keepdims=True)
    d = t - mu
    var = jnp.mean(d * d, axis=-1, keepdims=True)
    out = d * lax.rsqrt(var + LN_EPS) * g_ref[...] + b_ref[...]
    o_ref[...] = out
    for obf_ref in maybe_obf_ref:
        obf_ref[...] = out.astype(BF16)


def _layer_norm(t, g, b, *, row0, nrows, tm, emit_bf16):
    d = t.shape[1]
    rb0 = row0 // tm
    assert row0 % tm == 0 and nrows % tm == 0
    est = 2 * tm * d * 4 + 2 * tm * d * 4 + 2 * tm * d * 2 + 4 * tm * d * 4
    row = pl.BlockSpec((tm, d), lambda i: (i, 0))
    vec = pl.BlockSpec((1, d), lambda i: (0, 0))
    out_specs = [row]
    out_shape = [jax.ShapeDtypeStruct((nrows, d), F32)]
    if emit_bf16:
        out_specs.append(row)
        out_shape.append(jax.ShapeDtypeStruct((nrows, d), BF16))
    return pl.pallas_call(
        _ln_body,
        grid=(nrows // tm,),
        in_specs=[pl.BlockSpec((tm, d), lambda i: (rb0 + i, 0)), vec, vec],
        out_specs=out_specs,
        out_shape=out_shape,
        compiler_params=pltpu.CompilerParams(
            dimension_semantics=("arbitrary",), vmem_limit_bytes=_vmem_limit(est)),
        name="layer_norm",
    )(t, g.reshape(1, d), b.reshape(1, d))


def _ret_body(q_ref, k_ref, v_ref, g_ref, cos_ref, sin_ref, lg_ref, s0_ref, za_ref, sout_ref,
              s_ref, d_ref, qd_ref, kd_ref, *, c, hb):
    b = pl.program_id(1)
    ci = pl.program_id(2)

    @pl.when((b == 0) & (ci == 0))
    def _():
        ri = lax.broadcasted_iota(jnp.int32, (c, c), 0)
        cj = lax.broadcasted_iota(jnp.int32, (c, c), 1)
        diff = (ri - cj).astype(F32)
        r = lax.broadcasted_iota(jnp.int32, (c, RET_DK), 0).astype(F32)
        for hh in range(hb):
            lg = lg_ref[hh]
            d_ref[hh] = jnp.where(diff >= 0, jnp.exp(jnp.maximum(diff, 0.0) * lg[:, :c]), 0.0)
            qd_ref[hh] = jnp.exp((r + 1.0) * lg)
            kd_ref[hh] = jnp.exp((c - 1.0 - r) * lg)

    @pl.when(ci == 0)
    def _():
        s_ref[...] = s0_ref[...]

    cos = cos_ref[...]
    sin = sin_ref[...]
    half = RET_DK // 2

    def rope(t):
        t1, t2 = t[:, :half], t[:, half:]
        return jnp.concatenate([t1 * cos - t2 * sin, t1 * sin + t2 * cos], axis=-1)

    for hh in range(hb):
        qc = slice(hh * RET_DK, (hh + 1) * RET_DK)
        vc = slice(hh * RET_DV, (hh + 1) * RET_DV)
        q = rope(q_ref[:, qc])
        k = rope(k_ref[:, qc]) * (RET_DK ** -0.5)
        vb = v_ref[:, vc].astype(BF16)
        s = s_ref[hh]
        scores = lax.dot_general(q.astype(BF16), k.astype(BF16), (((1,), (1,)), ((), ())),
                                 preferred_element_type=F32) * d_ref[hh]
        o = (jnp.dot(scores.astype(BF16), vb, preferred_element_type=F32)
             + jnp.dot((q * qd_ref[hh]).astype(BF16), s.astype(BF16), preferred_element_type=F32))
        kv = lax.dot_general((k * kd_ref[hh]).astype(BF16), vb, (((0,), (0,)), ((), ())),
                             preferred_element_type=F32)
        s_ref[hh] = s * jnp.exp(float(c) * lg_ref[hh]) + kv

        mu = jnp.mean(o, axis=-1, keepdims=True)
        od = o - mu
        var = jnp.mean(od * od, axis=-1, keepdims=True)
        on = od * lax.rsqrt(var + GN_EPS)
        za_ref[:, vc] = (jax.nn.silu(g_ref[:, vc]) * on).astype(BF16)

    @pl.when(ci == pl.num_programs(2) - 1)
    def _():
        sout_ref[...] = s_ref[...]


def _drop_leading(n, body):
    def wrapped(*refs):
        return body(*refs[n:])
    return wrapped


def _retention(h, s0, s0_layer, *, row0, nseq, seqlen, pos0, hb, za_into, st_into, st_layer,
               depth):
    m = h.shape[0]
    c = min(RET_BLOCK, seqlen)
    nc = seqlen // c
    rb0 = row0 // c
    hq = RET_HEADS // hb
    half = RET_DK // 2
    pos = pos0 + jnp.arange(seqlen, dtype=F32)
    freqs = ROPE_BASE ** (-jnp.arange(half, dtype=F32) / half)
    ang = pos[:, None] * freqs[None, :]
    cos, sin = jnp.cos(ang), jnp.sin(ang)
    log_gamma = jnp.log(1.0 - 2.0 ** (-5.0 - jnp.arange(RET_HEADS, dtype=F32)))
    lg = jnp.broadcast_to(log_gamma[:, None, None], (RET_HEADS, 1, RET_DK))

    def hcol(off):
        return pl.BlockSpec((c, hb * RET_DK), lambda hh, b, ci: (rb0 + b * nc + ci, off + hh))

    tab = pl.BlockSpec((c, half), lambda hh, b, ci: (ci, 0))

    def state(layer):
        return pl.BlockSpec((None, None, hb, RET_DK, RET_DV),
                            lambda hh, b, ci: (layer, b, hh, 0, 0))

    donated = [a for a in (za_into, st_into) if a is not None]
    aliases = {}
    if za_into is not None:
        aliases[0] = 0
    if st_into is not None:
        aliases[len(donated) - 1] = 1
    est = (2 * 4 * c * hb * RET_DK * 4 + 2 * 2 * hb * RET_DK * RET_DV * 4 + hb * RET_DK * RET_DV * 4
           + hb * (c * c + 2 * c * RET_DK) * 4 + 2 * c * hb * RET_DV * 2 + 16 * c * RET_DK * 4)
    return pl.pallas_call(
        _drop_leading(len(donated), functools.partial(_ret_body, c=c, hb=hb)),
        grid=(hq, nseq, nc),
        in_specs=[pl.BlockSpec(memory_space=pl.ANY)] * len(donated)
        + [hcol(0), hcol(hq), hcol(2 * hq), hcol(3 * hq), tab, tab,
           pl.BlockSpec((hb, 1, RET_DK), lambda hh, b, ci: (hh, 0, 0)), state(s0_layer)],
        out_specs=[pl.BlockSpec((c, hb * RET_DV), lambda hh, b, ci: (rb0 + b * nc + ci, hh)),
                   state(st_layer)],
        out_shape=[jax.ShapeDtypeStruct((m, RET_V), BF16),
                   jax.ShapeDtypeStruct((depth, nseq, RET_HEADS, RET_DK, RET_DV), F32)],
        input_output_aliases=aliases,
        scratch_shapes=[pltpu.VMEM((hb, RET_DK, RET_DV), F32), pltpu.VMEM((hb, c, c), F32),
                        pltpu.VMEM((hb, c, RET_DK), F32), pltpu.VMEM((hb, c, RET_DK), F32)],
        compiler_params=pltpu.CompilerParams(
            dimension_semantics=("arbitrary", "arbitrary", "arbitrary"),
            vmem_limit_bytes=_vmem_limit(est)),
        name="retention",
    )(*donated, h, h, h, h, cos, sin, lg, s0)


def _sgu_body(su_ref, sv_ref, lng_ref, lnb_ref, w_ref, bias_ref, zb_ref, *rest, c, tm, gdim, emit_v):
    sv = jax.nn.gelu(sv_ref[...])
    mu = jnp.mean(sv, axis=-1, keepdims=True)
    d = sv - mu
    var = jnp.mean(d * d, axis=-1, keepdims=True)
    svn = d * lax.rsqrt(var + LN_EPS) * lng_ref[...] + lnb_ref[...]
    if emit_v:
        rest[0][...] = svn
    svb = svn.astype(BF16)
    ri = lax.broadcasted_iota(jnp.int32, (SGU_CHUNK, SGU_CHUNK), 0)
    cj = lax.broadcasted_iota(jnp.int32, (SGU_CHUNK, SGU_CHUNK), 1)
    mask = (ri >= cj) & ((ri // c) == (cj // c))
    for g in range(SGU_GROUPS):
        wm = jnp.where(mask, w_ref[g], 0.0).astype(BF16)
        cols = slice(g * gdim, (g + 1) * gdim)
        for r in range(tm // SGU_CHUNK):
            rows = slice(r * SGU_CHUNK, (r + 1) * SGU_CHUNK)
            mixed = jnp.dot(wm, svb[rows, cols], preferred_element_type=F32) + bias_ref[:, cols]
            zb_ref[rows, cols] = (jax.nn.gelu(su_ref[rows, cols]) * mixed).astype(BF16)


def _spatial_gating(h, ln_g, ln_b, w_s, b_s, *, row0, nrows, seqlen, su_off, width, emit_v,
                    zb_into):
    m = h.shape[0]
    c = min(SGU_CHUNK, seqlen)
    rep = SGU_CHUNK // c
    gdim = width // SGU_GROUPS
    w_blk = jnp.tile(w_s[:, :c, :c], (1, rep, rep))
    bias = jnp.repeat(jnp.tile(b_s[:, :c], (1, rep)).T, gdim, axis=1)
    tm = _largest_divisor(nrows, 256, SGU_CHUNK)
    rb0 = row0 // tm
    su_blk = su_off // width
    assert row0 % tm == 0
    vec = pl.BlockSpec((1, width), lambda i: (0, 0))
    out_specs = [pl.BlockSpec((tm, width), lambda i: (rb0 + i, 0))]
    out_shape = [jax.ShapeDtypeStruct((m, width), BF16)]
    if emit_v:
        out_specs.append(pl.BlockSpec((tm, width), lambda i: (i, 0)))
        out_shape.append(jax.ShapeDtypeStruct((nrows, width), F32))
    donated = [] if zb_into is None else [zb_into]
    est = 2 * 2 * tm * width * 4 + 2 * tm * width * (2 + 4) + 8 * tm * width * 4
    res = pl.pallas_call(
        _drop_leading(len(donated),
                      functools.partial(_sgu_body, c=c, tm=tm, gdim=gdim, emit_v=emit_v)),
        grid=(nrows // tm,),
        in_specs=[pl.BlockSpec(memory_space=pl.ANY)] * len(donated)
        + [pl.BlockSpec((tm, width), lambda i: (rb0 + i, su_blk)),
           pl.BlockSpec((tm, width), lambda i: (rb0 + i, su_blk + 1)),
           vec, vec,
           pl.BlockSpec((SGU_GROUPS, SGU_CHUNK, SGU_CHUNK), lambda i: (0, 0, 0)),
           pl.BlockSpec((SGU_CHUNK, width), lambda i: (0, 0))],
        out_specs=out_specs,
        out_shape=out_shape,
        input_output_aliases={0: 0} if donated else {},
        compiler_params=pltpu.CompilerParams(
            dimension_semantics=("arbitrary",), vmem_limit_bytes=_vmem_limit(est)),
        name="spatial_gating",
    )(*donated, h, h, ln_g.reshape(1, width), ln_b.reshape(1, width), w_blk, bias)
    return res if emit_v else (res[0], None)


def _rows(ref, slab, start, n, stride):
    if n == 1:
        return ref[slab, start:start + 1, :]
    return ref[slab, pl.ds(start, n, stride=stride), :]


def _set_rows(ref, slab, start, n, stride, value):
    if n == 1:
        ref[slab, start:start + 1, :] = value
    else:
        ref[slab, pl.ds(start, n, stride=stride), :] = value


def _upgate_body(x_ref, wg_ref, wv_ref, cw_ref, cb_ref, st_ref, z_ref, tail_ref,
                 wbf_ref, buf_ref, conv_ref, val_ref, *, tm, tn, starts, ends):
    i = pl.program_id(1)
    slabs = tn // LANES

    @pl.when(i == 0)
    def _():
        wbf_ref[:, :tn] = wg_ref[...].astype(BF16)
        wbf_ref[:, tn:] = wv_ref[...].astype(BF16)
        buf_ref[:, 0:8, :] = jnp.zeros((slabs, 8, LANES), F32)

    @pl.when(i > 0)
    def _():
        buf_ref[:, 0:8, :] = buf_ref[:, tm:tm + 8, :]

    gv = jnp.dot(x_ref[...], wbf_ref[...], preferred_element_type=F32)
    val_ref[...] = gv[:, tn:]
    for s in range(slabs):
        lanes = slice(s * LANES, (s + 1) * LANES)
        buf_ref[s, 8:tm + 8, :] = gv[:, lanes]
        conv_ref[s] = (cb_ref[:, lanes] + buf_ref[s, 6:tm + 6, :] * cw_ref[0:1, lanes]
                       + buf_ref[s, 7:tm + 7, :] * cw_ref[1:2, lanes]
                       + gv[:, lanes] * cw_ref[2:3, lanes])

    for tile, off, n, stride, slot in starts:
        @pl.when(i == tile)
        def _(off=off, n=n, stride=stride, slot=slot):
            for s in range(slabs):
                lanes = slice(s * LANES, (s + 1) * LANES)
                w0, w1, w2 = cw_ref[0:1, lanes], cw_ref[1:2, lanes], cw_ref[2:3, lanes]
                cb = cb_ref[:, lanes]
                p0 = _rows(buf_ref, s, 8 + off, n, stride)
                p1 = _rows(buf_ref, s, 8 + off + 1, n, stride)
                if slot is None:
                    c0 = cb + p0 * w2
                    c1 = cb + p0 * w1 + p1 * w2
                else:
                    s0 = st_ref[0, slot:slot + n, lanes]
                    s1 = st_ref[1, slot:slot + n, lanes]
                    c0 = cb + s0 * w0 + s1 * w1 + p0 * w2
                    c1 = cb + s1 * w0 + p0 * w1 + p1 * w2
                _set_rows(conv_ref, s, off, n, stride, c0)
                _set_rows(conv_ref, s, off + 1, n, stride, c1)

    for tile, off, n, stride, slot in ends:
        @pl.when(i == tile)
        def _(off=off, n=n, stride=stride, slot=slot):
            for s in range(slabs):
                lanes = slice(s * LANES, (s + 1) * LANES)
                tail_ref[0, slot:slot + n, lanes] = _rows(buf_ref, s, 8 + off - 1, n, stride)
                tail_ref[1, slot:slot + n, lanes] = _rows(buf_ref, s, 8 + off, n, stride)

    for s in range(slabs):
        lanes = slice(s * LANES, (s + 1) * LANES)
        z_ref[:, lanes] = (jax.nn.gelu(conv_ref[s]) * val_ref[:, lanes]).astype(BF16)


def _seq_groups(seq_rows, seqlen, tm, within):
    groups = []
    for idx, r0 in enumerate(seq_rows):
        row = r0 + within
        tile, off = divmod(row, tm)
        if groups and groups[-1][0] == tile and seq_rows[idx - 1] + seqlen == r0:
            t, o, n, st, first = groups[-1]
            groups[-1] = (t, o, n + 1, st, first)
        else:
            groups.append((tile, off, 1, seqlen, idx))
    return groups


def _up_gate(xb, w_up, layer, conv_w, conv_b, conv_s0_s, *, bp, lp, bs, ls, tm, tn):
    m, k = xb.shape
    dff = w_up.shape[2] // 2
    vblk = dff // tn
    nseq = bp + bs
    prompt_rows = [b * lp for b in range(bp)]
    sample_rows = [bp * lp + b * ls for b in range(bs)]
    starts, ends = [], []
    for rows, sl, has_state, slot0 in ((sample_rows, ls, True, 0), (prompt_rows, lp, False, bs)):
        for t, o, n, st, first in _seq_groups(rows, sl, tm, 0):
            assert o + (n - 1) * st + 1 < tm, "a sequence's first two rows must share a row tile"
            starts.append((t, o, n, st, first if has_state else None))
        for t, o, n, st, first in _seq_groups(rows, sl, tm, sl - 1):
            assert o >= 1, "a sequence's last two rows must share a row tile"
            ends.append((t, o, n, st, slot0 + first))
    st_in = jnp.transpose(conv_s0_s, (1, 0, 2))
    est = (2 * tm * k * 2 + 2 * 2 * k * tn * 4 + 2 * k * tn * 2 + 2 * (tm + 8) * tn * 4
           + 2 * tm * tn * 2 + 2 * 2 * (bs + nseq) * tn * 4 + 6 * tm * tn * 4)
    return pl.pallas_call(
        functools.partial(_upgate_body, tm=tm, tn=tn, starts=tuple(starts), ends=tuple(ends)),
        grid=(dff // tn, m // tm),
        in_specs=[pl.BlockSpec((tm, k), lambda j, i: (i, 0)),
                  pl.BlockSpec((None, k, tn), lambda j, i: (layer, 0, j)),
                  pl.BlockSpec((None, k, tn), lambda j, i: (layer, 0, vblk + j)),
                  pl.BlockSpec((CONV_W, tn), lambda j, i: (0, j)),
                  pl.BlockSpec((1, tn), lambda j, i: (0, j)),
                  pl.BlockSpec((CONV_W - 1, bs, tn), lambda j, i: (0, 0, j))],
        out_specs=[pl.BlockSpec((tm, tn), lambda j, i: (i, j)),
                   pl.BlockSpec((CONV_W - 1, nseq, tn), lambda j, i: (0, 0, j))],
        out_shape=[jax.ShapeDtypeStruct((m, dff), BF16),
                   jax.ShapeDtypeStruct((CONV_W - 1, nseq, dff), F32)],
        scratch_shapes=[pltpu.VMEM((k, 2 * tn), BF16),
                        pltpu.VMEM((tn // LANES, tm + 8, LANES), F32),
                        pltpu.VMEM((tn // LANES, tm, LANES), F32), pltpu.VMEM((tm, tn), F32)],
        compiler_params=pltpu.CompilerParams(
            dimension_semantics=("arbitrary", "arbitrary"), vmem_limit_bytes=_vmem_limit(est)),
        name="up_gate",
    )(xb, w_up, w_up, conv_w, conv_b.reshape(1, dff), st_in)


def _layer(layer, depth, x, xb, state_ret, conv_s0_s, ret_bufs, big, small, *, bp, lp, bs, ls,
           alpha, last):
    w_in, w_a, w_b, w_o, w_up, w_down = big
    ln1_g, ln1_b, sgu_ln_g, sgu_ln_b, sgu_w, sgu_b, conv_w, conv_b, ln2_g, ln2_b = small
    m, d = x.shape
    mp = bp * lp
    ms = bs * ls
    width = d // 2
    dff = w_down.shape[1]
    su_off = 2 * RET_QK + 2 * RET_V
    ga_off = su_off + 2 * width
    gb_off = ga_off + d
    tm_big = _largest_divisor(m, 1088, 16)

    h = _matmul(xb, w_in, layer, tm=tm_big, tn=512, out_dtype=F32, name="mm_in")

    zero_ret = jnp.zeros((1, bp, RET_HEADS, RET_DK, RET_DV), F32)
    za, ret_p = _retention(h, zero_ret, 0, row0=0, nseq=bp, seqlen=lp, pos0=0.0, hb=2,
                           za_into=None, st_into=ret_bufs[0], st_layer=layer, depth=depth)
    za, ret_s = _retention(h, state_ret, layer, row0=mp, nseq=bs, seqlen=ls,
                           pos0=float(PAST_LEN), hb=RET_HEADS,
                           za_into=za, st_into=ret_bufs[1], st_layer=layer, depth=depth)
    zb, _ = _spatial_gating(h, sgu_ln_g, sgu_ln_b, sgu_w, sgu_b, row0=0, nrows=mp, seqlen=lp,
                            su_off=su_off, width=width, emit_v=False, zb_into=None)
    zb, sv_s = _spatial_gating(h, sgu_ln_g, sgu_ln_b, sgu_w, sgu_b, row0=mp, nrows=ms, seqlen=ls,
                               su_off=su_off, width=width, emit_v=True, zb_into=zb)

    merged = _merge(za, zb, h, w_a, w_b, layer, ga_off=ga_off, gb_off=gb_off, tm=tm_big, tn=512)
    t1 = _matmul(merged, w_o, layer, tm=tm_big, tn=512, out_dtype=F32, name="mm_o",
                 residual=x, res_scale=alpha)
    tm_ln = _largest_divisor(math.gcd(mp, ms), 256, 8)
    x1, x1b = _layer_norm(t1, ln1_g, ln1_b, row0=0, nrows=m, tm=tm_ln, emit_bf16=True)

    z, tails = _up_gate(x1b, w_up, layer, conv_w, conv_b, conv_s0_s, bp=bp, lp=lp, bs=bs, ls=ls,
                        tm=tm_big, tn=_largest_divisor(dff, 256, LANES))
    t2 = _matmul(z, w_down, layer, tm=_largest_divisor(m, 272, 16), tn=512, out_dtype=F32,
                 name="mm_down", w_buffers=1, residual=x1, res_scale=alpha)
    if last:
        (y_p,) = _layer_norm(t2, ln2_g, ln2_b, row0=0, nrows=mp, tm=tm_ln, emit_bf16=False)
        (y_s,) = _layer_norm(t2, ln2_g, ln2_b, row0=mp, nrows=ms, tm=tm_ln, emit_bf16=False)
        x2, x2b = (y_p, y_s), None
    else:
        x2, x2b = _layer_norm(t2, ln2_g, ln2_b, row0=0, nrows=m, tm=tm_ln, emit_bf16=True)

    conv_s = jnp.transpose(tails[:, :bs], (1, 0, 2))
    conv_p = jnp.transpose(tails[:, bs:], (1, 0, 2))
    return x2, x2b, (ret_p, ret_s), conv_p, conv_s, sv_s.reshape(bs, ls, width)


def kernel(x_prompt, x_sample, state_ret, state_conv, w_in, w_a, w_b, w_o, ln1_g, ln1_b,
           sgu_ln_g, sgu_ln_b, sgu_w, sgu_b, w_up, conv_w, conv_b, w_down, ln2_g, ln2_b):
    bp, lp, d = x_prompt.shape
    bs, ls, _ = x_sample.shape
    depth = w_in.shape[0]
    alpha = float((2 * depth) ** 0.25)
    x = jnp.concatenate([x_prompt.reshape(bp * lp, d), x_sample.reshape(bs * ls, d)], axis=0)
    xb = x.astype(BF16)
    big = (w_in, w_a, w_b, w_o, w_up, w_down)
    ret_bufs = (None, None)
    conv_p, conv_s, sgu_v = [], [], []
    for l in range(depth):
        small = (ln1_g[l], ln1_b[l], sgu_ln_g[l], sgu_ln_b[l], sgu_w[l], sgu_b[l],
                 conv_w[l], conv_b[l], ln2_g[l], ln2_b[l])
        x, xb, ret_bufs, cp, cs, sv = _layer(
            l, depth, x, xb, state_ret, state_conv[l], ret_bufs, big, small,
            bp=bp, lp=lp, bs=bs, ls=ls, alpha=alpha, last=l == depth - 1)
        conv_p.append(cp)
        conv_s.append(cs)
        sgu_v.append(sv)
    y_p, y_s = x
    return (y_p.reshape(bp, lp, d), y_s.reshape(bs, ls, d), ret_bufs[0], jnp.stack(conv_p),
            ret_bufs[1], jnp.stack(conv_s), jnp.stack(sgu_v))
```

```python
import functools
import math

import jax
import jax.numpy as jnp
from jax import lax
from jax.experimental import pallas as pl
from jax.experimental.pallas import tpu as pltpu

F32 = jnp.float32
BF16 = jnp.bfloat16

RET_HEADS = 8
RET_DK = 256
RET_DV = 256
RET_QK = RET_HEADS * RET_DK
RET_V = RET_HEADS * RET_DV
SGU_GROUPS = 8
SGU_CHUNK = 128
CONV_W = 3
PAST_LEN = 1024
ROPE_BASE = 10000.0
LN_EPS = 1e-5
GN_EPS = 1e-6

V7X_VMEM_BYTES = 64 * 1024 * 1024
LANES = 128
RET_BLOCK = 256


def _vmem_limit(estimate_bytes):
    return int(min(V7X_VMEM_BYTES - (4 << 20), max(estimate_bytes * 5 // 4 + (2 << 20), 16 << 20)))


def _largest_divisor(n, limit, multiple):
    best = None
    d = multiple
    while d <= min(n, limit):
        if n % d == 0:
            best = d
        d += multiple
    assert best is not None, (n, limit, multiple)
    return best


def _mm_body(x_ref, w_ref, *rest, res_scale):
    o_ref, wbf_ref = rest[-2], rest[-1]

    @pl.when(pl.program_id(1) == 0)
    def _():
        wbf_ref[...] = w_ref[...].astype(BF16)

    acc = jnp.dot(x_ref[...], wbf_ref[...], preferred_element_type=F32)
    if res_scale is not None:
        acc = res_scale * rest[0][...] + acc
    o_ref[...] = acc.astype(o_ref.dtype)


def _matmul(x, w, layer, *, tm, tn, out_dtype, name, w_buffers=2, residual=None, res_scale=None):
    m, k = x.shape
    n = w.shape[2]
    est = (2 * tm * k * 2 + w_buffers * k * tn * 4 + k * tn * 2
           + 2 * tm * tn * jnp.dtype(out_dtype).itemsize)
    in_specs = [pl.BlockSpec((tm, k), lambda j, i: (i, 0)),
                pl.BlockSpec((None, k, tn), lambda j, i: (layer, 0, j),
                             pipeline_mode=pl.Buffered(w_buffers))]
    args = [x, w]
    if residual is not None:
        in_specs.append(pl.BlockSpec((tm, tn), lambda j, i: (i, j)))
        args.append(residual)
        est += 2 * tm * tn * 4
    return pl.pallas_call(
        functools.partial(_mm_body, res_scale=res_scale if residual is not None else None),
        grid=(n // tn, m // tm),
        in_specs=in_specs,
        out_specs=pl.BlockSpec((tm, tn), lambda j, i: (i, j)),
        out_shape=jax.ShapeDtypeStruct((m, n), out_dtype),
        scratch_shapes=[pltpu.VMEM((k, tn), BF16)],
        compiler_params=pltpu.CompilerParams(
            dimension_semantics=("arbitrary", "arbitrary"), vmem_limit_bytes=_vmem_limit(est)),
        name=name,
    )(*args)


def _merge_body(za_ref, zb_ref, ga_ref, gb_ref, wa_ref, wb_ref, o_ref, wabf_ref, wbbf_ref):
    @pl.when(pl.program_id(1) == 0)
    def _():
        wabf_ref[...] = wa_ref[...].astype(BF16)
        wbbf_ref[...] = wb_ref[...].astype(BF16)

    a = jnp.dot(za_ref[...], wabf_ref[...], preferred_element_type=F32)
    b = jnp.dot(zb_ref[...], wbbf_ref[...], preferred_element_type=F32)
    o_ref[...] = (jax.nn.sigmoid(ga_ref[...]) * a + jax.nn.sigmoid(gb_ref[...]) * b).astype(o_ref.dtype)


def _merge(za, zb, h, w_a, w_b, layer, *, ga_off, gb_off, tm, tn):
    m, ka = za.shape
    kb = zb.shape[1]
    n = w_a.shape[2]
    ga_blk, gb_blk = ga_off // tn, gb_off // tn
    est = (2 * tm * (ka + kb) * 2 + 2 * (ka + kb) * tn * 4 + (ka + kb) * tn * 2
           + 2 * 2 * tm * tn * 4 + 2 * tm * tn * 2)
    return pl.pallas_call(
        _merge_body,
        grid=(n // tn, m // tm),
        in_specs=[pl.BlockSpec((tm, ka), lambda j, i: (i, 0)),
                  pl.BlockSpec((tm, kb), lambda j, i: (i, 0)),
                  pl.BlockSpec((tm, tn), lambda j, i: (i, ga_blk + j)),
                  pl.BlockSpec((tm, tn), lambda j, i: (i, gb_blk + j)),
                  pl.BlockSpec((None, ka, tn), lambda j, i: (layer, 0, j)),
                  pl.BlockSpec((None, kb, tn), lambda j, i: (layer, 0, j))],
        out_specs=pl.BlockSpec((tm, tn), lambda j, i: (i, j)),
        out_shape=jax.ShapeDtypeStruct((m, n), BF16),
        scratch_shapes=[pltpu.VMEM((ka, tn), BF16), pltpu.VMEM((kb, tn), BF16)],
        compiler_params=pltpu.CompilerParams(
            dimension_semantics=("arbitrary", "arbitrary"), vmem_limit_bytes=_vmem_limit(est)),
        name="merge_mm",
    )(za, zb, h, h, w_a, w_b)


def _ln_body(t_ref, g_ref, b_ref, o_ref, *maybe_obf_ref):
    t = t_ref[...]
    mu = jnp.mean(t, axis=-1, keepdims=True)
    d = t - mu
    var = jnp.mean(d * d, axis=-1, keepdims=True)
    out = d * lax.rsqrt(var + LN_EPS) * g_ref[...] + b_ref[...]
    o_ref[...] = out
    for obf_ref in maybe_obf_ref:
        obf_ref[...] = out.astype(BF16)


def _layer_norm(t, g, b, *, row0, nrows, tm, emit_bf16):
    d = t.shape[1]
    rb0 = row0 // tm
    assert row0 % tm == 0 and nrows % tm == 0
    est = 2 * tm * d * 4 + 2 * tm * d * 4 + 2 * tm * d * 2 + 4 * tm * d * 4
    row = pl.BlockSpec((tm, d), lambda i: (i, 0))
    vec = pl.BlockSpec((1, d), lambda i: (0, 0))
    out_specs = [row]
    out_shape = [jax.ShapeDtypeStruct((nrows, d), F32)]
    if emit_bf16:
        out_specs.append(row)
        out_shape.append(jax.ShapeDtypeStruct((nrows, d), BF16))
    return pl.pallas_call(
        _ln_body,
        grid=(nrows // tm,),
        in_specs=[pl.BlockSpec((tm, d), lambda i: (rb0 + i, 0)), vec, vec],
        out_specs=out_specs,
        out_shape=out_shape,
        compiler_params=pltpu.CompilerParams(
            dimension_semantics=("arbitrary",), vmem_limit_bytes=_vmem_limit(est)),
        name="layer_norm",
    )(t, g.reshape(1, d), b.reshape(1, d))


def _ret_body(q_ref, k_ref, v_ref, g_ref, cos_ref, sin_ref, lg_ref, s0_ref, za_ref, sout_ref,
              s_ref, d_ref, qd_ref, kd_ref, *, c, hb):
    b = pl.program_id(1)
    ci = pl.program_id(2)

    @pl.when((b == 0) & (ci == 0))
    def _():
        ri = lax.broadcasted_iota(jnp.int32, (c, c), 0)
        cj = lax.broadcasted_iota(jnp.int32, (c, c), 1)
        diff = (ri - cj).astype(F32)
        r = lax.broadcasted_iota(jnp.int32, (c, RET_DK), 0).astype(F32)
        for hh in range(hb):
            lg = lg_ref[hh]
            d_ref[hh] = jnp.where(diff >= 0, jnp.exp(jnp.maximum(diff, 0.0) * lg[:, :c]), 0.0)
            qd_ref[hh] = jnp.exp((r + 1.0) * lg)
            kd_ref[hh] = jnp.exp((c - 1.0 - r) * lg)

    @pl.when(ci == 0)
    def _():
        s_ref[...] = s0_ref[...]

    cos = cos_ref[...]
    sin = sin_ref[...]
    half = RET_DK // 2

    def rope(t):
        t1, t2 = t[:, :half], t[:, half:]
        return jnp.concatenate([t1 * cos - t2 * sin, t1 * sin + t2 * cos], axis=-1)

    for hh in range(hb):
        qc = slice(hh * RET_DK, (hh + 1) * RET_DK)
        vc = slice(hh * RET_DV, (hh + 1) * RET_DV)
        q = rope(q_ref[:, qc])
        k = rope(k_ref[:, qc]) * (RET_DK ** -0.5)
        vb = v_ref[:, vc].astype(BF16)
        s = s_ref[hh]
        scores = lax.dot_general(q.astype(BF16), k.astype(BF16), (((1,), (1,)), ((), ())),
                                 preferred_element_type=F32) * d_ref[hh]
        o = (jnp.dot(scores.astype(BF16), vb, preferred_element_type=F32)
             + jnp.dot((q * qd_ref[hh]).astype(BF16), s.astype(BF16), preferred_element_type=F32))
        kv = lax.dot_general((k * kd_ref[hh]).astype(BF16), vb, (((0,), (0,)), ((), ())),
                             preferred_element_type=F32)
        s_ref[hh] = s * jnp.exp(float(c) * lg_ref[hh]) + kv

        mu = jnp.mean(o, axis=-1, keepdims=True)
        od = o - mu
        var = jnp.mean(od * od, axis=-1, keepdims=True)
        on = od * lax.rsqrt(var + GN_EPS)
        za_ref[:, vc] = (jax.nn.silu(g_ref[:, vc]) * on).astype(BF16)

    @pl.when(ci == pl.num_programs(2) - 1)
    def _():
        sout_ref[...] = s_ref[...]


def _drop_leading(n, body):
    def wrapped(*refs):
        return body(*refs[n:])
    return wrapped


def _retention(h, s0, s0_layer, *, row0, nseq, seqlen, pos0, hb, za_into, st_into, st_layer,
               depth):
    m = h.shape[0]
    c = min(RET_BLOCK, seqlen)
    nc = seqlen // c
    rb0 = row0 // c
    hq = RET_HEADS // hb
    half = RET_DK // 2
    pos = pos0 + jnp.arange(seqlen, dtype=F32)
    freqs = ROPE_BASE ** (-jnp.arange(half, dtype=F32) / half)
    ang = pos[:, None] * freqs[None, :]
    cos, sin = jnp.cos(ang), jnp.sin(ang)
    log_gamma = jnp.log(1.0 - 2.0 ** (-5.0 - jnp.arange(RET_HEADS, dtype=F32)))
    lg = jnp.broadcast_to(log_gamma[:, None, None], (RET_HEADS, 1, RET_DK))

    def hcol(off):
        return pl.BlockSpec((c, hb * RET_DK), lambda hh, b, ci: (rb0 + b * nc + ci, off + hh))

    tab = pl.BlockSpec((c, half), lambda hh, b, ci: (ci, 0))

    def state(layer):
        return pl.BlockSpec((None, None, hb, RET_DK, RET_DV),
                            lambda hh, b, ci: (layer, b, hh, 0, 0))

    donated = [a for a in (za_into, st_into) if a is not None]
    aliases = {}
    if za_into is not None:
        aliases[0] = 0
    if st_into is not None:
        aliases[len(donated) - 1] = 1
    est = (2 * 4 * c * hb * RET_DK * 4 + 2 * 2 * hb * RET_DK * RET_DV * 4 + hb * RET_DK * RET_DV * 4
           + hb * (c * c + 2 * c * RET_DK) * 4 + 2 * c * hb * RET_DV * 2 + 16 * c * RET_DK * 4)
    return pl.pallas_call(
        _drop_leading(len(donated), functools.partial(_ret_body, c=c, hb=hb)),
        grid=(hq, nseq, nc),
        in_specs=[pl.BlockSpec(memory_space=pl.ANY)] * len(donated)
        + [hcol(0), hcol(hq), hcol(2 * hq), hcol(3 * hq), tab, tab,
           pl.BlockSpec((hb, 1, RET_DK), lambda hh, b, ci: (hh, 0, 0)), state(s0_layer)],
        out_specs=[pl.BlockSpec((c, hb * RET_DV), lambda hh, b, ci: (rb0 + b * nc + ci, hh)),
                   state(st_layer)],
        out_shape=[jax.ShapeDtypeStruct((m, RET_V), BF16),
                   jax.ShapeDtypeStruct((depth, nseq, RET_HEADS, RET_DK, RET_DV), F32)],
        input_output_aliases=aliases,
        scratch_shapes=[pltpu.VMEM((hb, RET_DK, RET_DV), F32), pltpu.VMEM((hb, c, c), F32),
                        pltpu.VMEM((hb, c, RET_DK), F32), pltpu.VMEM((hb, c, RET_DK), F32)],
        compiler_params=pltpu.CompilerParams(
            dimension_semantics=("arbitrary", "arbitrary", "arbitrary"),
            vmem_limit_bytes=_vmem_limit(est)),
        name="retention",
    )(*donated, h, h, h, h, cos, sin, lg, s0)


def _sgu_body(su_ref, sv_ref, lng_ref, lnb_ref, w_ref, bias_ref, zb_ref, *rest, c, tm, gdim, emit_v):
    sv = jax.nn.gelu(sv_ref[...])
    mu = jnp.mean(sv, axis=-1, keepdims=True)
    d = sv - mu
    var = jnp.mean(d * d, axis=-1, keepdims=True)
    svn = d * lax.rsqrt(var + LN_EPS) * lng_ref[...] + lnb_ref[...]
    if emit_v:
        rest[0][...] = svn
    svb = svn.astype(BF16)
    ri = lax.broadcasted_iota(jnp.int32, (SGU_CHUNK, SGU_CHUNK), 0)
    cj = lax.broadcasted_iota(jnp.int32, (SGU_CHUNK, SGU_CHUNK), 1)
    mask = (ri >= cj) & ((ri // c) == (cj // c))
    for g in range(SGU_GROUPS):
        wm = jnp.where(mask, w_ref[g], 0.0).astype(BF16)
        cols = slice(g * gdim, (g + 1) * gdim)
        for r in range(tm // SGU_CHUNK):
            rows = slice(r * SGU_CHUNK, (r + 1) * SGU_CHUNK)
            mixed = jnp.dot(wm, svb[rows, cols], preferred_element_type=F32) + bias_ref[:, cols]
            zb_ref[rows, cols] = (jax.nn.gelu(su_ref[rows, cols]) * mixed).astype(BF16)


def _spatial_gating(h, ln_g, ln_b, w_s, b_s, *, row0, nrows, seqlen, su_off, width, emit_v,
                    zb_into):
    m = h.shape[0]
    c = min(SGU_CHUNK, seqlen)
    rep = SGU_CHUNK // c
    gdim = width // SGU_GROUPS
    w_blk = jnp.tile(w_s[:, :c, :c], (1, rep, rep))
    bias = jnp.repeat(jnp.tile(b_s[:, :c], (1, rep)).T, gdim, axis=1)
    tm = _largest_divisor(nrows, 256, SGU_CHUNK)
    rb0 = row0 // tm
    su_blk = su_off // width
    assert row0 % tm == 0
    vec = pl.BlockSpec((1, width), lambda i: (0, 0))
    out_specs = [pl.BlockSpec((tm, width), lambda i: (rb0 + i, 0))]
    out_shape = [jax.ShapeDtypeStruct((m, width), BF16)]
    if emit_v:
        out_specs.append(pl.BlockSpec((tm, width), lambda i: (i, 0)))
        out_shape.append(jax.ShapeDtypeStruct((nrows, width), F32))
    donated = [] if zb_into is None else [zb_into]
    est = 2 * 2 * tm * width * 4 + 2 * tm * width * (2 + 4) + 8 * tm * width * 4
    res = pl.pallas_call(
        _drop_leading(len(donated),
                      functools.partial(_sgu_body, c=c, tm=tm, gdim=gdim, emit_v=emit_v)),
        grid=(nrows // tm,),
        in_specs=[pl.BlockSpec(memory_space=pl.ANY)] * len(donated)
        + [pl.BlockSpec((tm, width), lambda i: (rb0 + i, su_blk)),
           pl.BlockSpec((tm, width), lambda i: (rb0 + i, su_blk + 1)),
           vec, vec,
           pl.BlockSpec((SGU_GROUPS, SGU_CHUNK, SGU_CHUNK), lambda i: (0, 0, 0)),
           pl.BlockSpec((SGU_CHUNK, width), lambda i: (0, 0))],
        out_specs=out_specs,
        out_shape=out_shape,
        input_output_aliases={0: 0} if donated else {},
        compiler_params=pltpu.CompilerParams(
            dimension_semantics=("arbitrary",), vmem_limit_bytes=_vmem_limit(est)),
        name="spatial_gating",
    )(*donated, h, h, ln_g.reshape(1, width), ln_b.reshape(1, width), w_blk, bias)
    return res if emit_v else (res[0], None)


def _rows(ref, slab, start, n, stride):
    if n == 1:
        return ref[slab, start:start + 1, :]
    return ref[slab, pl.ds(start, n, stride=stride), :]


def _set_rows(ref, slab, start, n, stride, value):
    if n == 1:
        ref[slab, start:start + 1, :] = value
    else:
        ref[slab, pl.ds(start, n, stride=stride), :] = value


def _upgate_body(x_ref, wg_ref, wv_ref, cw_ref, cb_ref, st_ref, zero_ref, z_ref, tail_ref,
                 wbf_ref, buf_ref, conv_ref, val_ref, *, tm, tn, ni, nch, starts, ends):
    i = pl.program_id(1)
    slabs = tn // LANES
    cur = i % 2
    prev = 1 - cur
    buf_cur, conv_cur = buf_ref.at[cur], conv_ref.at[cur]

    @pl.when(i == 0)
    def _():
        wbf_ref[:, :tn] = wg_ref[...].astype(BF16)
        wbf_ref[:, tn:] = wv_ref[...].astype(BF16)
        buf_ref[0, :, 0:8, :] = jnp.zeros((slabs, 8, LANES), F32)

    @pl.when((i > 0) & (i < ni))
    def _():
        buf_cur[:, 0:8, :] = buf_ref[prev, :, tm:tm + 8, :]

    ch = tm // nch

    def project(k, after=()):
        rows = slice(k * ch, (k + 1) * ch)
        xk = x_ref[rows, :]
        if after:
            bits = pltpu.bitcast(after[0].astype(F32), jnp.int32)
            for a in after[1:]:
                bits = bits | pltpu.bitcast(a.astype(F32), jnp.int32)
            zero = pltpu.bitcast(bits & zero_ref[0:1, :], F32).astype(BF16)
            xk = jnp.concatenate([xk[:, :LANES] + zero, xk[:, LANES:]], axis=1)
        gv = jnp.dot(xk, wbf_ref[...], preferred_element_type=F32)
        val_ref[cur, rows, :] = gv[:, tn:]
        lo = 8 + k * ch
        for s in range(slabs):
            lanes = slice(s * LANES, (s + 1) * LANES)
            buf_cur[s, lo:lo + ch, :] = gv[:, lanes]
            conv_cur[s, rows, :] = (cb_ref[:, lanes]
                                    + buf_cur[s, lo - 2:lo - 2 + ch, :] * cw_ref[0:1, lanes]
                                    + buf_cur[s, lo - 1:lo - 1 + ch, :] * cw_ref[1:2, lanes]
                                    + gv[:, lanes] * cw_ref[2:3, lanes])

    def finish_previous(k):
        rows = slice(k * ch, (k + 1) * ch)
        out = []
        for s in range(slabs):
            lanes = slice(s * LANES, (s + 1) * LANES)
            zk = (jax.nn.gelu(conv_ref[prev, s, rows, :]) * val_ref[prev, rows, lanes]).astype(BF16)
            z_ref[rows, lanes] = zk
            out.append(zk)
        return out

    @pl.when(i == 0)
    def _():
        for k in range(nch):
            project(k)

    @pl.when((i > 0) & (i < ni))
    def _():
        done = [finish_previous(k) for k in range(nch)]
        project(0)
        for k in range(1, nch):
            deps = done[k - 1] + (done[k] if k == nch - 1 else [])
            project(k, after=tuple(deps))

    @pl.when(i == ni)
    def _():
        for k in range(nch):
            finish_previous(k)

    for tile, off, n, stride, slot in starts:
        @pl.when(i == tile)
        def _(off=off, n=n, stride=stride, slot=slot):
            for s in range(slabs):
                lanes = slice(s * LANES, (s + 1) * LANES)
                w0, w1, w2 = cw_ref[0:1, lanes], cw_ref[1:2, lanes], cw_ref[2:3, lanes]
                cb = cb_ref[:, lanes]
                p0 = _rows(buf_cur, s, 8 + off, n, stride)
                p1 = _rows(buf_cur, s, 8 + off + 1, n, stride)
                if slot is None:
                    c0 = cb + p0 * w2
                    c1 = cb + p0 * w1 + p1 * w2
                else:
                    s0 = st_ref[0, slot:slot + n, lanes]
                    s1 = st_ref[1, slot:slot + n, lanes]
                    c0 = cb + s0 * w0 + s1 * w1 + p0 * w2
                    c1 = cb + s1 * w0 + p0 * w1 + p1 * w2
                _set_rows(conv_cur, s, off, n, stride, c0)
                _set_rows(conv_cur, s, off + 1, n, stride, c1)

    for tile, off, n, stride, slot in ends:
        @pl.when(i == tile)
        def _(off=off, n=n, stride=stride, slot=slot):
            for s in range(slabs):
                lanes = slice(s * LANES, (s + 1) * LANES)
                tail_ref[0, slot:slot + n, lanes] = _rows(buf_cur, s, 8 + off - 1, n, stride)
                tail_ref[1, slot:slot + n, lanes] = _rows(buf_cur, s, 8 + off, n, stride)


def _seq_groups(seq_rows, seqlen, tm, within):
    groups = []
    for idx, r0 in enumerate(seq_rows):
        row = r0 + within
        tile, off = divmod(row, tm)
        if groups and groups[-1][0] == tile and seq_rows[idx - 1] + seqlen == r0:
            t, o, n, st, first = groups[-1]
            groups[-1] = (t, o, n + 1, st, first)
        else:
            groups.append((tile, off, 1, seqlen, idx))
    return groups


def _up_gate(xb, w_up, layer, conv_w, conv_b, conv_s0_s, *, bp, lp, bs, ls, tm, tn):
    m, k = xb.shape
    dff = w_up.shape[2] // 2
    vblk = dff // tn
    nseq = bp + bs
    prompt_rows = [b * lp for b in range(bp)]
    sample_rows = [bp * lp + b * ls for b in range(bs)]
    starts, ends = [], []
    for rows, sl, has_state, slot0 in ((sample_rows, ls, True, 0), (prompt_rows, lp, False, bs)):
        for t, o, n, st, first in _seq_groups(rows, sl, tm, 0):
            assert o + (n - 1) * st + 1 < tm, "a sequence's first two rows must share a row tile"
            starts.append((t, o, n, st, first if has_state else None))
        for t, o, n, st, first in _seq_groups(rows, sl, tm, sl - 1):
            assert o >= 1, "a sequence's last two rows must share a row tile"
            ends.append((t, o, n, st, slot0 + first))
    st_in = jnp.transpose(conv_s0_s, (1, 0, 2))
    ni = m // tm
    nch = max(n for n in (1, 2, 4) if tm % (16 * n) == 0)
    est = (2 * tm * k * 2 + 2 * 2 * k * tn * 4 + 2 * k * tn * 2 + 3 * 2 * (tm + 8) * tn * 4
           + 2 * tm * tn * 2 + 2 * 2 * (bs + nseq) * tn * 4 + 6 * tm * tn * 4)
    return pl.pallas_call(
        functools.partial(_upgate_body, tm=tm, tn=tn, ni=ni, nch=nch, starts=tuple(starts),
                          ends=tuple(ends)),
        grid=(dff // tn, ni + 1),
        in_specs=[pl.BlockSpec((tm, k), lambda j, i: (jnp.minimum(i, ni - 1), 0)),
                  pl.BlockSpec((None, k, tn), lambda j, i: (layer, 0, j)),
                  pl.BlockSpec((None, k, tn), lambda j, i: (layer, 0, vblk + j)),
                  pl.BlockSpec((CONV_W, tn), lambda j, i: (0, j)),
                  pl.BlockSpec((1, tn), lambda j, i: (0, j)),
                  pl.BlockSpec((CONV_W - 1, bs, tn), lambda j, i: (0, 0, j)),
                  pl.BlockSpec((8, LANES), lambda j, i: (0, 0))],
        out_specs=[pl.BlockSpec((tm, tn), lambda j, i: (jnp.maximum(i - 1, 0), j)),
                   pl.BlockSpec((CONV_W - 1, nseq, tn), lambda j, i: (0, 0, j))],
        out_shape=[jax.ShapeDtypeStruct((m, dff), BF16),
                   jax.ShapeDtypeStruct((CONV_W - 1, nseq, dff), F32)],
        scratch_shapes=[pltpu.VMEM((k, 2 * tn), BF16),
                        pltpu.VMEM((2, tn // LANES, tm + 8, LANES), F32),
                        pltpu.VMEM((2, tn // LANES, tm, LANES), F32),
                        pltpu.VMEM((2, tm, tn), F32)],
        compiler_params=pltpu.CompilerParams(
            dimension_semantics=("arbitrary", "arbitrary"), vmem_limit_bytes=_vmem_limit(est)),
        name="up_gate",
    )(xb, w_up, w_up, conv_w, conv_b.reshape(1, dff), st_in, jnp.zeros((8, LANES), jnp.int32))


def _layer(layer, depth, x, xb, state_ret, conv_s0_s, ret_bufs, big, small, *, bp, lp, bs, ls,
           alpha, last):
    w_in, w_a, w_b, w_o, w_up, w_down = big
    ln1_g, ln1_b, sgu_ln_g, sgu_ln_b, sgu_w, sgu_b, conv_w, conv_b, ln2_g, ln2_b = small
    m, d = x.shape
    mp = bp * lp
    ms = bs * ls
    width = d // 2
    dff = w_down.shape[1]
    su_off = 2 * RET_QK + 2 * RET_V
    ga_off = su_off + 2 * width
    gb_off = ga_off + d
    tm_big = _largest_divisor(m, 1088, 16)

    h = _matmul(xb, w_in, layer, tm=tm_big, tn=512, out_dtype=F32, name="mm_in")

    zero_ret = jnp.zeros((1, bp, RET_HEADS, RET_DK, RET_DV), F32)
    za, ret_p = _retention(h, zero_ret, 0, row0=0, nseq=bp, seqlen=lp, pos0=0.0, hb=4,
                           za_into=None, st_into=ret_bufs[0], st_layer=layer, depth=depth)
    za, ret_s = _retention(h, state_ret, layer, row0=mp, nseq=bs, seqlen=ls,
                           pos0=float(PAST_LEN), hb=RET_HEADS,
                           za_into=za, st_into=ret_bufs[1], st_layer=layer, depth=depth)
    zb, _ = _spatial_gating(h, sgu_ln_g, sgu_ln_b, sgu_w, sgu_b, row0=0, nrows=mp, seqlen=lp,
                            su_off=su_off, width=width, emit_v=False, zb_into=None)
    zb, sv_s = _spatial_gating(h, sgu_ln_g, sgu_ln_b, sgu_w, sgu_b, row0=mp, nrows=ms, seqlen=ls,
                               su_off=su_off, width=width, emit_v=True, zb_into=zb)

    merged = _merge(za, zb, h, w_a, w_b, layer, ga_off=ga_off, gb_off=gb_off, tm=tm_big, tn=512)
    t1 = _matmul(merged, w_o, layer, tm=tm_big, tn=512, out_dtype=F32, name="mm_o",
                 residual=x, res_scale=alpha)
    tm_ln = _largest_divisor(math.gcd(mp, ms), 256, 8)
    x1, x1b = _layer_norm(t1, ln1_g, ln1_b, row0=0, nrows=m, tm=tm_ln, emit_bf16=True)

    z, tails = _up_gate(x1b, w_up, layer, conv_w, conv_b, conv_s0_s, bp=bp, lp=lp, bs=bs, ls=ls,
                        tm=tm_big, tn=_largest_divisor(dff, 256, LANES))
    t2 = _matmul(z, w_down, layer, tm=_largest_divisor(m, 272, 16), tn=512, out_dtype=F32,
                 name="mm_down", w_buffers=1, residual=x1, res_scale=alpha)
    if last:
        (y_p,) = _layer_norm(t2, ln2_g, ln2_b, row0=0, nrows=mp, tm=tm_ln, emit_bf16=False)
        (y_s,) = _layer_norm(t2, ln2_g, ln2_b, row0=mp, nrows=ms, tm=tm_ln, emit_bf16=False)
        x2, x2b = (y_p, y_s), None
    else:
        x2, x2b = _layer_norm(t2, ln2_g, ln2_b, row0=0, nrows=m, tm=tm_ln, emit_bf16=True)

    conv_s = jnp.transpose(tails[:, :bs], (1, 0, 2))
    conv_p = jnp.transpose(tails[:, bs:], (1, 0, 2))
    return x2, x2b, (ret_p, ret_s), conv_p, conv_s, sv_s.reshape(bs, ls, width)


def kernel(x_prompt, x_sample, state_ret, state_conv, w_in, w_a, w_b, w_o, ln1_g, ln1_b,
           sgu_ln_g, sgu_ln_b, sgu_w, sgu_b, w_up, conv_w, conv_b, w_down, ln2_g, ln2_b):
    bp, lp, d = x_prompt.shape
    bs, ls, _ = x_sample.shape
    depth = w_in.shape[0]
    alpha = float((2 * depth) ** 0.25)
    x = jnp.concatenate([x_prompt.reshape(bp * lp, d), x_sample.reshape(bs * ls, d)], axis=0)
    xb = x.astype(BF16)
    big = (w_in, w_a, w_b, w_o, w_up, w_down)
    ret_bufs = (None, None)
    conv_p, conv_s, sgu_v = [], [], []
    for l in range(depth):
        small = (ln1_g[l], ln1_b[l], sgu_ln_g[l], sgu_ln_b[l], sgu_w[l], sgu_b[l],
                 conv_w[l], conv_b[l], ln2_g[l], ln2_b[l])
        x, xb, ret_bufs, cp, cs, sv = _layer(
            l, depth, x, xb, state_ret, state_conv[l], ret_bufs, big, small,
            bp=bp, lp=lp, bs=bs, ls=ls, alpha=alpha, last=l == depth - 1)
        conv_p.append(cp)
        conv_s.append(cs)
        sgu_v.append(sv)
    y_p, y_s = x
    return (y_p.reshape(bp, lp, d), y_s.reshape(bs, ls, d), ret_bufs[0], jnp.stack(conv_p),
            ret_bufs[1], jnp.stack(conv_s), jnp.stack(sgu_v))
```

```python
import functools
import math

import jax
import jax.numpy as jnp
from jax import lax
from jax.experimental import pallas as pl
from jax.experimental.pallas import tpu as pltpu

F32 = jnp.float32
BF16 = jnp.bfloat16

RET_HEADS = 8
RET_DK = 256
RET_DV = 256
RET_QK = RET_HEADS * RET_DK
RET_V = RET_HEADS * RET_DV
SGU_GROUPS = 8
SGU_CHUNK = 128
CONV_W = 3
PAST_LEN = 1024
ROPE_BASE = 10000.0
LN_EPS = 1e-5
GN_EPS = 1e-6

V7X_VMEM_BYTES = 64 * 1024 * 1024
LANES = 128
RET_BLOCK = 256


def _vmem_limit(estimate_bytes):
    return int(min(V7X_VMEM_BYTES - (4 << 20), max(estimate_bytes * 5 // 4 + (2 << 20), 16 << 20)))


def _largest_divisor(n, limit, multiple):
    best = None
    d = multiple
    while d <= min(n, limit):
        if n % d == 0:
            best = d
        d += multiple
    assert best is not None, (n, limit, multiple)
    return best


def _mm_body(x_ref, w_ref, *rest, n_res, res_fn, res_scale):
    o_ref, wbf_ref = rest[-2], rest[-1]

    @pl.when(pl.program_id(1) == 0)
    def _():
        wbf_ref[...] = w_ref[...].astype(BF16)

    acc = jnp.dot(x_ref[...], wbf_ref[...], preferred_element_type=F32)
    if n_res:
        acc = res_scale * res_fn(*rest[:n_res]) + acc
    o_ref[...] = acc.astype(o_ref.dtype)


def _res_normed(t_ref, stats_ref, g_ref, b_ref):
    reps = t_ref.shape[1] // LANES
    mu = jnp.concatenate([stats_ref[:, :LANES]] * reps, axis=1)
    rstd = jnp.concatenate([stats_ref[:, LANES:]] * reps, axis=1)
    return (t_ref[...] - mu) * rstd * g_ref[...] + b_ref[...]


def _res_split(p_ref, s_ref, *, tile, off):
    p = p_ref[...]
    n = s_ref.shape[0]
    parts = [p[:off], s_ref[...], p[off + n:]]
    mixed = jnp.concatenate([q for q in parts if q.shape[0]], axis=0)
    return jnp.where(pl.program_id(1) == tile, mixed, p)


def _matmul(x, w, layer, *, tm, tn, out_dtype, name, w_buffers=2, residual=None, res_scale=None):
    m, k = x.shape
    n = w.shape[2]
    est = (2 * tm * k * 2 + w_buffers * k * tn * 4 + k * tn * 2
           + 2 * tm * tn * jnp.dtype(out_dtype).itemsize)
    in_specs = [pl.BlockSpec((tm, k), lambda j, i: (i, 0)),
                pl.BlockSpec((None, k, tn), lambda j, i: (layer, 0, j),
                             pipeline_mode=pl.Buffered(w_buffers))]
    args = [x, w]
    res_fn = None
    tile_mn = pl.BlockSpec((tm, tn), lambda j, i: (i, j))
    if residual is not None:
        kind = residual[0]
        est += 2 * tm * tn * 4
        if kind == "normed":
            _, t, stats, g, b = residual
            res_fn = _res_normed
            vec = pl.BlockSpec((1, tn), lambda j, i: (0, j))
            in_specs += [tile_mn, pl.BlockSpec((tm, 2 * LANES), lambda j, i: (i, 0)), vec, vec]
            args += [t, stats, g.reshape(1, n), b.reshape(1, n)]
            est += 2 * tm * 2 * LANES * 4
        else:
            _, p, sec = residual
            mp, ms = p.shape[0], sec.shape[0]
            tile, off = divmod(mp, tm)
            assert kind == "split" and off + ms <= tm and mp + ms == m
            res_fn = functools.partial(_res_split, tile=tile, off=off)
            in_specs += [pl.BlockSpec((tm, tn), lambda j, i: (jnp.minimum(i, tile), j)),
                         pl.BlockSpec((ms, tn), lambda j, i: (0, j))]
            args += [p, sec]
            est += 2 * ms * tn * 4
    return pl.pallas_call(
        functools.partial(_mm_body, n_res=len(args) - 2, res_fn=res_fn, res_scale=res_scale),
        grid=(n // tn, m // tm),
        in_specs=in_specs,
        out_specs=tile_mn,
        out_shape=jax.ShapeDtypeStruct((m, n), out_dtype),
        scratch_shapes=[pltpu.VMEM((k, tn), BF16)],
        compiler_params=pltpu.CompilerParams(
            dimension_semantics=("arbitrary", "arbitrary"), vmem_limit_bytes=_vmem_limit(est)),
        name=name,
    )(*args)


def _merge_body(za_ref, zb_ref, ga_ref, gb_ref, wa_ref, wb_ref, o_ref, wabf_ref, wbbf_ref):
    @pl.when(pl.program_id(1) == 0)
    def _():
        wabf_ref[...] = wa_ref[...].astype(BF16)
        wbbf_ref[...] = wb_ref[...].astype(BF16)

    a = jnp.dot(za_ref[...], wabf_ref[...], preferred_element_type=F32)
    b = jnp.dot(zb_ref[...], wbbf_ref[...], preferred_element_type=F32)
    ga = jax.nn.sigmoid(ga_ref[...].astype(F32))
    gb = jax.nn.sigmoid(gb_ref[...].astype(F32))
    o_ref[...] = (ga * a + gb * b).astype(o_ref.dtype)


def _merge(za, zb, h, w_a, w_b, layer, *, ga_off, gb_off, tm, tn):
    m, ka = za.shape
    kb = zb.shape[1]
    n = w_a.shape[2]
    ga_blk, gb_blk = ga_off // tn, gb_off // tn
    est = (2 * tm * (ka + kb) * 2 + 2 * (ka + kb) * tn * 4 + (ka + kb) * tn * 2
           + 2 * 2 * tm * tn * h.dtype.itemsize + 2 * tm * tn * 2 + 4 * tm * tn * 4)
    return pl.pallas_call(
        _merge_body,
        grid=(n // tn, m // tm),
        in_specs=[pl.BlockSpec((tm, ka), lambda j, i: (i, 0)),
                  pl.BlockSpec((tm, kb), lambda j, i: (i, 0)),
                  pl.BlockSpec((tm, tn), lambda j, i: (i, ga_blk + j)),
                  pl.BlockSpec((tm, tn), lambda j, i: (i, gb_blk + j)),
                  pl.BlockSpec((None, ka, tn), lambda j, i: (layer, 0, j)),
                  pl.BlockSpec((None, kb, tn), lambda j, i: (layer, 0, j))],
        out_specs=pl.BlockSpec((tm, tn), lambda j, i: (i, j)),
        out_shape=jax.ShapeDtypeStruct((m, n), BF16),
        scratch_shapes=[pltpu.VMEM((ka, tn), BF16), pltpu.VMEM((kb, tn), BF16)],
        compiler_params=pltpu.CompilerParams(
            dimension_semantics=("arbitrary", "arbitrary"), vmem_limit_bytes=_vmem_limit(est)),
        name="merge_mm",
    )(za, zb, h, h, w_a, w_b)


def _ln_body(t_ref, g_ref, b_ref, *out_refs, emit_f32):
    t = t_ref[...]
    mu = jnp.mean(t, axis=-1, keepdims=True)
    d = t - mu
    var = jnp.mean(d * d, axis=-1, keepdims=True)
    rstd = lax.rsqrt(var + LN_EPS)
    out = d * rstd * g_ref[...] + b_ref[...]
    if emit_f32:
        out_refs[0][...] = out
    else:
        obf_ref, stats_ref = out_refs
        obf_ref[...] = out.astype(BF16)
        rows = t.shape[0]
        stats_ref[:, :LANES] = jnp.broadcast_to(mu, (rows, LANES))
        stats_ref[:, LANES:] = jnp.broadcast_to(rstd, (rows, LANES))


def _layer_norm(t, g, b, *, row0, nrows, tm, emit_f32):
    d = t.shape[1]
    rb0 = row0 // tm
    assert row0 % tm == 0 and nrows % tm == 0
    est = 2 * tm * d * 4 + 2 * tm * d * 4 + 6 * tm * d * 4
    row = pl.BlockSpec((tm, d), lambda i: (i, 0))
    vec = pl.BlockSpec((1, d), lambda i: (0, 0))
    if emit_f32:
        out_specs = [row]
        out_shape = [jax.ShapeDtypeStruct((nrows, d), F32)]
    else:
        out_specs = [row, pl.BlockSpec((tm, 2 * LANES), lambda i: (i, 0))]
        out_shape = [jax.ShapeDtypeStruct((nrows, d), BF16),
                     jax.ShapeDtypeStruct((nrows, 2 * LANES), F32)]
    return pl.pallas_call(
        functools.partial(_ln_body, emit_f32=emit_f32),
        grid=(nrows // tm,),
        in_specs=[pl.BlockSpec((tm, d), lambda i: (rb0 + i, 0)), vec, vec],
        out_specs=out_specs,
        out_shape=out_shape,
        compiler_params=pltpu.CompilerParams(
            dimension_semantics=("arbitrary",), vmem_limit_bytes=_vmem_limit(est)),
        name="layer_norm",
    )(t, g.reshape(1, d), b.reshape(1, d))


def _ret_body(q_ref, k_ref, v_ref, g_ref, cos_ref, sin_ref, lg_ref, s0_ref, za_ref, sout_ref,
              s_ref, d_ref, qd_ref, kd_ref, *, c, hb):
    b = pl.program_id(1)
    ci = pl.program_id(2)

    @pl.when((b == 0) & (ci == 0))
    def _():
        ri = lax.broadcasted_iota(jnp.int32, (c, c), 0)
        cj = lax.broadcasted_iota(jnp.int32, (c, c), 1)
        diff = (ri - cj).astype(F32)
        r = lax.broadcasted_iota(jnp.int32, (c, RET_DK), 0).astype(F32)
        for hh in range(hb):
            lg = lg_ref[hh]
            d_ref[hh] = jnp.where(diff >= 0, jnp.exp(jnp.maximum(diff, 0.0) * lg[:, :c]), 0.0)
            qd_ref[hh] = jnp.exp((r + 1.0) * lg)
            kd_ref[hh] = jnp.exp((c - 1.0 - r) * lg)

    @pl.when(ci == 0)
    def _():
        s_ref[...] = s0_ref[...]

    cos = cos_ref[...]
    sin = sin_ref[...]
    half = RET_DK // 2

    def rope(t):
        t1, t2 = t[:, :half], t[:, half:]
        return jnp.concatenate([t1 * cos - t2 * sin, t1 * sin + t2 * cos], axis=-1)

    for hh in range(hb):
        qc = slice(hh * RET_DK, (hh + 1) * RET_DK)
        vc = slice(hh * RET_DV, (hh + 1) * RET_DV)
        q = rope(q_ref[:, qc].astype(F32))
        k = rope(k_ref[:, qc].astype(F32)) * (RET_DK ** -0.5)
        vb = v_ref[:, vc].astype(BF16)
        s = s_ref[hh]
        scores = lax.dot_general(q.astype(BF16), k.astype(BF16), (((1,), (1,)), ((), ())),
                                 preferred_element_type=F32) * d_ref[hh]
        o = (jnp.dot(scores.astype(BF16), vb, preferred_element_type=F32)
             + jnp.dot((q * qd_ref[hh]).astype(BF16), s.astype(BF16), preferred_element_type=F32))
        kv = lax.dot_general((k * kd_ref[hh]).astype(BF16), vb, (((0,), (0,)), ((), ())),
                             preferred_element_type=F32)
        s_ref[hh] = s * jnp.exp(float(c) * lg_ref[hh]) + kv

        mu = jnp.mean(o, axis=-1, keepdims=True)
        od = o - mu
        var = jnp.mean(od * od, axis=-1, keepdims=True)
        on = od * lax.rsqrt(var + GN_EPS)
        za_ref[:, vc] = (jax.nn.silu(g_ref[:, vc].astype(F32)) * on).astype(BF16)

    @pl.when(ci == pl.num_programs(2) - 1)
    def _():
        sout_ref[...] = s_ref[...]


def _drop_leading(n, body):
    def wrapped(*refs):
        return body(*refs[n:])
    return wrapped


def _retention(h, s0, s0_layer, *, row0, nseq, seqlen, pos0, hb, za_into, st_into, st_layer,
               depth):
    m = h.shape[0]
    c = min(RET_BLOCK, seqlen)
    nc = seqlen // c
    rb0 = row0 // c
    hq = RET_HEADS // hb
    half = RET_DK // 2
    pos = pos0 + jnp.arange(seqlen, dtype=F32)
    freqs = ROPE_BASE ** (-jnp.arange(half, dtype=F32) / half)
    ang = pos[:, None] * freqs[None, :]
    cos, sin = jnp.cos(ang), jnp.sin(ang)
    log_gamma = jnp.log(1.0 - 2.0 ** (-5.0 - jnp.arange(RET_HEADS, dtype=F32)))
    lg = jnp.broadcast_to(log_gamma[:, None, None], (RET_HEADS, 1, RET_DK))

    def hcol(off):
        return pl.BlockSpec((c, hb * RET_DK), lambda hh, b, ci: (rb0 + b * nc + ci, off + hh))

    tab = pl.BlockSpec((c, half), lambda hh, b, ci: (ci, 0))

    def state(layer):
        return pl.BlockSpec((None, None, hb, RET_DK, RET_DV),
                            lambda hh, b, ci: (layer, b, hh, 0, 0))

    donated = [a for a in (za_into, st_into) if a is not None]
    aliases = {}
    if za_into is not None:
        aliases[0] = 0
    if st_into is not None:
        aliases[len(donated) - 1] = 1
    est = (2 * 4 * c * hb * RET_DK * 4 + 2 * 2 * hb * RET_DK * RET_DV * 4 + hb * RET_DK * RET_DV * 4
           + hb * (c * c + 2 * c * RET_DK) * 4 + 2 * c * hb * RET_DV * 2 + 16 * c * RET_DK * 4)
    return pl.pallas_call(
        _drop_leading(len(donated), functools.partial(_ret_body, c=c, hb=hb)),
        grid=(hq, nseq, nc),
        in_specs=[pl.BlockSpec(memory_space=pl.ANY)] * len(donated)
        + [hcol(0), hcol(hq), hcol(2 * hq), hcol(3 * hq), tab, tab,
           pl.BlockSpec((hb, 1, RET_DK), lambda hh, b, ci: (hh, 0, 0)), state(s0_layer)],
        out_specs=[pl.BlockSpec((c, hb * RET_DV), lambda hh, b, ci: (rb0 + b * nc + ci, hh)),
                   state(st_layer)],
        out_shape=[jax.ShapeDtypeStruct((m, RET_V), BF16),
                   jax.ShapeDtypeStruct((depth, nseq, RET_HEADS, RET_DK, RET_DV), F32)],
        input_output_aliases=aliases,
        scratch_shapes=[pltpu.VMEM((hb, RET_DK, RET_DV), F32), pltpu.VMEM((hb, c, c), F32),
                        pltpu.VMEM((hb, c, RET_DK), F32), pltpu.VMEM((hb, c, RET_DK), F32)],
        compiler_params=pltpu.CompilerParams(
            dimension_semantics=("arbitrary", "arbitrary", "arbitrary"),
            vmem_limit_bytes=_vmem_limit(est)),
        name="retention",
    )(*donated, h, h, h, h, cos, sin, lg, s0)


def _sgu_body(su_ref, sv_ref, lng_ref, lnb_ref, w_ref, bias_ref, zb_ref, *rest, c, tm, gdim, emit_v):
    sv = jax.nn.gelu(sv_ref[...].astype(F32))
    mu = jnp.mean(sv, axis=-1, keepdims=True)
    d = sv - mu
    var = jnp.mean(d * d, axis=-1, keepdims=True)
    svn = d * lax.rsqrt(var + LN_EPS) * lng_ref[...] + lnb_ref[...]
    if emit_v:
        rest[0][...] = svn
    svb = svn.astype(BF16)
    ri = lax.broadcasted_iota(jnp.int32, (SGU_CHUNK, SGU_CHUNK), 0)
    cj = lax.broadcasted_iota(jnp.int32, (SGU_CHUNK, SGU_CHUNK), 1)
    mask = (ri >= cj) & ((ri // c) == (cj // c))
    for g in range(SGU_GROUPS):
        wm = jnp.where(mask, w_ref[g], 0.0).astype(BF16)
        cols = slice(g * gdim, (g + 1) * gdim)
        for r in range(tm // SGU_CHUNK):
            rows = slice(r * SGU_CHUNK, (r + 1) * SGU_CHUNK)
            mixed = jnp.dot(wm, svb[rows, cols], preferred_element_type=F32) + bias_ref[:, cols]
            zb_ref[rows, cols] = (jax.nn.gelu(su_ref[rows, cols].astype(F32)) * mixed).astype(BF16)


def _spatial_gating(h, ln_g, ln_b, w_s, b_s, *, row0, nrows, seqlen, su_off, width, emit_v,
                    zb_into):
    m = h.shape[0]
    c = min(SGU_CHUNK, seqlen)
    rep = SGU_CHUNK // c
    gdim = width // SGU_GROUPS
    w_blk = jnp.tile(w_s[:, :c, :c], (1, rep, rep))
    bias = jnp.repeat(jnp.tile(b_s[:, :c], (1, rep)).T, gdim, axis=1)
    tm = _largest_divisor(nrows, 256, SGU_CHUNK)
    rb0 = row0 // tm
    su_blk = su_off // width
    assert row0 % tm == 0
    vec = pl.BlockSpec((1, width), lambda i: (0, 0))
    out_specs = [pl.BlockSpec((tm, width), lambda i: (rb0 + i, 0))]
    out_shape = [jax.ShapeDtypeStruct((m, width), BF16)]
    if emit_v:
        out_specs.append(pl.BlockSpec((tm, width), lambda i: (i, 0)))
        out_shape.append(jax.ShapeDtypeStruct((nrows, width), F32))
    donated = [] if zb_into is None else [zb_into]
    est = 2 * 2 * tm * width * 4 + 2 * tm * width * (2 + 4) + 8 * tm * width * 4
    res = pl.pallas_call(
        _drop_leading(len(donated),
                      functools.partial(_sgu_body, c=c, tm=tm, gdim=gdim, emit_v=emit_v)),
        grid=(nrows // tm,),
        in_specs=[pl.BlockSpec(memory_space=pl.ANY)] * len(donated)
        + [pl.BlockSpec((tm, width), lambda i: (rb0 + i, su_blk)),
           pl.BlockSpec((tm, width), lambda i: (rb0 + i, su_blk + 1)),
           vec, vec,
           pl.BlockSpec((SGU_GROUPS, SGU_CHUNK, SGU_CHUNK), lambda i: (0, 0, 0)),
           pl.BlockSpec((SGU_CHUNK, width), lambda i: (0, 0))],
        out_specs=out_specs,
        out_shape=out_shape,
        input_output_aliases={0: 0} if donated else {},
        compiler_params=pltpu.CompilerParams(
            dimension_semantics=("arbitrary",), vmem_limit_bytes=_vmem_limit(est)),
        name="spatial_gating",
    )(*donated, h, h, ln_g.reshape(1, width), ln_b.reshape(1, width), w_blk, bias)
    return res if emit_v else (res[0], None)


def _rows(ref, slab, start, n, stride):
    if n == 1:
        return ref[slab, start:start + 1, :]
    return ref[slab, pl.ds(start, n, stride=stride), :]


def _set_rows(ref, slab, start, n, stride, value):
    if n == 1:
        ref[slab, start:start + 1, :] = value
    else:
        ref[slab, pl.ds(start, n, stride=stride), :] = value


def _upgate_body(x_ref, wg_ref, wv_ref, cw_ref, cb_ref, st_ref, z_ref, tail_ref,
                 wbf_ref, buf_ref, conv_ref, val_ref, *, tm, tn, starts, ends):
    i = pl.program_id(1)
    slabs = tn // LANES

    @pl.when(i == 0)
    def _():
        wbf_ref[:, :tn] = wg_ref[...].astype(BF16)
        wbf_ref[:, tn:] = wv_ref[...].astype(BF16)
        buf_ref[:, 0:8, :] = jnp.zeros((slabs, 8, LANES), F32)

    @pl.when(i > 0)
    def _():
        buf_ref[:, 0:8, :] = buf_ref[:, tm:tm + 8, :]

    gv = jnp.dot(x_ref[...], wbf_ref[...], preferred_element_type=F32)
    val_ref[...] = gv[:, tn:]
    for s in range(slabs):
        lanes = slice(s * LANES, (s + 1) * LANES)
        buf_ref[s, 8:tm + 8, :] = gv[:, lanes]
        conv_ref[s] = (cb_ref[:, lanes] + buf_ref[s, 6:tm + 6, :] * cw_ref[0:1, lanes]
                       + buf_ref[s, 7:tm + 7, :] * cw_ref[1:2, lanes]
                       + gv[:, lanes] * cw_ref[2:3, lanes])

    for tile, off, n, stride, slot in starts:
        @pl.when(i == tile)
        def _(off=off, n=n, stride=stride, slot=slot):
            for s in range(slabs):
                lanes = slice(s * LANES, (s + 1) * LANES)
                w0, w1, w2 = cw_ref[0:1, lanes], cw_ref[1:2, lanes], cw_ref[2:3, lanes]
                cb = cb_ref[:, lanes]
                p0 = _rows(buf_ref, s, 8 + off, n, stride)
                p1 = _rows(buf_ref, s, 8 + off + 1, n, stride)
                if slot is None:
                    c0 = cb + p0 * w2
                    c1 = cb + p0 * w1 + p1 * w2
                else:
                    s0 = st_ref[0, slot:slot + n, lanes]
                    s1 = st_ref[1, slot:slot + n, lanes]
                    c0 = cb + s0 * w0 + s1 * w1 + p0 * w2
                    c1 = cb + s1 * w0 + p0 * w1 + p1 * w2
                _set_rows(conv_ref, s, off, n, stride, c0)
                _set_rows(conv_ref, s, off + 1, n, stride, c1)

    for tile, off, n, stride, slot in ends:
        @pl.when(i == tile)
        def _(off=off, n=n, stride=stride, slot=slot):
            for s in range(slabs):
                lanes = slice(s * LANES, (s + 1) * LANES)
                tail_ref[0, slot:slot + n, lanes] = _rows(buf_ref, s, 8 + off - 1, n, stride)
                tail_ref[1, slot:slot + n, lanes] = _rows(buf_ref, s, 8 + off, n, stride)

    for s in range(slabs):
        lanes = slice(s * LANES, (s + 1) * LANES)
        z_ref[:, lanes] = (jax.nn.gelu(conv_ref[s]) * val_ref[:, lanes]).astype(BF16)


def _seq_groups(seq_rows, seqlen, tm, within):
    groups = []
    for idx, r0 in enumerate(seq_rows):
        row = r0 + within
        tile, off = divmod(row, tm)
        if groups and groups[-1][0] == tile and seq_rows[idx - 1] + seqlen == r0:
            t, o, n, st, first = groups[-1]
            groups[-1] = (t, o, n + 1, st, first)
        else:
            groups.append((tile, off, 1, seqlen, idx))
    return groups


def _up_gate(xb, w_up, layer, conv_w, conv_b, conv_s0_s, *, bp, lp, bs, ls, tm, tn):
    m, k = xb.shape
    dff = w_up.shape[2] // 2
    vblk = dff // tn
    nseq = bp + bs
    prompt_rows = [b * lp for b in range(bp)]
    sample_rows = [bp * lp + b * ls for b in range(bs)]
    starts, ends = [], []
    for rows, sl, has_state, slot0 in ((sample_rows, ls, True, 0), (prompt_rows, lp, False, bs)):
        for t, o, n, st, first in _seq_groups(rows, sl, tm, 0):
            assert o + (n - 1) * st + 1 < tm, "a sequence's first two rows must share a row tile"
            starts.append((t, o, n, st, first if has_state else None))
        for t, o, n, st, first in _seq_groups(rows, sl, tm, sl - 1):
            assert o >= 1, "a sequence's last two rows must share a row tile"
            ends.append((t, o, n, st, slot0 + first))
    st_in = jnp.transpose(conv_s0_s, (1, 0, 2))
    est = (2 * tm * k * 2 + 2 * 2 * k * tn * 4 + 2 * k * tn * 2 + 2 * (tm + 8) * tn * 4
           + 2 * tm * tn * 2 + 2 * 2 * (bs + nseq) * tn * 4 + 6 * tm * tn * 4)
    return pl.pallas_call(
        functools.partial(_upgate_body, tm=tm, tn=tn, starts=tuple(starts), ends=tuple(ends)),
        grid=(dff // tn, m // tm),
        in_specs=[pl.BlockSpec((tm, k), lambda j, i: (i, 0)),
                  pl.BlockSpec((None, k, tn), lambda j, i: (layer, 0, j)),
                  pl.BlockSpec((None, k, tn), lambda j, i: (layer, 0, vblk + j)),
                  pl.BlockSpec((CONV_W, tn), lambda j, i: (0, j)),
                  pl.BlockSpec((1, tn), lambda j, i: (0, j)),
                  pl.BlockSpec((CONV_W - 1, bs, tn), lambda j, i: (0, 0, j))],
        out_specs=[pl.BlockSpec((tm, tn), lambda j, i: (i, j)),
                   pl.BlockSpec((CONV_W - 1, nseq, tn), lambda j, i: (0, 0, j))],
        out_shape=[jax.ShapeDtypeStruct((m, dff), BF16),
                   jax.ShapeDtypeStruct((CONV_W - 1, nseq, dff), F32)],
        scratch_shapes=[pltpu.VMEM((k, 2 * tn), BF16),
                        pltpu.VMEM((tn // LANES, tm + 8, LANES), F32),
                        pltpu.VMEM((tn // LANES, tm, LANES), F32), pltpu.VMEM((tm, tn), F32)],
        compiler_params=pltpu.CompilerParams(
            dimension_semantics=("arbitrary", "arbitrary"), vmem_limit_bytes=_vmem_limit(est)),
        name="up_gate",
    )(xb, w_up, w_up, conv_w, conv_b.reshape(1, dff), st_in)


def _layer(layer, depth, x_res, xb, state_ret, conv_s0_s, ret_bufs, big, small, *, bp, lp, bs,
           ls, alpha, last):
    w_in, w_a, w_b, w_o, w_up, w_down = big
    ln1_g, ln1_b, sgu_ln_g, sgu_ln_b, sgu_w, sgu_b, conv_w, conv_b, ln2_g, ln2_b = small
    m, d = xb.shape
    mp = bp * lp
    ms = bs * ls
    width = d // 2
    dff = w_down.shape[1]
    su_off = 2 * RET_QK + 2 * RET_V
    ga_off = su_off + 2 * width
    gb_off = ga_off + d
    tm_big = _largest_divisor(m, 1088, 16)

    h = _matmul(xb, w_in, layer, tm=tm_big, tn=512, out_dtype=BF16, name="mm_in")

    zero_ret = jnp.zeros((1, bp, RET_HEADS, RET_DK, RET_DV), F32)
    za, ret_p = _retention(h, zero_ret, 0, row0=0, nseq=bp, seqlen=lp, pos0=0.0, hb=4,
                           za_into=None, st_into=ret_bufs[0], st_layer=layer, depth=depth)
    za, ret_s = _retention(h, state_ret, layer, row0=mp, nseq=bs, seqlen=ls,
                           pos0=float(PAST_LEN), hb=RET_HEADS,
                           za_into=za, st_into=ret_bufs[1], st_layer=layer, depth=depth)
    zb, _ = _spatial_gating(h, sgu_ln_g, sgu_ln_b, sgu_w, sgu_b, row0=0, nrows=mp, seqlen=lp,
                            su_off=su_off, width=width, emit_v=False, zb_into=None)
    zb, sv_s = _spatial_gating(h, sgu_ln_g, sgu_ln_b, sgu_w, sgu_b, row0=mp, nrows=ms, seqlen=ls,
                               su_off=su_off, width=width, emit_v=True, zb_into=zb)

    merged = _merge(za, zb, h, w_a, w_b, layer, ga_off=ga_off, gb_off=gb_off, tm=tm_big, tn=512)
    t1 = _matmul(merged, w_o, layer, tm=tm_big, tn=512, out_dtype=F32, name="mm_o",
                 residual=x_res, res_scale=alpha)
    tm_ln = _largest_divisor(math.gcd(mp, ms), 256, 8)
    x1b, stats1 = _layer_norm(t1, ln1_g, ln1_b, row0=0, nrows=m, tm=tm_ln, emit_f32=False)

    z, tails = _up_gate(x1b, w_up, layer, conv_w, conv_b, conv_s0_s, bp=bp, lp=lp, bs=bs, ls=ls,
                        tm=tm_big, tn=_largest_divisor(dff, 256, LANES))
    t2 = _matmul(z, w_down, layer, tm=_largest_divisor(m, 272, 16), tn=512, out_dtype=F32,
                 name="mm_down", w_buffers=1, residual=("normed", t1, stats1, ln1_g, ln1_b),
                 res_scale=alpha)
    if last:
        (y_p,) = _layer_norm(t2, ln2_g, ln2_b, row0=0, nrows=mp, tm=tm_ln, emit_f32=True)
        (y_s,) = _layer_norm(t2, ln2_g, ln2_b, row0=mp, nrows=ms, tm=tm_ln, emit_f32=True)
        x2_res, x2b = (y_p, y_s), None
    else:
        x2b, stats2 = _layer_norm(t2, ln2_g, ln2_b, row0=0, nrows=m, tm=tm_ln, emit_f32=False)
        x2_res = ("normed", t2, stats2, ln2_g, ln2_b)

    conv_s = jnp.transpose(tails[:, :bs], (1, 0, 2))
    conv_p = jnp.transpose(tails[:, bs:], (1, 0, 2))
    return x2_res, x2b, (ret_p, ret_s), conv_p, conv_s, sv_s.reshape(bs, ls, width)


def kernel(x_prompt, x_sample, state_ret, state_conv, w_in, w_a, w_b, w_o, ln1_g, ln1_b,
           sgu_ln_g, sgu_ln_b, sgu_w, sgu_b, w_up, conv_w, conv_b, w_down, ln2_g, ln2_b):
    bp, lp, d = x_prompt.shape
    bs, ls, _ = x_sample.shape
    depth = w_in.shape[0]
    alpha = float((2 * depth) ** 0.25)
    xp, xs = x_prompt.reshape(bp * lp, d), x_sample.reshape(bs * ls, d)
    xb = jnp.concatenate([xp.astype(BF16), xs.astype(BF16)], axis=0)
    x = ("split", xp, xs)
    big = (w_in, w_a, w_b, w_o, w_up, w_down)
    ret_bufs = (None, None)
    conv_p, conv_s, sgu_v = [], [], []
    for l in range(depth):
        small = (ln1_g[l], ln1_b[l], sgu_ln_g[l], sgu_ln_b[l], sgu_w[l], sgu_b[l],
                 conv_w[l], conv_b[l], ln2_g[l], ln2_b[l])
        x, xb, ret_bufs, cp, cs, sv = _layer(
            l, depth, x, xb, state_ret, state_conv[l], ret_bufs, big, small,
            bp=bp, lp=lp, bs=bs, ls=ls, alpha=alpha, last=l == depth - 1)
        conv_p.append(cp)
        conv_s.append(cs)
        sgu_v.append(sv)
    y_p, y_s = x
    return (y_p.reshape(bp, lp, d), y_s.reshape(bs, ls, d), ret_bufs[0], jnp.stack(conv_p),
            ret_bufs[1], jnp.stack(conv_s), jnp.stack(sgu_v))
```

```python
import functools
import math

import jax
import jax.numpy as jnp
from jax import lax
from jax.experimental import pallas as pl
from jax.experimental.pallas import tpu as pltpu

F32 = jnp.float32
BF16 = jnp.bfloat16

RET_HEADS = 8
RET_DK = 256
RET_DV = 256
RET_QK = RET_HEADS * RET_DK
RET_V = RET_HEADS * RET_DV
SGU_GROUPS = 8
SGU_CHUNK = 128
CONV_W = 3
PAST_LEN = 1024
ROPE_BASE = 10000.0
LN_EPS = 1e-5
GN_EPS = 1e-6

V7X_VMEM_BYTES = 64 * 1024 * 1024
LANES = 128
RET_BLOCK = 256


def _vmem_limit(estimate_bytes):
    return int(min(V7X_VMEM_BYTES - (4 << 20), max(estimate_bytes * 5 // 4 + (2 << 20), 16 << 20)))


def _largest_divisor(n, limit, multiple):
    best = None
    d = multiple
    while d <= min(n, limit):
        if n % d == 0:
            best = d
        d += multiple
    assert best is not None, (n, limit, multiple)
    return best


def _mm_body(x_ref, w_ref, *rest, n_res, res_fn, res_scale, stream):
    o_ref, wbf_ref = rest[n_res], rest[n_res + 1]
    if stream is None:
        @pl.when(pl.program_id(1) == 0)
        def _():
            wbf_ref[...] = w_ref[...].astype(BF16)

        w_tile = wbf_ref[...]
    else:
        w_tile = _streamed_weight_tile(w_ref, wbf_ref, rest[n_res + 2], rest[n_res + 3], **stream)

    acc = jnp.dot(x_ref[...], w_tile, preferred_element_type=F32)
    if n_res:
        acc = res_scale * res_fn(*rest[:n_res]) + acc
    o_ref[...] = acc.astype(o_ref.dtype)


def _streamed_weight_tile(w_hbm, wbf_ref, stage_ref, sem_ref, *, layer, kc, tn, nj, ni):
    j = pl.program_id(0)
    i = pl.program_id(1)
    cur = j % 2

    def chunk_copy(tile, c, slot):
        return pltpu.make_async_copy(
            w_hbm.at[layer, pl.ds(c * kc, kc), pl.ds(tile * tn, tn)],
            stage_ref.at[slot], sem_ref.at[slot])

    def land(tile, c, slot, wslot):
        chunk_copy(tile, c, slot).wait()
        row0 = c * kc if isinstance(c, int) else pl.multiple_of(c * kc, kc)
        wbf_ref[wslot, pl.ds(row0, kc), :] = stage_ref[slot].astype(BF16)

    @pl.when((j == 0) & (i == 0))
    def _():
        chunk_copy(0, 0, 0).start()
        for c in range(ni):
            if c + 1 < ni:
                chunk_copy(0, c + 1, (c + 1) % 2).start()
            land(0, c, c % 2, 0)

    @pl.when((j > 0) & (i == 0))
    def _():
        land(j, ni - 1, (ni - 1) % 2, cur)

    @pl.when(j + 1 < nj)
    def _():
        chunk_copy(j + 1, i, i % 2).start()

        @pl.when(i > 0)
        def _():
            land(j + 1, i - 1, (i - 1) % 2, 1 - cur)

    return wbf_ref[cur]


def _res_normed(t_ref, stats_ref, g_ref, b_ref):
    reps = t_ref.shape[1] // LANES
    mu = jnp.concatenate([stats_ref[:, :LANES]] * reps, axis=1)
    rstd = jnp.concatenate([stats_ref[:, LANES:]] * reps, axis=1)
    return (t_ref[...] - mu) * rstd * g_ref[...] + b_ref[...]


def _res_split(p_ref, s_ref, *, tile, off):
    p = p_ref[...]
    n = s_ref.shape[0]
    parts = [p[:off], s_ref[...], p[off + n:]]
    mixed = jnp.concatenate([q for q in parts if q.shape[0]], axis=0)
    return jnp.where(pl.program_id(1) == tile, mixed, p)


def _matmul(x, w, layer, *, tm, tn, out_dtype, name, stream_weights=False, residual=None,
            res_scale=None):
    m, k = x.shape
    n = w.shape[2]
    nj, ni = n // tn, m // tm
    est = 2 * tm * k * 2 + 2 * tm * tn * jnp.dtype(out_dtype).itemsize
    if stream_weights:
        kc = k // ni
        assert k == kc * ni and kc % 16 == 0 and ni % 2 == 0, (k, ni)
        stream = dict(layer=layer, kc=kc, tn=tn, nj=nj, ni=ni)
        w_spec = pl.BlockSpec(memory_space=pl.ANY)
        scratch = [pltpu.VMEM((2, k, tn), BF16), pltpu.VMEM((2, kc, tn), F32),
                   pltpu.SemaphoreType.DMA((2,))]
        est += 2 * k * tn * 2 + 2 * kc * tn * 4 + 2 * kc * tn * 4
    else:
        stream = None
        w_spec = pl.BlockSpec((None, k, tn), lambda j, i: (layer, 0, j))
        scratch = [pltpu.VMEM((k, tn), BF16)]
        est += 2 * k * tn * 4 + k * tn * 2
    in_specs = [pl.BlockSpec((tm, k), lambda j, i: (i, 0)), w_spec]
    args = [x, w]
    res_fn = None
    tile_mn = pl.BlockSpec((tm, tn), lambda j, i: (i, j))
    if residual is not None:
        kind = residual[0]
        est += 2 * tm * tn * 4
        if kind == "normed":
            _, t, stats, g, b = residual
            res_fn = _res_normed
            vec = pl.BlockSpec((1, tn), lambda j, i: (0, j))
            in_specs += [tile_mn, pl.BlockSpec((tm, 2 * LANES), lambda j, i: (i, 0)), vec, vec]
            args += [t, stats, g.reshape(1, n), b.reshape(1, n)]
            est += 2 * tm * 2 * LANES * 4
        else:
            _, p, sec = residual
            mp, ms = p.shape[0], sec.shape[0]
            tile, off = divmod(mp, tm)
            assert kind == "split" and off + ms <= tm and mp + ms == m
            res_fn = functools.partial(_res_split, tile=tile, off=off)
            in_specs += [pl.BlockSpec((tm, tn), lambda j, i: (jnp.minimum(i, tile), j)),
                         pl.BlockSpec((ms, tn), lambda j, i: (0, j))]
            args += [p, sec]
            est += 2 * ms * tn * 4
    return pl.pallas_call(
        functools.partial(_mm_body, n_res=len(args) - 2, res_fn=res_fn, res_scale=res_scale,
                          stream=stream),
        grid=(nj, ni),
        in_specs=in_specs,
        out_specs=tile_mn,
        out_shape=jax.ShapeDtypeStruct((m, n), out_dtype),
        scratch_shapes=scratch,
        compiler_params=pltpu.CompilerParams(
            dimension_semantics=("arbitrary", "arbitrary"), vmem_limit_bytes=_vmem_limit(est)),
        name=name,
    )(*args)


def _merge_body(za_ref, zb_ref, ga_ref, gb_ref, wa_ref, wb_ref, o_ref, wabf_ref, wbbf_ref):
    @pl.when(pl.program_id(1) == 0)
    def _():
        wabf_ref[...] = wa_ref[...].astype(BF16)
        wbbf_ref[...] = wb_ref[...].astype(BF16)

    a = jnp.dot(za_ref[...], wabf_ref[...], preferred_element_type=F32)
    b = jnp.dot(zb_ref[...], wbbf_ref[...], preferred_element_type=F32)
    ga = jax.nn.sigmoid(ga_ref[...].astype(F32))
    gb = jax.nn.sigmoid(gb_ref[...].astype(F32))
    o_ref[...] = (ga * a + gb * b).astype(o_ref.dtype)


def _merge(za, zb, h, w_a, w_b, layer, *, ga_off, gb_off, tm, tn):
    m, ka = za.shape
    kb = zb.shape[1]
    n = w_a.shape[2]
    ga_blk, gb_blk = ga_off // tn, gb_off // tn
    est = (2 * tm * (ka + kb) * 2 + 2 * (ka + kb) * tn * 4 + (ka + kb) * tn * 2
           + 2 * 2 * tm * tn * h.dtype.itemsize + 2 * tm * tn * 2 + 4 * tm * tn * 4)
    return pl.pallas_call(
        _merge_body,
        grid=(n // tn, m // tm),
        in_specs=[pl.BlockSpec((tm, ka), lambda j, i: (i, 0)),
                  pl.BlockSpec((tm, kb), lambda j, i: (i, 0)),
                  pl.BlockSpec((tm, tn), lambda j, i: (i, ga_blk + j)),
                  pl.BlockSpec((tm, tn), lambda j, i: (i, gb_blk + j)),
                  pl.BlockSpec((None, ka, tn), lambda j, i: (layer, 0, j)),
                  pl.BlockSpec((None, kb, tn), lambda j, i: (layer, 0, j))],
        out_specs=pl.BlockSpec((tm, tn), lambda j, i: (i, j)),
        out_shape=jax.ShapeDtypeStruct((m, n), BF16),
        scratch_shapes=[pltpu.VMEM((ka, tn), BF16), pltpu.VMEM((kb, tn), BF16)],
        compiler_params=pltpu.CompilerParams(
            dimension_semantics=("arbitrary", "arbitrary"), vmem_limit_bytes=_vmem_limit(est)),
        name="merge_mm",
    )(za, zb, h, h, w_a, w_b)


def _ln_body(t_ref, g_ref, b_ref, *out_refs, emit_f32):
    t = t_ref[...]
    mu = jnp.mean(t, axis=-1, keepdims=True)
    d = t - mu
    var = jnp.mean(d * d, axis=-1, keepdims=True)
    rstd = lax.rsqrt(var + LN_EPS)
    out = d * rstd * g_ref[...] + b_ref[...]
    if emit_f32:
        out_refs[0][...] = out
    else:
        obf_ref, stats_ref = out_refs
        obf_ref[...] = out.astype(BF16)
        rows = t.shape[0]
        stats_ref[:, :LANES] = jnp.broadcast_to(mu, (rows, LANES))
        stats_ref[:, LANES:] = jnp.broadcast_to(rstd, (rows, LANES))


def _layer_norm(t, g, b, *, row0, nrows, tm, emit_f32):
    d = t.shape[1]
    rb0 = row0 // tm
    assert row0 % tm == 0 and nrows % tm == 0
    est = 2 * tm * d * 4 + 2 * tm * d * 4 + 6 * tm * d * 4
    row = pl.BlockSpec((tm, d), lambda i: (i, 0))
    vec = pl.BlockSpec((1, d), lambda i: (0, 0))
    if emit_f32:
        out_specs = [row]
        out_shape = [jax.ShapeDtypeStruct((nrows, d), F32)]
    else:
        out_specs = [row, pl.BlockSpec((tm, 2 * LANES), lambda i: (i, 0))]
        out_shape = [jax.ShapeDtypeStruct((nrows, d), BF16),
                     jax.ShapeDtypeStruct((nrows, 2 * LANES), F32)]
    return pl.pallas_call(
        functools.partial(_ln_body, emit_f32=emit_f32),
        grid=(nrows // tm,),
        in_specs=[pl.BlockSpec((tm, d), lambda i: (rb0 + i, 0)), vec, vec],
        out_specs=out_specs,
        out_shape=out_shape,
        compiler_params=pltpu.CompilerParams(
            dimension_semantics=("arbitrary",), vmem_limit_bytes=_vmem_limit(est)),
        name="layer_norm",
    )(t, g.reshape(1, d), b.reshape(1, d))


def _ret_body(q_ref, k_ref, v_ref, g_ref, cos_ref, sin_ref, lg_ref, s0_ref, za_ref, sout_ref,
              s_ref, d_ref, qd_ref, kd_ref, *, c, hb):
    b = pl.program_id(1)
    ci = pl.program_id(2)

    @pl.when((b == 0) & (ci == 0))
    def _():
        ri = lax.broadcasted_iota(jnp.int32, (c, c), 0)
        cj = lax.broadcasted_iota(jnp.int32, (c, c), 1)
        diff = (ri - cj).astype(F32)
        r = lax.broadcasted_iota(jnp.int32, (c, RET_DK), 0).astype(F32)
        for hh in range(hb):
            lg = lg_ref[hh]
            d_ref[hh] = jnp.where(diff >= 0, jnp.exp(jnp.maximum(diff, 0.0) * lg[:, :c]), 0.0)
            qd_ref[hh] = jnp.exp((r + 1.0) * lg)
            kd_ref[hh] = jnp.exp((c - 1.0 - r) * lg)

    @pl.when(ci == 0)
    def _():
        s_ref[...] = s0_ref[...]

    cos = cos_ref[...]
    sin = sin_ref[...]
    half = RET_DK // 2

    def rope(t):
        t1, t2 = t[:, :half], t[:, half:]
        return jnp.concatenate([t1 * cos - t2 * sin, t1 * sin + t2 * cos], axis=-1)

    for hh in range(hb):
        qc = slice(hh * RET_DK, (hh + 1) * RET_DK)
        vc = slice(hh * RET_DV, (hh + 1) * RET_DV)
        q = rope(q_ref[:, qc].astype(F32))
        k = rope(k_ref[:, qc].astype(F32)) * (RET_DK ** -0.5)
        vb = v_ref[:, vc].astype(BF16)
        s = s_ref[hh]
        scores = lax.dot_general(q.astype(BF16), k.astype(BF16), (((1,), (1,)), ((), ())),
                                 preferred_element_type=F32) * d_ref[hh]
        o = (jnp.dot(scores.astype(BF16), vb, preferred_element_type=F32)
             + jnp.dot((q * qd_ref[hh]).astype(BF16), s.astype(BF16), preferred_element_type=F32))
        kv = lax.dot_general((k * kd_ref[hh]).astype(BF16), vb, (((0,), (0,)), ((), ())),
                             preferred_element_type=F32)
        s_ref[hh] = s * jnp.exp(float(c) * lg_ref[hh]) + kv

        mu = jnp.mean(o, axis=-1, keepdims=True)
        od = o - mu
        var = jnp.mean(od * od, axis=-1, keepdims=True)
        on = od * lax.rsqrt(var + GN_EPS)
        za_ref[:, vc] = (jax.nn.silu(g_ref[:, vc].astype(F32)) * on).astype(BF16)

    @pl.when(ci == pl.num_programs(2) - 1)
    def _():
        sout_ref[...] = s_ref[...]


def _drop_leading(n, body):
    def wrapped(*refs):
        return body(*refs[n:])
    return wrapped


def _retention(h, s0, s0_layer, *, row0, nseq, seqlen, pos0, hb, za_into, st_into, st_layer,
               depth):
    m = h.shape[0]
    c = min(RET_BLOCK, seqlen)
    nc = seqlen // c
    rb0 = row0 // c
    hq = RET_HEADS // hb
    half = RET_DK // 2
    pos = pos0 + jnp.arange(seqlen, dtype=F32)
    freqs = ROPE_BASE ** (-jnp.arange(half, dtype=F32) / half)
    ang = pos[:, None] * freqs[None, :]
    cos, sin = jnp.cos(ang), jnp.sin(ang)
    log_gamma = jnp.log(1.0 - 2.0 ** (-5.0 - jnp.arange(RET_HEADS, dtype=F32)))
    lg = jnp.broadcast_to(log_gamma[:, None, None], (RET_HEADS, 1, RET_DK))

    def hcol(off):
        return pl.BlockSpec((c, hb * RET_DK), lambda hh, b, ci: (rb0 + b * nc + ci, off + hh))

    tab = pl.BlockSpec((c, half), lambda hh, b, ci: (ci, 0))

    def state(layer):
        return pl.BlockSpec((None, None, hb, RET_DK, RET_DV),
                            lambda hh, b, ci: (layer, b, hh, 0, 0))

    donated = [za_into, st_into]
    assert za_into.shape == (m, RET_V) and st_into.shape == (depth, nseq, RET_HEADS, RET_DK, RET_DV)
    est = (2 * 4 * c * hb * RET_DK * 4 + 2 * 2 * hb * RET_DK * RET_DV * 4 + hb * RET_DK * RET_DV * 4
           + hb * (c * c + 2 * c * RET_DK) * 4 + 2 * c * hb * RET_DV * 2 + 16 * c * RET_DK * 4)
    return pl.pallas_call(
        _drop_leading(len(donated), functools.partial(_ret_body, c=c, hb=hb)),
        grid=(hq, nseq, nc),
        in_specs=[pl.BlockSpec(memory_space=pl.ANY)] * len(donated)
        + [hcol(0), hcol(hq), hcol(2 * hq), hcol(3 * hq), tab, tab,
           pl.BlockSpec((hb, 1, RET_DK), lambda hh, b, ci: (hh, 0, 0)), state(s0_layer)],
        out_specs=[pl.BlockSpec((c, hb * RET_DV), lambda hh, b, ci: (rb0 + b * nc + ci, hh)),
                   state(st_layer)],
        out_shape=[jax.ShapeDtypeStruct((m, RET_V), BF16),
                   jax.ShapeDtypeStruct((depth, nseq, RET_HEADS, RET_DK, RET_DV), F32)],
        input_output_aliases={0: 0, 1: 1},
        scratch_shapes=[pltpu.VMEM((hb, RET_DK, RET_DV), F32), pltpu.VMEM((hb, c, c), F32),
                        pltpu.VMEM((hb, c, RET_DK), F32), pltpu.VMEM((hb, c, RET_DK), F32)],
        compiler_params=pltpu.CompilerParams(
            dimension_semantics=("arbitrary", "arbitrary", "arbitrary"),
            vmem_limit_bytes=_vmem_limit(est)),
        name="retention",
    )(*donated, h, h, h, h, cos, sin, lg, s0)


def _sgu_body(su_ref, sv_ref, lng_ref, lnb_ref, w_ref, bias_ref, zb_ref, *rest, c, tm, gdim, emit_v):
    sv = jax.nn.gelu(sv_ref[...].astype(F32))
    mu = jnp.mean(sv, axis=-1, keepdims=True)
    d = sv - mu
    var = jnp.mean(d * d, axis=-1, keepdims=True)
    svn = d * lax.rsqrt(var + LN_EPS) * lng_ref[...] + lnb_ref[...]
    if emit_v:
        rest[0][...] = svn
    svb = svn.astype(BF16)
    ri = lax.broadcasted_iota(jnp.int32, (SGU_CHUNK, SGU_CHUNK), 0)
    cj = lax.broadcasted_iota(jnp.int32, (SGU_CHUNK, SGU_CHUNK), 1)
    mask = (ri >= cj) & ((ri // c) == (cj // c))
    for g in range(SGU_GROUPS):
        wm = jnp.where(mask, w_ref[g], 0.0).astype(BF16)
        cols = slice(g * gdim, (g + 1) * gdim)
        for r in range(tm // SGU_CHUNK):
            rows = slice(r * SGU_CHUNK, (r + 1) * SGU_CHUNK)
            mixed = jnp.dot(wm, svb[rows, cols], preferred_element_type=F32) + bias_ref[:, cols]
            zb_ref[rows, cols] = (jax.nn.gelu(su_ref[rows, cols].astype(F32)) * mixed).astype(BF16)


def _spatial_gating(h, ln_g, ln_b, w_s, b_s, *, row0, nrows, seqlen, su_off, width, emit_v,
                    zb_into):
    m = h.shape[0]
    c = min(SGU_CHUNK, seqlen)
    rep = SGU_CHUNK // c
    gdim = width // SGU_GROUPS
    w_blk = jnp.tile(w_s[:, :c, :c], (1, rep, rep))
    bias = jnp.repeat(jnp.tile(b_s[:, :c], (1, rep)).T, gdim, axis=1)
    tm = _largest_divisor(nrows, 256, SGU_CHUNK)
    rb0 = row0 // tm
    su_blk = su_off // width
    assert row0 % tm == 0
    vec = pl.BlockSpec((1, width), lambda i: (0, 0))
    out_specs = [pl.BlockSpec((tm, width), lambda i: (rb0 + i, 0))]
    out_shape = [jax.ShapeDtypeStruct((m, width), BF16)]
    if emit_v:
        out_specs.append(pl.BlockSpec((tm, width), lambda i: (i, 0)))
        out_shape.append(jax.ShapeDtypeStruct((nrows, width), F32))
    donated = [zb_into]
    assert zb_into.shape == (m, width)
    est = 2 * 2 * tm * width * 4 + 2 * tm * width * (2 + 4) + 8 * tm * width * 4
    res = pl.pallas_call(
        _drop_leading(len(donated),
                      functools.partial(_sgu_body, c=c, tm=tm, gdim=gdim, emit_v=emit_v)),
        grid=(nrows // tm,),
        in_specs=[pl.BlockSpec(memory_space=pl.ANY)] * len(donated)
        + [pl.BlockSpec((tm, width), lambda i: (rb0 + i, su_blk)),
           pl.BlockSpec((tm, width), lambda i: (rb0 + i, su_blk + 1)),
           vec, vec,
           pl.BlockSpec((SGU_GROUPS, SGU_CHUNK, SGU_CHUNK), lambda i: (0, 0, 0)),
           pl.BlockSpec((SGU_CHUNK, width), lambda i: (0, 0))],
        out_specs=out_specs,
        out_shape=out_shape,
        input_output_aliases={0: 0},
        compiler_params=pltpu.CompilerParams(
            dimension_semantics=("arbitrary",), vmem_limit_bytes=_vmem_limit(est)),
        name="spatial_gating",
    )(*donated, h, h, ln_g.reshape(1, width), ln_b.reshape(1, width), w_blk, bias)
    return res if emit_v else (res[0], None)


def _rows(ref, slab, start, n, stride):
    if n == 1:
        return ref[slab, start:start + 1, :]
    return ref[slab, pl.ds(start, n, stride=stride), :]


def _set_rows(ref, slab, start, n, stride, value):
    if n == 1:
        ref[slab, start:start + 1, :] = value
    else:
        ref[slab, pl.ds(start, n, stride=stride), :] = value


def _upgate_body(x_ref, wg_ref, wv_ref, cw_ref, cb_ref, st_ref, z_ref, tail_ref,
                 wbf_ref, buf_ref, conv_ref, val_ref, *, tm, tn, starts, ends):
    i = pl.program_id(1)
    slabs = tn // LANES

    @pl.when(i == 0)
    def _():
        wbf_ref[:, :tn] = wg_ref[...].astype(BF16)
        wbf_ref[:, tn:] = wv_ref[...].astype(BF16)
        buf_ref[:, 0:8, :] = jnp.zeros((slabs, 8, LANES), F32)

    @pl.when(i > 0)
    def _():
        buf_ref[:, 0:8, :] = buf_ref[:, tm:tm + 8, :]

    gv = jnp.dot(x_ref[...], wbf_ref[...], preferred_element_type=F32)
    val_ref[...] = gv[:, tn:]
    for s in range(slabs):
        lanes = slice(s * LANES, (s + 1) * LANES)
        buf_ref[s, 8:tm + 8, :] = gv[:, lanes]
        conv_ref[s] = (cb_ref[:, lanes] + buf_ref[s, 6:tm + 6, :] * cw_ref[0:1, lanes]
                       + buf_ref[s, 7:tm + 7, :] * cw_ref[1:2, lanes]
                       + gv[:, lanes] * cw_ref[2:3, lanes])

    for tile, off, n, stride, slot in starts:
        @pl.when(i == tile)
        def _(off=off, n=n, stride=stride, slot=slot):
            for s in range(slabs):
                lanes = slice(s * LANES, (s + 1) * LANES)
                w0, w1, w2 = cw_ref[0:1, lanes], cw_ref[1:2, lanes], cw_ref[2:3, lanes]
                cb = cb_ref[:, lanes]
                p0 = _rows(buf_ref, s, 8 + off, n, stride)
                p1 = _rows(buf_ref, s, 8 + off + 1, n, stride)
                if slot is None:
                    c0 = cb + p0 * w2
                    c1 = cb + p0 * w1 + p1 * w2
                else:
                    s0 = st_ref[0, slot:slot + n, lanes]
                    s1 = st_ref[1, slot:slot + n, lanes]
                    c0 = cb + s0 * w0 + s1 * w1 + p0 * w2
                    c1 = cb + s1 * w0 + p0 * w1 + p1 * w2
                _set_rows(conv_ref, s, off, n, stride, c0)
                _set_rows(conv_ref, s, off + 1, n, stride, c1)

    for tile, off, n, stride, slot in ends:
        @pl.when(i == tile)
        def _(off=off, n=n, stride=stride, slot=slot):
            for s in range(slabs):
                lanes = slice(s * LANES, (s + 1) * LANES)
                tail_ref[0, slot:slot + n, lanes] = _rows(buf_ref, s, 8 + off - 1, n, stride)
                tail_ref[1, slot:slot + n, lanes] = _rows(buf_ref, s, 8 + off, n, stride)

    for s in range(slabs):
        lanes = slice(s * LANES, (s + 1) * LANES)
        z_ref[:, lanes] = (jax.nn.gelu(conv_ref[s]) * val_ref[:, lanes]).astype(BF16)


def _seq_groups(seq_rows, seqlen, tm, within):
    groups = []
    for idx, r0 in enumerate(seq_rows):
        row = r0 + within
        tile, off = divmod(row, tm)
        if groups and groups[-1][0] == tile and seq_rows[idx - 1] + seqlen == r0:
            t, o, n, st, first = groups[-1]
            groups[-1] = (t, o, n + 1, st, first)
        else:
            groups.append((tile, off, 1, seqlen, idx))
    return groups


def _up_gate(xb, w_up, layer, conv_w, conv_b, conv_s0_s, *, bp, lp, bs, ls, tm, tn):
    m, k = xb.shape
    dff = w_up.shape[2] // 2
    vblk = dff // tn
    nseq = bp + bs
    prompt_rows = [b * lp for b in range(bp)]
    sample_rows = [bp * lp + b * ls for b in range(bs)]
    starts, ends = [], []
    for rows, sl, has_state, slot0 in ((sample_rows, ls, True, 0), (prompt_rows, lp, False, bs)):
        for t, o, n, st, first in _seq_groups(rows, sl, tm, 0):
            assert o + (n - 1) * st + 1 < tm, "a sequence's first two rows must share a row tile"
            starts.append((t, o, n, st, first if has_state else None))
        for t, o, n, st, first in _seq_groups(rows, sl, tm, sl - 1):
            assert o >= 1, "a sequence's last two rows must share a row tile"
            ends.append((t, o, n, st, slot0 + first))
    st_in = jnp.transpose(conv_s0_s, (1, 0, 2))
    est = (2 * tm * k * 2 + 2 * 2 * k * tn * 4 + 2 * k * tn * 2 + 2 * (tm + 8) * tn * 4
           + 2 * tm * tn * 2 + 2 * 2 * (bs + nseq) * tn * 4 + 6 * tm * tn * 4)
    return pl.pallas_call(
        functools.partial(_upgate_body, tm=tm, tn=tn, starts=tuple(starts), ends=tuple(ends)),
        grid=(dff // tn, m // tm),
        in_specs=[pl.BlockSpec((tm, k), lambda j, i: (i, 0)),
                  pl.BlockSpec((None, k, tn), lambda j, i: (layer, 0, j)),
                  pl.BlockSpec((None, k, tn), lambda j, i: (layer, 0, vblk + j)),
                  pl.BlockSpec((CONV_W, tn), lambda j, i: (0, j)),
                  pl.BlockSpec((1, tn), lambda j, i: (0, j)),
                  pl.BlockSpec((CONV_W - 1, bs, tn), lambda j, i: (0, 0, j))],
        out_specs=[pl.BlockSpec((tm, tn), lambda j, i: (i, j)),
                   pl.BlockSpec((CONV_W - 1, nseq, tn), lambda j, i: (0, 0, j))],
        out_shape=[jax.ShapeDtypeStruct((m, dff), BF16),
                   jax.ShapeDtypeStruct((CONV_W - 1, nseq, dff), F32)],
        scratch_shapes=[pltpu.VMEM((k, 2 * tn), BF16),
                        pltpu.VMEM((tn // LANES, tm + 8, LANES), F32),
                        pltpu.VMEM((tn // LANES, tm, LANES), F32), pltpu.VMEM((tm, tn), F32)],
        compiler_params=pltpu.CompilerParams(
            dimension_semantics=("arbitrary", "arbitrary"), vmem_limit_bytes=_vmem_limit(est)),
        name="up_gate",
    )(xb, w_up, w_up, conv_w, conv_b.reshape(1, dff), st_in)


def _layer(layer, depth, x_res, xb, state_ret, conv_s0_s, ret_bufs, big, small, *, bp, lp, bs,
           ls, alpha, last):
    w_in, w_a, w_b, w_o, w_up, w_down = big
    ln1_g, ln1_b, sgu_ln_g, sgu_ln_b, sgu_w, sgu_b, conv_w, conv_b, ln2_g, ln2_b = small
    m, d = xb.shape
    mp = bp * lp
    ms = bs * ls
    width = d // 2
    dff = w_down.shape[1]
    su_off = 2 * RET_QK + 2 * RET_V
    ga_off = su_off + 2 * width
    gb_off = ga_off + d
    tm_big = _largest_divisor(m, 1088, 16)

    h = _matmul(xb, w_in, layer, tm=tm_big, tn=1024, out_dtype=BF16, name="mm_in",
                stream_weights=True)

    zero_ret = jnp.zeros((1, bp, RET_HEADS, RET_DK, RET_DV), F32)
    za, ret_p = _retention(h, zero_ret, 0, row0=0, nseq=bp, seqlen=lp, pos0=0.0, hb=4,
                           za_into=jnp.zeros((m, RET_V), BF16), st_into=ret_bufs[0],
                           st_layer=layer, depth=depth)
    za, ret_s = _retention(h, state_ret, layer, row0=mp, nseq=bs, seqlen=ls,
                           pos0=float(PAST_LEN), hb=RET_HEADS,
                           za_into=za, st_into=ret_bufs[1], st_layer=layer, depth=depth)
    zb, _ = _spatial_gating(h, sgu_ln_g, sgu_ln_b, sgu_w, sgu_b, row0=0, nrows=mp, seqlen=lp,
                            su_off=su_off, width=width, emit_v=False,
                            zb_into=jnp.zeros((m, width), BF16))
    zb, sv_s = _spatial_gating(h, sgu_ln_g, sgu_ln_b, sgu_w, sgu_b, row0=mp, nrows=ms, seqlen=ls,
                               su_off=su_off, width=width, emit_v=True, zb_into=zb)

    merged = _merge(za, zb, h, w_a, w_b, layer, ga_off=ga_off, gb_off=gb_off, tm=tm_big, tn=512)
    t1 = _matmul(merged, w_o, layer, tm=tm_big, tn=512, out_dtype=F32, name="mm_o",
                 residual=x_res, res_scale=alpha)
    tm_ln = _largest_divisor(math.gcd(mp, ms), 256, 8)
    x1b, stats1 = _layer_norm(t1, ln1_g, ln1_b, row0=0, nrows=m, tm=tm_ln, emit_f32=False)

    z, tails = _up_gate(x1b, w_up, layer, conv_w, conv_b, conv_s0_s, bp=bp, lp=lp, bs=bs, ls=ls,
                        tm=tm_big, tn=_largest_divisor(dff, 256, LANES))
    t2 = _matmul(z, w_down, layer, tm=_largest_divisor(m, 544, 16), tn=512, out_dtype=F32,
                 name="mm_down", stream_weights=True,
                 residual=("normed", t1, stats1, ln1_g, ln1_b), res_scale=alpha)
    if last:
        (y_p,) = _layer_norm(t2, ln2_g, ln2_b, row0=0, nrows=mp, tm=tm_ln, emit_f32=True)
        (y_s,) = _layer_norm(t2, ln2_g, ln2_b, row0=mp, nrows=ms, tm=tm_ln, emit_f32=True)
        x2_res, x2b = (y_p, y_s), None
    else:
        x2b, stats2 = _layer_norm(t2, ln2_g, ln2_b, row0=0, nrows=m, tm=tm_ln, emit_f32=False)
        x2_res = ("normed", t2, stats2, ln2_g, ln2_b)

    conv_s = jnp.transpose(tails[:, :bs], (1, 0, 2))
    conv_p = jnp.transpose(tails[:, bs:], (1, 0, 2))
    return x2_res, x2b, (ret_p, ret_s), conv_p, conv_s, sv_s.reshape(bs, ls, width)


def kernel(x_prompt, x_sample, state_ret, state_conv, w_in, w_a, w_b, w_o, ln1_g, ln1_b,
           sgu_ln_g, sgu_ln_b, sgu_w, sgu_b, w_up, conv_w, conv_b, w_down, ln2_g, ln2_b):
    bp, lp, d = x_prompt.shape
    bs, ls, _ = x_sample.shape
    depth = w_in.shape[0]
    alpha = float((2 * depth) ** 0.25)
    xp, xs = x_prompt.reshape(bp * lp, d), x_sample.reshape(bs * ls, d)
    xb = jnp.concatenate([xp, xs], axis=0).astype(BF16)
    x = ("split", xp, xs)
    big = (w_in, w_a, w_b, w_o, w_up, w_down)
    ret_bufs = tuple(jnp.zeros((depth, n, RET_HEADS, RET_DK, RET_DV), F32) for n in (bp, bs))
    conv_p, conv_s, sgu_v = [], [], []
    for l in range(depth):
        small = (ln1_g[l], ln1_b[l], sgu_ln_g[l], sgu_ln_b[l], sgu_w[l], sgu_b[l],
                 conv_w[l], conv_b[l], ln2_g[l], ln2_b[l])
        x, xb, ret_bufs, cp, cs, sv = _layer(
            l, depth, x, xb, state_ret, state_conv[l], ret_bufs, big, small,
            bp=bp, lp=lp, bs=bs, ls=ls, alpha=alpha, last=l == depth - 1)
        conv_p.append(cp)
        conv_s.append(cs)
        sgu_v.append(sv)
    y_p, y_s = x
    return (y_p.reshape(bp, lp, d), y_s.reshape(bs, ls, d), ret_bufs[0], jnp.stack(conv_p),
            ret_bufs[1], jnp.stack(conv_s), jnp.stack(sgu_v))
```

```python
import functools
import math

import jax
import jax.numpy as jnp
from jax import lax
from jax.experimental import pallas as pl
from jax.experimental.pallas import tpu as pltpu

F32 = jnp.float32
BF16 = jnp.bfloat16

RET_HEADS = 8
RET_DK = 256
RET_DV = 256
RET_QK = RET_HEADS * RET_DK
RET_V = RET_HEADS * RET_DV
SGU_GROUPS = 8
SGU_CHUNK = 128
CONV_W = 3
PAST_LEN = 1024
ROPE_BASE = 10000.0
LN_EPS = 1e-5
GN_EPS = 1e-6

V7X_VMEM_BYTES = 64 * 1024 * 1024
LANES = 128
RET_BLOCK = 256


def _vmem_limit(estimate_bytes):
    return int(min(V7X_VMEM_BYTES - (4 << 20), max(estimate_bytes * 5 // 4 + (2 << 20), 16 << 20)))


def _largest_divisor(n, limit, multiple):
    best = None
    d = multiple
    while d <= min(n, limit):
        if n % d == 0:
            best = d
        d += multiple
    assert best is not None, (n, limit, multiple)
    return best


def _mm_body(x_ref, w_ref, *rest, n_res, res_fn, res_scale, stream):
    o_ref, wbf_ref = rest[n_res], rest[n_res + 1]
    if stream is None:
        @pl.when(pl.program_id(1) == 0)
        def _():
            wbf_ref[...] = w_ref[...].astype(BF16)

        w_tile = wbf_ref[...]
    else:
        w_tile = _streamed_weight_tile(w_ref, wbf_ref, rest[n_res + 2], rest[n_res + 3], **stream)

    acc = jnp.dot(x_ref[...], w_tile, preferred_element_type=F32)
    if n_res:
        acc = res_scale * res_fn(*rest[:n_res]) + acc
    o_ref[...] = acc.astype(o_ref.dtype)


def _streamed_weight_tile(w_hbm, wbf_ref, stage_ref, sem_ref, *, layer, kc, tn, nj, ni):
    j = pl.program_id(0)
    i = pl.program_id(1)
    cur = j % 2

    def chunk_copy(tile, c, slot):
        return pltpu.make_async_copy(
            w_hbm.at[layer, pl.ds(c * kc, kc), pl.ds(tile * tn, tn)],
            stage_ref.at[slot], sem_ref.at[slot])

    def land(tile, c, slot, wslot):
        chunk_copy(tile, c, slot).wait()
        row0 = c * kc if isinstance(c, int) else pl.multiple_of(c * kc, kc)
        wbf_ref[wslot, pl.ds(row0, kc), :] = stage_ref[slot].astype(BF16)

    @pl.when((j == 0) & (i == 0))
    def _():
        chunk_copy(0, 0, 0).start()
        for c in range(ni):
            if c + 1 < ni:
                chunk_copy(0, c + 1, (c + 1) % 2).start()
            land(0, c, c % 2, 0)

    @pl.when((j > 0) & (i == 0))
    def _():
        land(j, ni - 1, (ni - 1) % 2, cur)

    @pl.when(j + 1 < nj)
    def _():
        chunk_copy(j + 1, i, i % 2).start()

        @pl.when(i > 0)
        def _():
            land(j + 1, i - 1, (i - 1) % 2, 1 - cur)

    return wbf_ref[cur]


def _res_normed(t_ref, stats_ref, g_ref, b_ref):
    reps = t_ref.shape[1] // LANES
    mu = jnp.concatenate([stats_ref[:, :LANES]] * reps, axis=1)
    rstd = jnp.concatenate([stats_ref[:, LANES:]] * reps, axis=1)
    return (t_ref[...] - mu) * rstd * g_ref[...] + b_ref[...]


def _res_split(p_ref, s_ref, *, tile, off, axis):
    p = p_ref[...]
    n = s_ref.shape[0]
    parts = [p[:off], s_ref[...], p[off + n:]]
    mixed = jnp.concatenate([q for q in parts if q.shape[0]], axis=0)
    return jnp.where(pl.program_id(axis) == tile, mixed, p)


def _cast_body(p_ref, s_ref, o_ref, *, tile, off):
    o_ref[...] = _res_split(p_ref, s_ref, tile=tile, off=off, axis=0).astype(o_ref.dtype)


def _concat_cast(p, sec, *, tm, dtype):
    mp, d = p.shape
    ms = sec.shape[0]
    tile, off = divmod(mp, tm)
    assert off + ms <= tm and (mp + ms) % tm == 0
    last_p = (mp - 1) // tm
    est = 2 * 2 * tm * d * 4 + 2 * tm * d * jnp.dtype(dtype).itemsize + 2 * tm * d * 4
    return pl.pallas_call(
        functools.partial(_cast_body, tile=tile, off=off),
        grid=((mp + ms) // tm,),
        in_specs=[pl.BlockSpec((tm, d), lambda i: (jnp.minimum(i, last_p), 0)),
                  pl.BlockSpec((ms, d), lambda i: (0, 0))],
        out_specs=pl.BlockSpec((tm, d), lambda i: (i, 0)),
        out_shape=jax.ShapeDtypeStruct((mp + ms, d), dtype),
        compiler_params=pltpu.CompilerParams(
            dimension_semantics=("arbitrary",), vmem_limit_bytes=_vmem_limit(est)),
        name="concat_cast",
    )(p, sec)


def _matmul(x, w, layer, *, tm, tn, out_dtype, name, stream_weights=False, residual=None,
            res_scale=None):
    m, k = x.shape
    n = w.shape[2]
    nj, ni = n // tn, m // tm
    est = 2 * tm * k * 2 + 2 * tm * tn * jnp.dtype(out_dtype).itemsize
    if stream_weights:
        kc = k // ni
        assert k == kc * ni and kc % 16 == 0 and ni % 2 == 0, (k, ni)
        stream = dict(layer=layer, kc=kc, tn=tn, nj=nj, ni=ni)
        w_spec = pl.BlockSpec(memory_space=pl.ANY)
        scratch = [pltpu.VMEM((2, k, tn), BF16), pltpu.VMEM((2, kc, tn), F32),
                   pltpu.SemaphoreType.DMA((2,))]
        est += 2 * k * tn * 2 + 2 * kc * tn * 4 + 2 * kc * tn * 4
    else:
        stream = None
        w_spec = pl.BlockSpec((None, k, tn), lambda j, i: (layer, 0, j))
        scratch = [pltpu.VMEM((k, tn), BF16)]
        est += 2 * k * tn * 4 + k * tn * 2
    in_specs = [pl.BlockSpec((tm, k), lambda j, i: (i, 0)), w_spec]
    args = [x, w]
    res_fn = None
    tile_mn = pl.BlockSpec((tm, tn), lambda j, i: (i, j))
    if residual is not None:
        kind = residual[0]
        est += 2 * tm * tn * 4
        if kind == "normed":
            _, t, stats, g, b = residual
            res_fn = _res_normed
            vec = pl.BlockSpec((1, tn), lambda j, i: (0, j))
            in_specs += [tile_mn, pl.BlockSpec((tm, 2 * LANES), lambda j, i: (i, 0)), vec, vec]
            args += [t, stats, g.reshape(1, n), b.reshape(1, n)]
            est += 2 * tm * 2 * LANES * 4
        else:
            _, p, sec = residual
            mp, ms = p.shape[0], sec.shape[0]
            tile, off = divmod(mp, tm)
            assert kind == "split" and off + ms <= tm and mp + ms == m
            last_p = (mp - 1) // tm
            res_fn = functools.partial(_res_split, tile=tile, off=off, axis=1)
            in_specs += [pl.BlockSpec((tm, tn), lambda j, i: (jnp.minimum(i, last_p), j)),
                         pl.BlockSpec((ms, tn), lambda j, i: (0, j))]
            args += [p, sec]
            est += 2 * ms * tn * 4
    return pl.pallas_call(
        functools.partial(_mm_body, n_res=len(args) - 2, res_fn=res_fn, res_scale=res_scale,
                          stream=stream),
        grid=(nj, ni),
        in_specs=in_specs,
        out_specs=tile_mn,
        out_shape=jax.ShapeDtypeStruct((m, n), out_dtype),
        scratch_shapes=scratch,
        compiler_params=pltpu.CompilerParams(
            dimension_semantics=("arbitrary", "arbitrary"), vmem_limit_bytes=_vmem_limit(est)),
        name=name,
    )(*args)


def _merge_body(za_ref, zb_ref, ga_ref, gb_ref, wa_ref, wb_ref, o_ref, wabf_ref, wbbf_ref):
    @pl.when(pl.program_id(1) == 0)
    def _():
        wabf_ref[...] = wa_ref[...].astype(BF16)
        wbbf_ref[...] = wb_ref[...].astype(BF16)

    a = jnp.dot(za_ref[...], wabf_ref[...], preferred_element_type=F32)
    b = jnp.dot(zb_ref[...], wbbf_ref[...], preferred_element_type=F32)
    ga = jax.nn.sigmoid(ga_ref[...].astype(F32))
    gb = jax.nn.sigmoid(gb_ref[...].astype(F32))
    o_ref[...] = (ga * a + gb * b).astype(o_ref.dtype)


def _merge(za, zb, h, w_a, w_b, layer, *, ga_off, gb_off, tm, tn):
    m, ka = za.shape
    kb = zb.shape[1]
    n = w_a.shape[2]
    ga_blk, gb_blk = ga_off // tn, gb_off // tn
    est = (2 * tm * (ka + kb) * 2 + 2 * (ka + kb) * tn * 4 + (ka + kb) * tn * 2
           + 2 * 2 * tm * tn * h.dtype.itemsize + 2 * tm * tn * 2 + 4 * tm * tn * 4)
    return pl.pallas_call(
        _merge_body,
        grid=(n // tn, m // tm),
        in_specs=[pl.BlockSpec((tm, ka), lambda j, i: (i, 0)),
                  pl.BlockSpec((tm, kb), lambda j, i: (i, 0)),
                  pl.BlockSpec((tm, tn), lambda j, i: (i, ga_blk + j)),
                  pl.BlockSpec((tm, tn), lambda j, i: (i, gb_blk + j)),
                  pl.BlockSpec((None, ka, tn), lambda j, i: (layer, 0, j)),
                  pl.BlockSpec((None, kb, tn), lambda j, i: (layer, 0, j))],
        out_specs=pl.BlockSpec((tm, tn), lambda j, i: (i, j)),
        out_shape=jax.ShapeDtypeStruct((m, n), BF16),
        scratch_shapes=[pltpu.VMEM((ka, tn), BF16), pltpu.VMEM((kb, tn), BF16)],
        compiler_params=pltpu.CompilerParams(
            dimension_semantics=("arbitrary", "arbitrary"), vmem_limit_bytes=_vmem_limit(est)),
        name="merge_mm",
    )(za, zb, h, h, w_a, w_b)


def _ln_body(t_ref, g_ref, b_ref, *out_refs, emit_f32):
    t = t_ref[...]
    mu = jnp.mean(t, axis=-1, keepdims=True)
    d = t - mu
    var = jnp.mean(d * d, axis=-1, keepdims=True)
    rstd = lax.rsqrt(var + LN_EPS)
    out = d * rstd * g_ref[...] + b_ref[...]
    if emit_f32:
        out_refs[0][...] = out
    else:
        obf_ref, stats_ref = out_refs
        obf_ref[...] = out.astype(BF16)
        rows = t.shape[0]
        stats_ref[:, :LANES] = jnp.broadcast_to(mu, (rows, LANES))
        stats_ref[:, LANES:] = jnp.broadcast_to(rstd, (rows, LANES))


def _layer_norm(t, g, b, *, row0, nrows, tm, emit_f32):
    d = t.shape[1]
    rb0 = row0 // tm
    assert row0 % tm == 0 and nrows % tm == 0
    est = 2 * tm * d * 4 + 2 * tm * d * 4 + 6 * tm * d * 4
    row = pl.BlockSpec((tm, d), lambda i: (i, 0))
    vec = pl.BlockSpec((1, d), lambda i: (0, 0))
    if emit_f32:
        out_specs = [row]
        out_shape = [jax.ShapeDtypeStruct((nrows, d), F32)]
    else:
        out_specs = [row, pl.BlockSpec((tm, 2 * LANES), lambda i: (i, 0))]
        out_shape = [jax.ShapeDtypeStruct((nrows, d), BF16),
                     jax.ShapeDtypeStruct((nrows, 2 * LANES), F32)]
    return pl.pallas_call(
        functools.partial(_ln_body, emit_f32=emit_f32),
        grid=(nrows // tm,),
        in_specs=[pl.BlockSpec((tm, d), lambda i: (rb0 + i, 0)), vec, vec],
        out_specs=out_specs,
        out_shape=out_shape,
        compiler_params=pltpu.CompilerParams(
            dimension_semantics=("arbitrary",), vmem_limit_bytes=_vmem_limit(est)),
        name="layer_norm",
    )(t, g.reshape(1, d), b.reshape(1, d))


def _ret_body(q_ref, k_ref, v_ref, g_ref, cos_ref, sin_ref, lg_ref, s0_ref, za_ref, sout_ref,
              s_ref, d_ref, qd_ref, kd_ref, *, c, hb):
    b = pl.program_id(1)
    ci = pl.program_id(2)

    @pl.when((b == 0) & (ci == 0))
    def _():
        ri = lax.broadcasted_iota(jnp.int32, (c, c), 0)
        cj = lax.broadcasted_iota(jnp.int32, (c, c), 1)
        diff = (ri - cj).astype(F32)
        r = lax.broadcasted_iota(jnp.int32, (c, RET_DK), 0).astype(F32)
        for hh in range(hb):
            lg = lg_ref[hh]
            d_ref[hh] = jnp.where(diff >= 0, jnp.exp(jnp.maximum(diff, 0.0) * lg[:, :c]), 0.0)
            qd_ref[hh] = jnp.exp((r + 1.0) * lg)
            kd_ref[hh] = jnp.exp((c - 1.0 - r) * lg)

    @pl.when(ci == 0)
    def _():
        s_ref[...] = s0_ref[...]

    cos = cos_ref[...]
    sin = sin_ref[...]
    half = RET_DK // 2

    def rope(t):
        t1, t2 = t[:, :half], t[:, half:]
        return jnp.concatenate([t1 * cos - t2 * sin, t1 * sin + t2 * cos], axis=-1)

    for hh in range(hb):
        qc = slice(hh * RET_DK, (hh + 1) * RET_DK)
        vc = slice(hh * RET_DV, (hh + 1) * RET_DV)
        q = rope(q_ref[:, qc].astype(F32))
        k = rope(k_ref[:, qc].astype(F32)) * (RET_DK ** -0.5)
        vb = v_ref[:, vc].astype(BF16)
        s = s_ref[hh]
        scores = lax.dot_general(q.astype(BF16), k.astype(BF16), (((1,), (1,)), ((), ())),
                                 preferred_element_type=F32) * d_ref[hh]
        o = (jnp.dot(scores.astype(BF16), vb, preferred_element_type=F32)
             + jnp.dot((q * qd_ref[hh]).astype(BF16), s.astype(BF16), preferred_element_type=F32))
        kv = lax.dot_general((k * kd_ref[hh]).astype(BF16), vb, (((0,), (0,)), ((), ())),
                             preferred_element_type=F32)
        s_ref[hh] = s * jnp.exp(float(c) * lg_ref[hh]) + kv

        mu = jnp.mean(o, axis=-1, keepdims=True)
        od = o - mu
        var = jnp.mean(od * od, axis=-1, keepdims=True)
        on = od * lax.rsqrt(var + GN_EPS)
        za_ref[:, vc] = (jax.nn.silu(g_ref[:, vc].astype(F32)) * on).astype(BF16)

    @pl.when(ci == pl.num_programs(2) - 1)
    def _():
        sout_ref[...] = s_ref[...]


def _drop_leading(n, body):
    def wrapped(*refs):
        return body(*refs[n:])
    return wrapped


def _retention(h, s0, s0_layer, *, row0, nseq, seqlen, pos0, hb, za_into, st_into, st_layer,
               depth):
    m = h.shape[0]
    c = min(RET_BLOCK, seqlen)
    nc = seqlen // c
    rb0 = row0 // c
    hq = RET_HEADS // hb
    half = RET_DK // 2
    pos = pos0 + jnp.arange(seqlen, dtype=F32)
    freqs = ROPE_BASE ** (-jnp.arange(half, dtype=F32) / half)
    ang = pos[:, None] * freqs[None, :]
    cos, sin = jnp.cos(ang), jnp.sin(ang)
    log_gamma = jnp.log(1.0 - 2.0 ** (-5.0 - jnp.arange(RET_HEADS, dtype=F32)))
    lg = jnp.broadcast_to(log_gamma[:, None, None], (RET_HEADS, 1, RET_DK))

    def hcol(off):
        return pl.BlockSpec((c, hb * RET_DK), lambda hh, b, ci: (rb0 + b * nc + ci, off + hh))

    tab = pl.BlockSpec((c, half), lambda hh, b, ci: (ci, 0))

    def state(layer):
        return pl.BlockSpec((None, None, hb, RET_DK, RET_DV),
                            lambda hh, b, ci: (layer, b, hh, 0, 0))

    donated = [za_into, st_into]
    assert za_into.shape == (m, RET_V) and st_into.shape == (depth, nseq, RET_HEADS, RET_DK, RET_DV)
    est = (2 * 4 * c * hb * RET_DK * 4 + 2 * 2 * hb * RET_DK * RET_DV * 4 + hb * RET_DK * RET_DV * 4
           + hb * (c * c + 2 * c * RET_DK) * 4 + 2 * c * hb * RET_DV * 2 + 16 * c * RET_DK * 4)
    return pl.pallas_call(
        _drop_leading(len(donated), functools.partial(_ret_body, c=c, hb=hb)),
        grid=(hq, nseq, nc),
        in_specs=[pl.BlockSpec(memory_space=pl.ANY)] * len(donated)
        + [hcol(0), hcol(hq), hcol(2 * hq), hcol(3 * hq), tab, tab,
           pl.BlockSpec((hb, 1, RET_DK), lambda hh, b, ci: (hh, 0, 0)), state(s0_layer)],
        out_specs=[pl.BlockSpec((c, hb * RET_DV), lambda hh, b, ci: (rb0 + b * nc + ci, hh)),
                   state(st_layer)],
        out_shape=[jax.ShapeDtypeStruct((m, RET_V), BF16),
                   jax.ShapeDtypeStruct((depth, nseq, RET_HEADS, RET_DK, RET_DV), F32)],
        input_output_aliases={0: 0, 1: 1},
        scratch_shapes=[pltpu.VMEM((hb, RET_DK, RET_DV), F32), pltpu.VMEM((hb, c, c), F32),
                        pltpu.VMEM((hb, c, RET_DK), F32), pltpu.VMEM((hb, c, RET_DK), F32)],
        compiler_params=pltpu.CompilerParams(
            dimension_semantics=("arbitrary", "arbitrary", "arbitrary"),
            vmem_limit_bytes=_vmem_limit(est)),
        name="retention",
    )(*donated, h, h, h, h, cos, sin, lg, s0)


def _sgu_body(su_ref, sv_ref, lng_ref, lnb_ref, w_ref, bias_ref, zb_ref, *rest, c, tm, gdim, emit_v):
    sv = jax.nn.gelu(sv_ref[...].astype(F32))
    mu = jnp.mean(sv, axis=-1, keepdims=True)
    d = sv - mu
    var = jnp.mean(d * d, axis=-1, keepdims=True)
    svn = d * lax.rsqrt(var + LN_EPS) * lng_ref[...] + lnb_ref[...]
    if emit_v:
        rest[0][...] = svn
    svb = svn.astype(BF16)
    ri = lax.broadcasted_iota(jnp.int32, (SGU_CHUNK, SGU_CHUNK), 0)
    cj = lax.broadcasted_iota(jnp.int32, (SGU_CHUNK, SGU_CHUNK), 1)
    mask = (ri >= cj) & ((ri // c) == (cj // c))
    for g in range(SGU_GROUPS):
        wm = jnp.where(mask, w_ref[g], 0.0).astype(BF16)
        cols = slice(g * gdim, (g + 1) * gdim)
        for r in range(tm // SGU_CHUNK):
            rows = slice(r * SGU_CHUNK, (r + 1) * SGU_CHUNK)
            mixed = jnp.dot(wm, svb[rows, cols], preferred_element_type=F32) + bias_ref[:, cols]
            zb_ref[rows, cols] = (jax.nn.gelu(su_ref[rows, cols].astype(F32)) * mixed).astype(BF16)


def _spatial_gating(h, ln_g, ln_b, w_s, b_s, *, row0, nrows, seqlen, su_off, width, emit_v,
                    zb_into):
    m = h.shape[0]
    c = min(SGU_CHUNK, seqlen)
    rep = SGU_CHUNK // c
    gdim = width // SGU_GROUPS
    w_blk = jnp.tile(w_s[:, :c, :c], (1, rep, rep))
    bias = jnp.repeat(jnp.tile(b_s[:, :c], (1, rep)).T, gdim, axis=1)
    tm = _largest_divisor(nrows, 256, SGU_CHUNK)
    rb0 = row0 // tm
    su_blk = su_off // width
    assert row0 % tm == 0
    vec = pl.BlockSpec((1, width), lambda i: (0, 0))
    out_specs = [pl.BlockSpec((tm, width), lambda i: (rb0 + i, 0))]
    out_shape = [jax.ShapeDtypeStruct((m, width), BF16)]
    if emit_v:
        out_specs.append(pl.BlockSpec((tm, width), lambda i: (i, 0)))
        out_shape.append(jax.ShapeDtypeStruct((nrows, width), F32))
    donated = [zb_into]
    assert zb_into.shape == (m, width)
    est = 2 * 2 * tm * width * 4 + 2 * tm * width * (2 + 4) + 8 * tm * width * 4
    res = pl.pallas_call(
        _drop_leading(len(donated),
                      functools.partial(_sgu_body, c=c, tm=tm, gdim=gdim, emit_v=emit_v)),
        grid=(nrows // tm,),
        in_specs=[pl.BlockSpec(memory_space=pl.ANY)] * len(donated)
        + [pl.BlockSpec((tm, width), lambda i: (rb0 + i, su_blk)),
           pl.BlockSpec((tm, width), lambda i: (rb0 + i, su_blk + 1)),
           vec, vec,
           pl.BlockSpec((SGU_GROUPS, SGU_CHUNK, SGU_CHUNK), lambda i: (0, 0, 0)),
           pl.BlockSpec((SGU_CHUNK, width), lambda i: (0, 0))],
        out_specs=out_specs,
        out_shape=out_shape,
        input_output_aliases={0: 0},
        compiler_params=pltpu.CompilerParams(
            dimension_semantics=("arbitrary",), vmem_limit_bytes=_vmem_limit(est)),
        name="spatial_gating",
    )(*donated, h, h, ln_g.reshape(1, width), ln_b.reshape(1, width), w_blk, bias)
    return res if emit_v else (res[0], None)


def _rows(ref, slab, start, n, stride):
    if n == 1:
        return ref[slab, start:start + 1, :]
    return ref[slab, pl.ds(start, n, stride=stride), :]


def _set_rows(ref, slab, start, n, stride, value):
    if n == 1:
        ref[slab, start:start + 1, :] = value
    else:
        ref[slab, pl.ds(start, n, stride=stride), :] = value


def _upgate_body(x_ref, wg_ref, wv_ref, cw_ref, cb_ref, st_ref, z_ref, tail_ref,
                 wbf_ref, buf_ref, conv_ref, val_ref, *, tm, tn, starts, ends):
    i = pl.program_id(1)
    slabs = tn // LANES

    @pl.when(i == 0)
    def _():
        wbf_ref[:, :tn] = wg_ref[...].astype(BF16)
        wbf_ref[:, tn:] = wv_ref[...].astype(BF16)
        buf_ref[:, 0:8, :] = jnp.zeros((slabs, 8, LANES), F32)

    @pl.when(i > 0)
    def _():
        buf_ref[:, 0:8, :] = buf_ref[:, tm:tm + 8, :]

    gv = jnp.dot(x_ref[...], wbf_ref[...], preferred_element_type=F32)
    val_ref[...] = gv[:, tn:]
    for s in range(slabs):
        lanes = slice(s * LANES, (s + 1) * LANES)
        buf_ref[s, 8:tm + 8, :] = gv[:, lanes]
        conv_ref[s] = (cb_ref[:, lanes] + buf_ref[s, 6:tm + 6, :] * cw_ref[0:1, lanes]
                       + buf_ref[s, 7:tm + 7, :] * cw_ref[1:2, lanes]
                       + gv[:, lanes] * cw_ref[2:3, lanes])

    for tile, off, n, stride, slot in starts:
        @pl.when(i == tile)
        def _(off=off, n=n, stride=stride, slot=slot):
            for s in range(slabs):
                lanes = slice(s * LANES, (s + 1) * LANES)
                w0, w1, w2 = cw_ref[0:1, lanes], cw_ref[1:2, lanes], cw_ref[2:3, lanes]
                cb = cb_ref[:, lanes]
                p0 = _rows(buf_ref, s, 8 + off, n, stride)
                p1 = _rows(buf_ref, s, 8 + off + 1, n, stride)
                if slot is None:
                    c0 = cb + p0 * w2
                    c1 = cb + p0 * w1 + p1 * w2
                else:
                    s0 = st_ref[0, slot:slot + n, lanes]
                    s1 = st_ref[1, slot:slot + n, lanes]
                    c0 = cb + s0 * w0 + s1 * w1 + p0 * w2
                    c1 = cb + s1 * w0 + p0 * w1 + p1 * w2
                _set_rows(conv_ref, s, off, n, stride, c0)
                _set_rows(conv_ref, s, off + 1, n, stride, c1)

    for tile, off, n, stride, slot in ends:
        @pl.when(i == tile)
        def _(off=off, n=n, stride=stride, slot=slot):
            for s in range(slabs):
                lanes = slice(s * LANES, (s + 1) * LANES)
                tail_ref[0, slot:slot + n, lanes] = _rows(buf_ref, s, 8 + off - 1, n, stride)
                tail_ref[1, slot:slot + n, lanes] = _rows(buf_ref, s, 8 + off, n, stride)

    for s in range(slabs):
        lanes = slice(s * LANES, (s + 1) * LANES)
        z_ref[:, lanes] = (jax.nn.gelu(conv_ref[s]) * val_ref[:, lanes]).astype(BF16)


def _seq_groups(seq_rows, seqlen, tm, within):
    groups = []
    for idx, r0 in enumerate(seq_rows):
        row = r0 + within
        tile, off = divmod(row, tm)
        if groups and groups[-1][0] == tile and seq_rows[idx - 1] + seqlen == r0:
            t, o, n, st, first = groups[-1]
            groups[-1] = (t, o, n + 1, st, first)
        else:
            groups.append((tile, off, 1, seqlen, idx))
    return groups


def _up_gate(xb, w_up, layer, conv_w, conv_b, conv_s0_s, *, bp, lp, bs, ls, tm, tn):
    m, k = xb.shape
    dff = w_up.shape[2] // 2
    vblk = dff // tn
    nseq = bp + bs
    prompt_rows = [b * lp for b in range(bp)]
    sample_rows = [bp * lp + b * ls for b in range(bs)]
    starts, ends = [], []
    for rows, sl, has_state, slot0 in ((sample_rows, ls, True, 0), (prompt_rows, lp, False, bs)):
        for t, o, n, st, first in _seq_groups(rows, sl, tm, 0):
            assert o + (n - 1) * st + 1 < tm, "a sequence's first two rows must share a row tile"
            starts.append((t, o, n, st, first if has_state else None))
        for t, o, n, st, first in _seq_groups(rows, sl, tm, sl - 1):
            assert o >= 1, "a sequence's last two rows must share a row tile"
            ends.append((t, o, n, st, slot0 + first))
    st_in = jnp.transpose(conv_s0_s, (1, 0, 2))
    est = (2 * tm * k * 2 + 2 * 2 * k * tn * 4 + 2 * k * tn * 2 + 2 * (tm + 8) * tn * 4
           + 2 * tm * tn * 2 + 2 * 2 * (bs + nseq) * tn * 4 + 6 * tm * tn * 4)
    return pl.pallas_call(
        functools.partial(_upgate_body, tm=tm, tn=tn, starts=tuple(starts), ends=tuple(ends)),
        grid=(dff // tn, m // tm),
        in_specs=[pl.BlockSpec((tm, k), lambda j, i: (i, 0)),
                  pl.BlockSpec((None, k, tn), lambda j, i: (layer, 0, j)),
                  pl.BlockSpec((None, k, tn), lambda j, i: (layer, 0, vblk + j)),
                  pl.BlockSpec((CONV_W, tn), lambda j, i: (0, j)),
                  pl.BlockSpec((1, tn), lambda j, i: (0, j)),
                  pl.BlockSpec((CONV_W - 1, bs, tn), lambda j, i: (0, 0, j))],
        out_specs=[pl.BlockSpec((tm, tn), lambda j, i: (i, j)),
                   pl.BlockSpec((CONV_W - 1, nseq, tn), lambda j, i: (0, 0, j))],
        out_shape=[jax.ShapeDtypeStruct((m, dff), BF16),
                   jax.ShapeDtypeStruct((CONV_W - 1, nseq, dff), F32)],
        scratch_shapes=[pltpu.VMEM((k, 2 * tn), BF16),
                        pltpu.VMEM((tn // LANES, tm + 8, LANES), F32),
                        pltpu.VMEM((tn // LANES, tm, LANES), F32), pltpu.VMEM((tm, tn), F32)],
        compiler_params=pltpu.CompilerParams(
            dimension_semantics=("arbitrary", "arbitrary"), vmem_limit_bytes=_vmem_limit(est)),
        name="up_gate",
    )(xb, w_up, w_up, conv_w, conv_b.reshape(1, dff), st_in)


def _layer(layer, depth, x_res, xb, state_ret, conv_s0_s, ret_bufs, big, small, *, bp, lp, bs,
           ls, alpha, last):
    w_in, w_a, w_b, w_o, w_up, w_down = big
    ln1_g, ln1_b, sgu_ln_g, sgu_ln_b, sgu_w, sgu_b, conv_w, conv_b, ln2_g, ln2_b = small
    m, d = xb.shape
    mp = bp * lp
    ms = bs * ls
    width = d // 2
    dff = w_down.shape[1]
    su_off = 2 * RET_QK + 2 * RET_V
    ga_off = su_off + 2 * width
    gb_off = ga_off + d
    tm_big = _largest_divisor(m, 1088, 16)

    h = _matmul(xb, w_in, layer, tm=tm_big, tn=1024, out_dtype=BF16, name="mm_in",
                stream_weights=True)

    zero_ret = jnp.zeros((1, bp, RET_HEADS, RET_DK, RET_DV), F32)
    za, ret_p = _retention(h, zero_ret, 0, row0=0, nseq=bp, seqlen=lp, pos0=0.0, hb=4,
                           za_into=jnp.zeros((m, RET_V), BF16), st_into=ret_bufs[0],
                           st_layer=layer, depth=depth)
    za, ret_s = _retention(h, state_ret, layer, row0=mp, nseq=bs, seqlen=ls,
                           pos0=float(PAST_LEN), hb=RET_HEADS,
                           za_into=za, st_into=ret_bufs[1], st_layer=layer, depth=depth)
    zb, _ = _spatial_gating(h, sgu_ln_g, sgu_ln_b, sgu_w, sgu_b, row0=0, nrows=mp, seqlen=lp,
                            su_off=su_off, width=width, emit_v=False,
                            zb_into=jnp.zeros((m, width), BF16))
    zb, sv_s = _spatial_gating(h, sgu_ln_g, sgu_ln_b, sgu_w, sgu_b, row0=mp, nrows=ms, seqlen=ls,
                               su_off=su_off, width=width, emit_v=True, zb_into=zb)

    merged = _merge(za, zb, h, w_a, w_b, layer, ga_off=ga_off, gb_off=gb_off, tm=tm_big, tn=512)
    t1 = _matmul(merged, w_o, layer, tm=tm_big, tn=512, out_dtype=F32, name="mm_o",
                 residual=x_res, res_scale=alpha)
    tm_ln = _largest_divisor(math.gcd(mp, ms), 512, 8)
    x1b, stats1 = _layer_norm(t1, ln1_g, ln1_b, row0=0, nrows=m, tm=tm_ln, emit_f32=False)

    z, tails = _up_gate(x1b, w_up, layer, conv_w, conv_b, conv_s0_s, bp=bp, lp=lp, bs=bs, ls=ls,
                        tm=tm_big, tn=_largest_divisor(dff, 256, LANES))
    t2 = _matmul(z, w_down, layer, tm=_largest_divisor(m, 544, 16), tn=512, out_dtype=F32,
                 name="mm_down", stream_weights=True,
                 residual=("normed", t1, stats1, ln1_g, ln1_b), res_scale=alpha)
    if last:
        (y_p,) = _layer_norm(t2, ln2_g, ln2_b, row0=0, nrows=mp, tm=tm_ln, emit_f32=True)
        (y_s,) = _layer_norm(t2, ln2_g, ln2_b, row0=mp, nrows=ms, tm=tm_ln, emit_f32=True)
        x2_res, x2b = (y_p, y_s), None
    else:
        x2b, stats2 = _layer_norm(t2, ln2_g, ln2_b, row0=0, nrows=m, tm=tm_ln, emit_f32=False)
        x2_res = ("normed", t2, stats2, ln2_g, ln2_b)

    conv_s = jnp.transpose(tails[:, :bs], (1, 0, 2))
    conv_p = jnp.transpose(tails[:, bs:], (1, 0, 2))
    return x2_res, x2b, (ret_p, ret_s), conv_p, conv_s, sv_s.reshape(bs, ls, width)


def kernel(x_prompt, x_sample, state_ret, state_conv, w_in, w_a, w_b, w_o, ln1_g, ln1_b,
           sgu_ln_g, sgu_ln_b, sgu_w, sgu_b, w_up, conv_w, conv_b, w_down, ln2_g, ln2_b):
    bp, lp, d = x_prompt.shape
    bs, ls, _ = x_sample.shape
    depth = w_in.shape[0]
    alpha = float((2 * depth) ** 0.25)
    xp, xs = x_prompt.reshape(bp * lp, d), x_sample.reshape(bs * ls, d)
    xb = _concat_cast(xp, xs, tm=_largest_divisor(math.gcd(bp * lp, bs * ls), 512, 16), dtype=BF16)
    x = ("split", xp, xs)
    big = (w_in, w_a, w_b, w_o, w_up, w_down)
    ret_bufs = tuple(jnp.zeros((depth, n, RET_HEADS, RET_DK, RET_DV), F32) for n in (bp, bs))
    conv_p, conv_s, sgu_v = [], [], []
    for l in range(depth):
        small = (ln1_g[l], ln1_b[l], sgu_ln_g[l], sgu_ln_b[l], sgu_w[l], sgu_b[l],
                 conv_w[l], conv_b[l], ln2_g[l], ln2_b[l])
        x, xb, ret_bufs, cp, cs, sv = _layer(
            l, depth, x, xb, state_ret, state_conv[l], ret_bufs, big, small,
            bp=bp, lp=lp, bs=bs, ls=ls, alpha=alpha, last=l == depth - 1)
        conv_p.append(cp)
        conv_s.append(cs)
        sgu_v.append(sv)
    y_p, y_s = x
    return (y_p.reshape(bp, lp, d), y_s.reshape(bs, ls, d), ret_bufs[0], jnp.stack(conv_p),
            ret_bufs[1], jnp.stack(conv_s), jnp.stack(sgu_v))
```

```python
import functools
import math

import jax
import jax.numpy as jnp
from jax import lax
from jax.experimental import pallas as pl
from jax.experimental.pallas import tpu as pltpu

F32 = jnp.float32
BF16 = jnp.bfloat16

RET_HEADS = 8
RET_DK = 256
RET_DV = 256
RET_QK = RET_HEADS * RET_DK
RET_V = RET_HEADS * RET_DV
SGU_GROUPS = 8
SGU_CHUNK = 128
CONV_W = 3
PAST_LEN = 1024
ROPE_BASE = 10000.0
LN_EPS = 1e-5
GN_EPS = 1e-6

V7X_VMEM_BYTES = 64 * 1024 * 1024
LANES = 128
RET_BLOCK = 256


def _vmem_limit(estimate_bytes):
    return int(min(V7X_VMEM_BYTES - (4 << 20), max(estimate_bytes * 5 // 4 + (2 << 20), 16 << 20)))


def _largest_divisor(n, limit, multiple):
    best = None
    d = multiple
    while d <= min(n, limit):
        if n % d == 0:
            best = d
        d += multiple
    assert best is not None, (n, limit, multiple)
    return best


def _mm_body(x_ref, w_ref, *rest, n_res, res_fn, res_scale, stream):
    o_ref, wbf_ref = rest[n_res], rest[n_res + 1]
    if stream is None:
        @pl.when(pl.program_id(1) == 0)
        def _():
            wbf_ref[...] = w_ref[...].astype(BF16)

        w_tile = wbf_ref[...]
    else:
        w_tile = _streamed_weight_tile(w_ref, wbf_ref, rest[n_res + 2], rest[n_res + 3], **stream)

    acc = jnp.dot(x_ref[...], w_tile, preferred_element_type=F32)
    if n_res:
        acc = res_scale * res_fn(*rest[:n_res]) + acc
    o_ref[...] = acc.astype(o_ref.dtype)


def _streamed_weight_tile(w_hbm, wbf_ref, stage_ref, sem_ref, *, layer, kc, tn, nj, ni):
    j = pl.program_id(0)
    i = pl.program_id(1)
    cur = j % 2

    def chunk_copy(tile, c, slot):
        return pltpu.make_async_copy(
            w_hbm.at[layer, pl.ds(c * kc, kc), pl.ds(tile * tn, tn)],
            stage_ref.at[slot], sem_ref.at[slot])

    def land(tile, c, slot, wslot):
        chunk_copy(tile, c, slot).wait()
        row0 = c * kc if isinstance(c, int) else pl.multiple_of(c * kc, kc)
        wbf_ref[wslot, pl.ds(row0, kc), :] = stage_ref[slot].astype(BF16)

    @pl.when((j == 0) & (i == 0))
    def _():
        chunk_copy(0, 0, 0).start()
        for c in range(ni):
            if c + 1 < ni:
                chunk_copy(0, c + 1, (c + 1) % 2).start()
            land(0, c, c % 2, 0)

    @pl.when((j > 0) & (i == 0))
    def _():
        land(j, ni - 1, (ni - 1) % 2, cur)

    @pl.when(j + 1 < nj)
    def _():
        chunk_copy(j + 1, i, i % 2).start()

        @pl.when(i > 0)
        def _():
            land(j + 1, i - 1, (i - 1) % 2, 1 - cur)

    return wbf_ref[cur]


def _res_normed(t_ref, stats_ref, g_ref, b_ref):
    reps = t_ref.shape[1] // LANES
    mu = jnp.concatenate([stats_ref[:, :LANES]] * reps, axis=1)
    rstd = jnp.concatenate([stats_ref[:, LANES:]] * reps, axis=1)
    return (t_ref[...] - mu) * rstd * g_ref[...] + b_ref[...]


def _res_split(p_ref, s_ref, *, tile, off, axis):
    p = p_ref[...]
    n = s_ref.shape[0]
    parts = [p[:off], s_ref[...], p[off + n:]]
    mixed = jnp.concatenate([q for q in parts if q.shape[0]], axis=0)
    return jnp.where(pl.program_id(axis) == tile, mixed, p)


def _cast_body(p_ref, s_ref, o_ref, *, tile, off):
    o_ref[...] = _res_split(p_ref, s_ref, tile=tile, off=off, axis=0).astype(o_ref.dtype)


def _concat_cast(p, sec, *, tm, dtype):
    mp, d = p.shape
    ms = sec.shape[0]
    tile, off = divmod(mp, tm)
    assert off + ms <= tm and (mp + ms) % tm == 0
    last_p = (mp - 1) // tm
    est = 2 * 2 * tm * d * 4 + 2 * tm * d * jnp.dtype(dtype).itemsize + 2 * tm * d * 4
    return pl.pallas_call(
        functools.partial(_cast_body, tile=tile, off=off),
        grid=((mp + ms) // tm,),
        in_specs=[pl.BlockSpec((tm, d), lambda i: (jnp.minimum(i, last_p), 0)),
                  pl.BlockSpec((ms, d), lambda i: (0, 0))],
        out_specs=pl.BlockSpec((tm, d), lambda i: (i, 0)),
        out_shape=jax.ShapeDtypeStruct((mp + ms, d), dtype),
        compiler_params=pltpu.CompilerParams(
            dimension_semantics=("arbitrary",), vmem_limit_bytes=_vmem_limit(est)),
        name="concat_cast",
    )(p, sec)


def _matmul(x, w, layer, *, tm, tn, out_dtype, name, stream_weights=False, residual=None,
            res_scale=None):
    m, k = x.shape
    n = w.shape[2]
    nj, ni = n // tn, m // tm
    est = 2 * tm * k * 2 + 2 * tm * tn * jnp.dtype(out_dtype).itemsize
    if stream_weights:
        kc = k // ni
        assert k == kc * ni and kc % 16 == 0 and ni % 2 == 0, (k, ni)
        stream = dict(layer=layer, kc=kc, tn=tn, nj=nj, ni=ni)
        w_spec = pl.BlockSpec(memory_space=pltpu.HBM)
        scratch = [pltpu.VMEM((2, k, tn), BF16), pltpu.VMEM((2, kc, tn), F32),
                   pltpu.SemaphoreType.DMA((2,))]
        est += 2 * k * tn * 2 + 2 * kc * tn * 4 + 2 * kc * tn * 4
    else:
        stream = None
        w_spec = pl.BlockSpec((None, k, tn), lambda j, i: (layer, 0, j))
        scratch = [pltpu.VMEM((k, tn), BF16)]
        est += 2 * k * tn * 4 + k * tn * 2
    in_specs = [pl.BlockSpec((tm, k), lambda j, i: (i, 0)), w_spec]
    args = [x, w]
    res_fn = None
    tile_mn = pl.BlockSpec((tm, tn), lambda j, i: (i, j))
    if residual is not None:
        kind = residual[0]
        est += 2 * tm * tn * 4
        if kind == "normed":
            _, t, stats, g, b = residual
            res_fn = _res_normed
            vec = pl.BlockSpec((1, tn), lambda j, i: (0, j))
            in_specs += [tile_mn, pl.BlockSpec((tm, 2 * LANES), lambda j, i: (i, 0)), vec, vec]
            args += [t, stats, g.reshape(1, n), b.reshape(1, n)]
            est += 2 * tm * 2 * LANES * 4
        else:
            _, p, sec = residual
            mp, ms = p.shape[0], sec.shape[0]
            tile, off = divmod(mp, tm)
            assert kind == "split" and off + ms <= tm and mp + ms == m
            last_p = (mp - 1) // tm
            res_fn = functools.partial(_res_split, tile=tile, off=off, axis=1)
            in_specs += [pl.BlockSpec((tm, tn), lambda j, i: (jnp.minimum(i, last_p), j)),
                         pl.BlockSpec((ms, tn), lambda j, i: (0, j))]
            args += [p, sec]
            est += 2 * ms * tn * 4
    return pl.pallas_call(
        functools.partial(_mm_body, n_res=len(args) - 2, res_fn=res_fn, res_scale=res_scale,
                          stream=stream),
        grid=(nj, ni),
        in_specs=in_specs,
        out_specs=tile_mn,
        out_shape=jax.ShapeDtypeStruct((m, n), out_dtype),
        scratch_shapes=scratch,
        compiler_params=pltpu.CompilerParams(
            dimension_semantics=("arbitrary", "arbitrary"), vmem_limit_bytes=_vmem_limit(est)),
        name=name,
    )(*args)


def _merge_body(za_ref, zb_ref, ga_ref, gb_ref, wa_ref, wb_ref, o_ref, wabf_ref, wbbf_ref):
    @pl.when(pl.program_id(1) == 0)
    def _():
        wabf_ref[...] = wa_ref[...].astype(BF16)
        wbbf_ref[...] = wb_ref[...].astype(BF16)

    a = jnp.dot(za_ref[...], wabf_ref[...], preferred_element_type=F32)
    b = jnp.dot(zb_ref[...], wbbf_ref[...], preferred_element_type=F32)
    ga = jax.nn.sigmoid(ga_ref[...].astype(F32))
    gb = jax.nn.sigmoid(gb_ref[...].astype(F32))
    o_ref[...] = (ga * a + gb * b).astype(o_ref.dtype)


def _merge(za, zb, h, w_a, w_b, layer, *, ga_off, gb_off, tm, tn):
    m, ka = za.shape
    kb = zb.shape[1]
    n = w_a.shape[2]
    ga_blk, gb_blk = ga_off // tn, gb_off // tn
    est = (2 * tm * (ka + kb) * 2 + 2 * (ka + kb) * tn * 4 + (ka + kb) * tn * 2
           + 2 * 2 * tm * tn * h.dtype.itemsize + 2 * tm * tn * 2 + 4 * tm * tn * 4)
    return pl.pallas_call(
        _merge_body,
        grid=(n // tn, m // tm),
        in_specs=[pl.BlockSpec((tm, ka), lambda j, i: (i, 0)),
                  pl.BlockSpec((tm, kb), lambda j, i: (i, 0)),
                  pl.BlockSpec((tm, tn), lambda j, i: (i, ga_blk + j)),
                  pl.BlockSpec((tm, tn), lambda j, i: (i, gb_blk + j)),
                  pl.BlockSpec((None, ka, tn), lambda j, i: (layer, 0, j)),
                  pl.BlockSpec((None, kb, tn), lambda j, i: (layer, 0, j))],
        out_specs=pl.BlockSpec((tm, tn), lambda j, i: (i, j)),
        out_shape=jax.ShapeDtypeStruct((m, n), BF16),
        scratch_shapes=[pltpu.VMEM((ka, tn), BF16), pltpu.VMEM((kb, tn), BF16)],
        compiler_params=pltpu.CompilerParams(
            dimension_semantics=("arbitrary", "arbitrary"), vmem_limit_bytes=_vmem_limit(est)),
        name="merge_mm",
    )(za, zb, h, h, w_a, w_b)


def _ln_body(t_ref, g_ref, b_ref, *out_refs, emit_f32):
    t = t_ref[...]
    mu = jnp.mean(t, axis=-1, keepdims=True)
    d = t - mu
    var = jnp.mean(d * d, axis=-1, keepdims=True)
    rstd = lax.rsqrt(var + LN_EPS)
    out = d * rstd * g_ref[...] + b_ref[...]
    if emit_f32:
        out_refs[0][...] = out
    else:
        obf_ref, stats_ref = out_refs
        obf_ref[...] = out.astype(BF16)
        rows = t.shape[0]
        stats_ref[:, :LANES] = jnp.broadcast_to(mu, (rows, LANES))
        stats_ref[:, LANES:] = jnp.broadcast_to(rstd, (rows, LANES))


def _layer_norm(t, g, b, *, row0, nrows, tm, emit_f32):
    d = t.shape[1]
    rb0 = row0 // tm
    assert row0 % tm == 0 and nrows % tm == 0
    est = 2 * tm * d * 4 + 2 * tm * d * 4 + 6 * tm * d * 4
    row = pl.BlockSpec((tm, d), lambda i: (i, 0))
    vec = pl.BlockSpec((1, d), lambda i: (0, 0))
    if emit_f32:
        out_specs = [row]
        out_shape = [jax.ShapeDtypeStruct((nrows, d), F32)]
    else:
        out_specs = [row, pl.BlockSpec((tm, 2 * LANES), lambda i: (i, 0))]
        out_shape = [jax.ShapeDtypeStruct((nrows, d), BF16),
                     jax.ShapeDtypeStruct((nrows, 2 * LANES), F32)]
    return pl.pallas_call(
        functools.partial(_ln_body, emit_f32=emit_f32),
        grid=(nrows // tm,),
        in_specs=[pl.BlockSpec((tm, d), lambda i: (rb0 + i, 0)), vec, vec],
        out_specs=out_specs,
        out_shape=out_shape,
        compiler_params=pltpu.CompilerParams(
            dimension_semantics=("arbitrary",), vmem_limit_bytes=_vmem_limit(est)),
        name="layer_norm",
    )(t, g.reshape(1, d), b.reshape(1, d))


def _ret_body(q_ref, k_ref, v_ref, g_ref, cos_ref, sin_ref, lg_ref, s0_ref, za_ref, sout_ref,
              s_ref, d_ref, qd_ref, kd_ref, *, c, hb):
    b = pl.program_id(1)
    ci = pl.program_id(2)

    @pl.when((b == 0) & (ci == 0))
    def _():
        ri = lax.broadcasted_iota(jnp.int32, (c, c), 0)
        cj = lax.broadcasted_iota(jnp.int32, (c, c), 1)
        diff = (ri - cj).astype(F32)
        r = lax.broadcasted_iota(jnp.int32, (c, RET_DK), 0).astype(F32)
        for hh in range(hb):
            lg = lg_ref[hh]
            d_ref[hh] = jnp.where(diff >= 0, jnp.exp(jnp.maximum(diff, 0.0) * lg[:, :c]), 0.0)
            qd_ref[hh] = jnp.exp((r + 1.0) * lg)
            kd_ref[hh] = jnp.exp((c - 1.0 - r) * lg)

    @pl.when(ci == 0)
    def _():
        s_ref[...] = s0_ref[...]

    cos = cos_ref[...]
    sin = sin_ref[...]
    half = RET_DK // 2

    def rope(t):
        t1, t2 = t[:, :half], t[:, half:]
        return jnp.concatenate([t1 * cos - t2 * sin, t1 * sin + t2 * cos], axis=-1)

    for hh in range(hb):
        qc = slice(hh * RET_DK, (hh + 1) * RET_DK)
        vc = slice(hh * RET_DV, (hh + 1) * RET_DV)
        q = rope(q_ref[:, qc].astype(F32))
        k = rope(k_ref[:, qc].astype(F32)) * (RET_DK ** -0.5)
        vb = v_ref[:, vc].astype(BF16)
        s = s_ref[hh]
        scores = lax.dot_general(q.astype(BF16), k.astype(BF16), (((1,), (1,)), ((), ())),
                                 preferred_element_type=F32) * d_ref[hh]
        o = (jnp.dot(scores.astype(BF16), vb, preferred_element_type=F32)
             + jnp.dot((q * qd_ref[hh]).astype(BF16), s.astype(BF16), preferred_element_type=F32))
        kv = lax.dot_general((k * kd_ref[hh]).astype(BF16), vb, (((0,), (0,)), ((), ())),
                             preferred_element_type=F32)
        s_ref[hh] = s * jnp.exp(float(c) * lg_ref[hh]) + kv

        mu = jnp.mean(o, axis=-1, keepdims=True)
        od = o - mu
        var = jnp.mean(od * od, axis=-1, keepdims=True)
        on = od * lax.rsqrt(var + GN_EPS)
        za_ref[:, vc] = (jax.nn.silu(g_ref[:, vc].astype(F32)) * on).astype(BF16)

    @pl.when(ci == pl.num_programs(2) - 1)
    def _():
        sout_ref[...] = s_ref[...]


def _drop_leading(n, body):
    def wrapped(*refs):
        return body(*refs[n:])
    return wrapped


def _retention(h, s0, s0_layer, *, row0, nseq, seqlen, pos0, hb, za_into, st_into, st_layer,
               depth):
    m = h.shape[0]
    c = min(RET_BLOCK, seqlen)
    nc = seqlen // c
    rb0 = row0 // c
    hq = RET_HEADS // hb
    half = RET_DK // 2
    pos = pos0 + jnp.arange(seqlen, dtype=F32)
    freqs = ROPE_BASE ** (-jnp.arange(half, dtype=F32) / half)
    ang = pos[:, None] * freqs[None, :]
    cos, sin = jnp.cos(ang), jnp.sin(ang)
    log_gamma = jnp.log(1.0 - 2.0 ** (-5.0 - jnp.arange(RET_HEADS, dtype=F32)))
    lg = jnp.broadcast_to(log_gamma[:, None, None], (RET_HEADS, 1, RET_DK))

    def hcol(off):
        return pl.BlockSpec((c, hb * RET_DK), lambda hh, b, ci: (rb0 + b * nc + ci, off + hh))

    tab = pl.BlockSpec((c, half), lambda hh, b, ci: (ci, 0))

    def state(layer):
        return pl.BlockSpec((None, None, hb, RET_DK, RET_DV),
                            lambda hh, b, ci: (layer, b, hh, 0, 0))

    donated = [za_into, st_into]
    assert za_into.shape == (m, RET_V) and st_into.shape == (depth, nseq, RET_HEADS, RET_DK, RET_DV)
    est = (2 * 4 * c * hb * RET_DK * 4 + 2 * 2 * hb * RET_DK * RET_DV * 4 + hb * RET_DK * RET_DV * 4
           + hb * (c * c + 2 * c * RET_DK) * 4 + 2 * c * hb * RET_DV * 2 + 16 * c * RET_DK * 4)
    return pl.pallas_call(
        _drop_leading(len(donated), functools.partial(_ret_body, c=c, hb=hb)),
        grid=(hq, nseq, nc),
        in_specs=[pl.BlockSpec(memory_space=pltpu.HBM)] * len(donated)
        + [hcol(0), hcol(hq), hcol(2 * hq), hcol(3 * hq), tab, tab,
           pl.BlockSpec((hb, 1, RET_DK), lambda hh, b, ci: (hh, 0, 0)), state(s0_layer)],
        out_specs=[pl.BlockSpec((c, hb * RET_DV), lambda hh, b, ci: (rb0 + b * nc + ci, hh)),
                   state(st_layer)],
        out_shape=[jax.ShapeDtypeStruct((m, RET_V), BF16),
                   jax.ShapeDtypeStruct((depth, nseq, RET_HEADS, RET_DK, RET_DV), F32)],
        input_output_aliases={0: 0, 1: 1},
        scratch_shapes=[pltpu.VMEM((hb, RET_DK, RET_DV), F32), pltpu.VMEM((hb, c, c), F32),
                        pltpu.VMEM((hb, c, RET_DK), F32), pltpu.VMEM((hb, c, RET_DK), F32)],
        compiler_params=pltpu.CompilerParams(
            dimension_semantics=("arbitrary", "arbitrary", "arbitrary"),
            vmem_limit_bytes=_vmem_limit(est)),
        name="retention",
    )(*donated, h, h, h, h, cos, sin, lg, s0)


def _sgu_body(su_ref, sv_ref, lng_ref, lnb_ref, w_ref, bias_ref, zb_ref, *rest, nvalid, **kw):
    if nvalid is None:
        _sgu_tile(su_ref, sv_ref, lng_ref, lnb_ref, w_ref, bias_ref, zb_ref, *rest, **kw)
        return

    @pl.when(pl.program_id(0) < nvalid)
    def _():
        _sgu_tile(su_ref, sv_ref, lng_ref, lnb_ref, w_ref, bias_ref, zb_ref, *rest, **kw)

    @pl.when(pl.program_id(0) >= nvalid)
    def _():
        zb_ref[...] = jnp.zeros(zb_ref.shape, zb_ref.dtype)


def _sgu_tile(su_ref, sv_ref, lng_ref, lnb_ref, w_ref, bias_ref, zb_ref, *rest, c, tm, gdim, emit_v):
    sv = jax.nn.gelu(sv_ref[...].astype(F32))
    mu = jnp.mean(sv, axis=-1, keepdims=True)
    d = sv - mu
    var = jnp.mean(d * d, axis=-1, keepdims=True)
    svn = d * lax.rsqrt(var + LN_EPS) * lng_ref[...] + lnb_ref[...]
    if emit_v:
        rest[0][...] = svn
    svb = svn.astype(BF16)
    ri = lax.broadcasted_iota(jnp.int32, (SGU_CHUNK, SGU_CHUNK), 0)
    cj = lax.broadcasted_iota(jnp.int32, (SGU_CHUNK, SGU_CHUNK), 1)
    mask = (ri >= cj) & ((ri // c) == (cj // c))
    for g in range(SGU_GROUPS):
        wm = jnp.where(mask, w_ref[g], 0.0).astype(BF16)
        cols = slice(g * gdim, (g + 1) * gdim)
        for r in range(tm // SGU_CHUNK):
            rows = slice(r * SGU_CHUNK, (r + 1) * SGU_CHUNK)
            mixed = jnp.dot(wm, svb[rows, cols], preferred_element_type=F32) + bias_ref[:, cols]
            zb_ref[rows, cols] = (jax.nn.gelu(su_ref[rows, cols].astype(F32)) * mixed).astype(BF16)


def _spatial_gating(h, ln_g, ln_b, w_s, b_s, *, row0, nrows, seqlen, su_off, width, emit_v,
                    zb_into):
    m = h.shape[0]
    c = min(SGU_CHUNK, seqlen)
    rep = SGU_CHUNK // c
    gdim = width // SGU_GROUPS
    w_blk = jnp.tile(w_s[:, :c, :c], (1, rep, rep))
    bias = jnp.repeat(jnp.tile(b_s[:, :c], (1, rep)).T, gdim, axis=1)
    tm = _largest_divisor(math.gcd(nrows, m), 256, SGU_CHUNK)
    rb0 = row0 // tm
    su_blk = su_off // width
    assert row0 % tm == 0
    nvalid = nrows // tm
    if zb_into is None:
        assert row0 == 0 and not emit_v
        donated, steps, fill = [], m // tm, nvalid
    else:
        assert zb_into.shape == (m, width)
        donated, steps, fill = [zb_into], nvalid, None
    last = rb0 + nvalid - 1
    vec = pl.BlockSpec((1, width), lambda i: (0, 0))
    out_specs = [pl.BlockSpec((tm, width), lambda i: (rb0 + i, 0))]
    out_shape = [jax.ShapeDtypeStruct((m, width), BF16)]
    if emit_v:
        out_specs.append(pl.BlockSpec((tm, width), lambda i: (i, 0)))
        out_shape.append(jax.ShapeDtypeStruct((nrows, width), F32))
    est = 2 * 2 * tm * width * 4 + 2 * tm * width * (2 + 4) + 8 * tm * width * 4
    res = pl.pallas_call(
        _drop_leading(len(donated), functools.partial(_sgu_body, nvalid=fill, c=c, tm=tm,
                                                      gdim=gdim, emit_v=emit_v)),
        grid=(steps,),
        in_specs=[pl.BlockSpec(memory_space=pltpu.HBM)] * len(donated)
        + [pl.BlockSpec((tm, width), lambda i: (jnp.minimum(rb0 + i, last), su_blk)),
           pl.BlockSpec((tm, width), lambda i: (jnp.minimum(rb0 + i, last), su_blk + 1)),
           vec, vec,
           pl.BlockSpec((SGU_GROUPS, SGU_CHUNK, SGU_CHUNK), lambda i: (0, 0, 0)),
           pl.BlockSpec((SGU_CHUNK, width), lambda i: (0, 0))],
        out_specs=out_specs,
        out_shape=out_shape,
        input_output_aliases={0: 0} if donated else {},
        compiler_params=pltpu.CompilerParams(
            dimension_semantics=("arbitrary",), vmem_limit_bytes=_vmem_limit(est)),
        name="spatial_gating",
    )(*donated, h, h, ln_g.reshape(1, width), ln_b.reshape(1, width), w_blk, bias)
    return res if emit_v else (res[0], None)


def _rows(ref, slab, start, n, stride):
    if n == 1:
        return ref[slab, start:start + 1, :]
    return ref[slab, pl.ds(start, n, stride=stride), :]


def _set_rows(ref, slab, start, n, stride, value):
    if n == 1:
        ref[slab, start:start + 1, :] = value
    else:
        ref[slab, pl.ds(start, n, stride=stride), :] = value


def _upgate_body(x_ref, wg_ref, wv_ref, cw_ref, cb_ref, st_ref, z_ref, tail_ref,
                 wbf_ref, buf_ref, conv_ref, val_ref, *, tm, tn, starts, ends):
    i = pl.program_id(1)
    slabs = tn // LANES

    @pl.when(i == 0)
    def _():
        wbf_ref[:, :tn] = wg_ref[...].astype(BF16)
        wbf_ref[:, tn:] = wv_ref[...].astype(BF16)
        buf_ref[:, 0:8, :] = jnp.zeros((slabs, 8, LANES), F32)

    @pl.when(i > 0)
    def _():
        buf_ref[:, 0:8, :] = buf_ref[:, tm:tm + 8, :]

    gv = jnp.dot(x_ref[...], wbf_ref[...], preferred_element_type=F32)
    val_ref[...] = gv[:, tn:]
    for s in range(slabs):
        lanes = slice(s * LANES, (s + 1) * LANES)
        buf_ref[s, 8:tm + 8, :] = gv[:, lanes]
        conv_ref[s] = (cb_ref[:, lanes] + buf_ref[s, 6:tm + 6, :] * cw_ref[0:1, lanes]
                       + buf_ref[s, 7:tm + 7, :] * cw_ref[1:2, lanes]
                       + gv[:, lanes] * cw_ref[2:3, lanes])

    for tile, off, n, stride, slot in starts:
        @pl.when(i == tile)
        def _(off=off, n=n, stride=stride, slot=slot):
            for s in range(slabs):
                lanes = slice(s * LANES, (s + 1) * LANES)
                w0, w1, w2 = cw_ref[0:1, lanes], cw_ref[1:2, lanes], cw_ref[2:3, lanes]
                cb = cb_ref[:, lanes]
                p0 = _rows(buf_ref, s, 8 + off, n, stride)
                p1 = _rows(buf_ref, s, 8 + off + 1, n, stride)
                if slot is None:
                    c0 = cb + p0 * w2
                    c1 = cb + p0 * w1 + p1 * w2
                else:
                    s0 = st_ref[0, slot:slot + n, lanes]
                    s1 = st_ref[1, slot:slot + n, lanes]
                    c0 = cb + s0 * w0 + s1 * w1 + p0 * w2
                    c1 = cb + s1 * w0 + p0 * w1 + p1 * w2
                _set_rows(conv_ref, s, off, n, stride, c0)
                _set_rows(conv_ref, s, off + 1, n, stride, c1)

    for tile, off, n, stride, slot in ends:
        @pl.when(i == tile)
        def _(off=off, n=n, stride=stride, slot=slot):
            for s in range(slabs):
                lanes = slice(s * LANES, (s + 1) * LANES)
                tail_ref[0, slot:slot + n, lanes] = _rows(buf_ref, s, 8 + off - 1, n, stride)
                tail_ref[1, slot:slot + n, lanes] = _rows(buf_ref, s, 8 + off, n, stride)

    for s in range(slabs):
        lanes = slice(s * LANES, (s + 1) * LANES)
        z_ref[:, lanes] = (jax.nn.gelu(conv_ref[s]) * val_ref[:, lanes]).astype(BF16)


def _seq_groups(seq_rows, seqlen, tm, within):
    groups = []
    for idx, r0 in enumerate(seq_rows):
        row = r0 + within
        tile, off = divmod(row, tm)
        if groups and groups[-1][0] == tile and seq_rows[idx - 1] + seqlen == r0:
            t, o, n, st, first = groups[-1]
            groups[-1] = (t, o, n + 1, st, first)
        else:
            groups.append((tile, off, 1, seqlen, idx))
    return groups


def _up_gate(xb, w_up, layer, conv_w, conv_b, conv_s0_s, *, bp, lp, bs, ls, tm, tn):
    m, k = xb.shape
    dff = w_up.shape[2] // 2
    vblk = dff // tn
    nseq = bp + bs
    prompt_rows = [b * lp for b in range(bp)]
    sample_rows = [bp * lp + b * ls for b in range(bs)]
    starts, ends = [], []
    for rows, sl, has_state, slot0 in ((sample_rows, ls, True, 0), (prompt_rows, lp, False, bs)):
        for t, o, n, st, first in _seq_groups(rows, sl, tm, 0):
            assert o + (n - 1) * st + 1 < tm, "a sequence's first two rows must share a row tile"
            starts.append((t, o, n, st, first if has_state else None))
        for t, o, n, st, first in _seq_groups(rows, sl, tm, sl - 1):
            assert o >= 1, "a sequence's last two rows must share a row tile"
            ends.append((t, o, n, st, slot0 + first))
    st_in = jnp.transpose(conv_s0_s, (1, 0, 2))
    est = (2 * tm * k * 2 + 2 * 2 * k * tn * 4 + 2 * k * tn * 2 + 2 * (tm + 8) * tn * 4
           + 2 * tm * tn * 2 + 2 * 2 * (bs + nseq) * tn * 4 + 6 * tm * tn * 4)
    return pl.pallas_call(
        functools.partial(_upgate_body, tm=tm, tn=tn, starts=tuple(starts), ends=tuple(ends)),
        grid=(dff // tn, m // tm),
        in_specs=[pl.BlockSpec((tm, k), lambda j, i: (i, 0)),
                  pl.BlockSpec((None, k, tn), lambda j, i: (layer, 0, j)),
                  pl.BlockSpec((None, k, tn), lambda j, i: (layer, 0, vblk + j)),
                  pl.BlockSpec((CONV_W, tn), lambda j, i: (0, j)),
                  pl.BlockSpec((1, tn), lambda j, i: (0, j)),
                  pl.BlockSpec((CONV_W - 1, bs, tn), lambda j, i: (0, 0, j))],
        out_specs=[pl.BlockSpec((tm, tn), lambda j, i: (i, j)),
                   pl.BlockSpec((CONV_W - 1, nseq, tn), lambda j, i: (0, 0, j))],
        out_shape=[jax.ShapeDtypeStruct((m, dff), BF16),
                   jax.ShapeDtypeStruct((CONV_W - 1, nseq, dff), F32)],
        scratch_shapes=[pltpu.VMEM((k, 2 * tn), BF16),
                        pltpu.VMEM((tn // LANES, tm + 8, LANES), F32),
                        pltpu.VMEM((tn // LANES, tm, LANES), F32), pltpu.VMEM((tm, tn), F32)],
        compiler_params=pltpu.CompilerParams(
            dimension_semantics=("arbitrary", "arbitrary"), vmem_limit_bytes=_vmem_limit(est)),
        name="up_gate",
    )(xb, w_up, w_up, conv_w, conv_b.reshape(1, dff), st_in)


def _layer(layer, depth, x_res, xb, state_ret, conv_s0_s, ret_bufs, big, small, *, bp, lp, bs,
           ls, alpha, last):
    w_in, w_a, w_b, w_o, w_up, w_down = big
    ln1_g, ln1_b, sgu_ln_g, sgu_ln_b, sgu_w, sgu_b, conv_w, conv_b, ln2_g, ln2_b = small
    m, d = xb.shape
    mp = bp * lp
    ms = bs * ls
    width = d // 2
    dff = w_down.shape[1]
    su_off = 2 * RET_QK + 2 * RET_V
    ga_off = su_off + 2 * width
    gb_off = ga_off + d
    tm_big = _largest_divisor(m, 1088, 16)

    h = _matmul(xb, w_in, layer, tm=tm_big, tn=1024, out_dtype=BF16, name="mm_in",
                stream_weights=True)

    zero_ret = jnp.zeros((1, bp, RET_HEADS, RET_DK, RET_DV), F32)
    za, ret_p = _retention(h, zero_ret, 0, row0=0, nseq=bp, seqlen=lp, pos0=0.0, hb=4,
                           za_into=jnp.zeros((m, RET_V), BF16), st_into=ret_bufs[0],
                           st_layer=layer, depth=depth)
    za, ret_s = _retention(h, state_ret, layer, row0=mp, nseq=bs, seqlen=ls,
                           pos0=float(PAST_LEN), hb=RET_HEADS,
                           za_into=za, st_into=ret_bufs[1], st_layer=layer, depth=depth)
    zb, _ = _spatial_gating(h, sgu_ln_g, sgu_ln_b, sgu_w, sgu_b, row0=0, nrows=mp, seqlen=lp,
                            su_off=su_off, width=width, emit_v=False, zb_into=None)
    zb, sv_s = _spatial_gating(h, sgu_ln_g, sgu_ln_b, sgu_w, sgu_b, row0=mp, nrows=ms, seqlen=ls,
                               su_off=su_off, width=width, emit_v=True, zb_into=zb)

    merged = _merge(za, zb, h, w_a, w_b, layer, ga_off=ga_off, gb_off=gb_off, tm=tm_big, tn=512)
    t1 = _matmul(merged, w_o, layer, tm=tm_big, tn=512, out_dtype=F32, name="mm_o",
                 residual=x_res, res_scale=alpha)
    tm_ln = _largest_divisor(math.gcd(mp, ms), 512, 8)
    x1b, stats1 = _layer_norm(t1, ln1_g, ln1_b, row0=0, nrows=m, tm=tm_ln, emit_f32=False)

    z, tails = _up_gate(x1b, w_up, layer, conv_w, conv_b, conv_s0_s, bp=bp, lp=lp, bs=bs, ls=ls,
                        tm=tm_big, tn=_largest_divisor(dff, 256, LANES))
    t2 = _matmul(z, w_down, layer, tm=_largest_divisor(m, 544, 16), tn=512, out_dtype=F32,
                 name="mm_down", stream_weights=True,
                 residual=("normed", t1, stats1, ln1_g, ln1_b), res_scale=alpha)
    if last:
        (y_p,) = _layer_norm(t2, ln2_g, ln2_b, row0=0, nrows=mp, tm=tm_ln, emit_f32=True)
        (y_s,) = _layer_norm(t2, ln2_g, ln2_b, row0=mp, nrows=ms, tm=tm_ln, emit_f32=True)
        x2_res, x2b = (y_p, y_s), None
    else:
        x2b, stats2 = _layer_norm(t2, ln2_g, ln2_b, row0=0, nrows=m, tm=tm_ln, emit_f32=False)
        x2_res = ("normed", t2, stats2, ln2_g, ln2_b)

    conv_s = jnp.transpose(tails[:, :bs], (1, 0, 2))
    conv_p = jnp.transpose(tails[:, bs:], (1, 0, 2))
    return x2_res, x2b, (ret_p, ret_s), conv_p, conv_s, sv_s.reshape(bs, ls, width)


def kernel(x_prompt, x_sample, state_ret, state_conv, w_in, w_a, w_b, w_o, ln1_g, ln1_b,
           sgu_ln_g, sgu_ln_b, sgu_w, sgu_b, w_up, conv_w, conv_b, w_down, ln2_g, ln2_b):
    bp, lp, d = x_prompt.shape
    bs, ls, _ = x_sample.shape
    depth = w_in.shape[0]
    alpha = float((2 * depth) ** 0.25)
    xp, xs = x_prompt.reshape(bp * lp, d), x_sample.reshape(bs * ls, d)
    xb = _concat_cast(xp, xs, tm=_largest_divisor(math.gcd(bp * lp, bs * ls), 512, 16), dtype=BF16)
    x = ("split", xp, xs)
    big = (w_in, w_a, w_b, w_o, w_up, w_down)
    ret_bufs = tuple(jnp.zeros((depth, n, RET_HEADS, RET_DK, RET_DV), F32) for n in (bp, bs))
    conv_p, conv_s, sgu_v = [], [], []
    for l in range(depth):
        small = (ln1_g[l], ln1_b[l], sgu_ln_g[l], sgu_ln_b[l], sgu_w[l], sgu_b[l],
                 conv_w[l], conv_b[l], ln2_g[l], ln2_b[l])
        x, xb, ret_bufs, cp, cs, sv = _layer(
            l, depth, x, xb, state_ret, state_conv[l], ret_bufs, big, small,
            bp=bp, lp=lp, bs=bs, ls=ls, alpha=alpha, last=l == depth - 1)
        conv_p.append(cp)
        conv_s.append(cs)
        sgu_v.append(sv)
    y_p, y_s = x
    return (y_p.reshape(bp, lp, d), y_s.reshape(bs, ls, d), ret_bufs[0], jnp.stack(conv_p),
            ret_bufs[1], jnp.stack(conv_s), jnp.stack(sgu_v))
```

```python
import functools
import math

import jax
import jax.numpy as jnp
from jax import lax
from jax.experimental import pallas as pl
from jax.experimental.pallas import tpu as pltpu

F32 = jnp.float32
BF16 = jnp.bfloat16

RET_HEADS = 8
RET_DK = 256
RET_DV = 256
RET_QK = RET_HEADS * RET_DK
RET_V = RET_HEADS * RET_DV
SGU_GROUPS = 8
SGU_CHUNK = 128
CONV_W = 3
PAST_LEN = 1024
ROPE_BASE = 10000.0
LN_EPS = 1e-5
GN_EPS = 1e-6

V7X_VMEM_BYTES = 64 * 1024 * 1024
LANES = 128
RET_BLOCK = 256


def _vmem_limit(estimate_bytes):
    return int(min(V7X_VMEM_BYTES - (4 << 20), max(estimate_bytes * 5 // 4 + (2 << 20), 16 << 20)))


def _largest_divisor(n, limit, multiple):
    best = None
    d = multiple
    while d <= min(n, limit):
        if n % d == 0:
            best = d
        d += multiple
    assert best is not None, (n, limit, multiple)
    return best


def _mm_body(x_ref, w_ref, *rest, n_res, res_fn, res_scale, stream):
    o_ref, wbf_ref = rest[n_res], rest[n_res + 1]
    if stream is None:
        @pl.when(pl.program_id(1) == 0)
        def _():
            wbf_ref[...] = w_ref[...].astype(BF16)

        w_tile = wbf_ref[...]
    else:
        w_tile = _streamed_weight_tile(w_ref, wbf_ref, rest[n_res + 2], rest[n_res + 3], **stream)

    acc = jnp.dot(x_ref[...], w_tile, preferred_element_type=F32)
    if n_res:
        acc = res_scale * res_fn(*rest[:n_res]) + acc
    o_ref[...] = acc.astype(o_ref.dtype)


def _streamed_weight_tile(w_hbm, wbf_ref, stage_ref, sem_ref, *, layer, kc, tn, nj, ni,
                          col_offsets=(0,)):
    j = pl.program_id(0)
    i = pl.program_id(1)
    cur = j % 2

    def chunk_copies(tile, c, slot):
        return [pltpu.make_async_copy(
            w_hbm.at[layer, pl.ds(c * kc, kc), pl.ds(off + tile * tn, tn)],
            stage_ref.at[slot, w], sem_ref.at[slot, w]) for w, off in enumerate(col_offsets)]

    def start(tile, c, slot):
        for cp in chunk_copies(tile, c, slot):
            cp.start()

    def land(tile, c, slot, wslot):
        row0 = c * kc if isinstance(c, int) else pl.multiple_of(c * kc, kc)
        for w, cp in enumerate(chunk_copies(tile, c, slot)):
            cp.wait()
            wbf_ref[wslot, pl.ds(row0, kc), w * tn:(w + 1) * tn] = stage_ref[slot, w].astype(BF16)

    @pl.when((j == 0) & (i == 0))
    def _():
        start(0, 0, 0)
        for c in range(ni):
            if c + 1 < ni:
                start(0, c + 1, (c + 1) % 2)
            land(0, c, c % 2, 0)

    @pl.when((j > 0) & (i == 0))
    def _():
        land(j, ni - 1, (ni - 1) % 2, cur)

    @pl.when(j + 1 < nj)
    def _():
        start(j + 1, i, i % 2)

        @pl.when(i > 0)
        def _():
            land(j + 1, i - 1, (i - 1) % 2, 1 - cur)

    return wbf_ref[cur]


def _res_normed(t_ref, stats_ref, g_ref, b_ref):
    reps = t_ref.shape[1] // LANES
    mu = jnp.concatenate([stats_ref[:, :LANES]] * reps, axis=1)
    rstd = jnp.concatenate([stats_ref[:, LANES:]] * reps, axis=1)
    return (t_ref[...] - mu) * rstd * g_ref[...] + b_ref[...]


def _res_split(p_ref, s_ref, *, tile, off, axis):
    p = p_ref[...]
    n = s_ref.shape[0]
    parts = [p[:off], s_ref[...], p[off + n:]]
    mixed = jnp.concatenate([q for q in parts if q.shape[0]], axis=0)
    return jnp.where(pl.program_id(axis) == tile, mixed, p)


def _cast_body(p_ref, s_ref, o_ref, *, tile, off):
    o_ref[...] = _res_split(p_ref, s_ref, tile=tile, off=off, axis=0).astype(o_ref.dtype)


def _concat_cast(p, sec, *, tm, dtype):
    mp, d = p.shape
    ms = sec.shape[0]
    tile, off = divmod(mp, tm)
    assert off + ms <= tm and (mp + ms) % tm == 0
    last_p = (mp - 1) // tm
    est = 2 * 2 * tm * d * 4 + 2 * tm * d * jnp.dtype(dtype).itemsize + 2 * tm * d * 4
    return pl.pallas_call(
        functools.partial(_cast_body, tile=tile, off=off),
        grid=((mp + ms) // tm,),
        in_specs=[pl.BlockSpec((tm, d), lambda i: (jnp.minimum(i, last_p), 0)),
                  pl.BlockSpec((ms, d), lambda i: (0, 0))],
        out_specs=pl.BlockSpec((tm, d), lambda i: (i, 0)),
        out_shape=jax.ShapeDtypeStruct((mp + ms, d), dtype),
        compiler_params=pltpu.CompilerParams(
            dimension_semantics=("arbitrary",), vmem_limit_bytes=_vmem_limit(est)),
        name="concat_cast",
    )(p, sec)


def _matmul(x, w, layer, *, tm, tn, out_dtype, name, stream_weights=False, residual=None,
            res_scale=None):
    m, k = x.shape
    n = w.shape[2]
    nj, ni = n // tn, m // tm
    est = 2 * tm * k * 2 + 2 * tm * tn * jnp.dtype(out_dtype).itemsize
    if stream_weights:
        kc = k // ni
        assert k == kc * ni and kc % 16 == 0 and ni % 2 == 0, (k, ni)
        stream = dict(layer=layer, kc=kc, tn=tn, nj=nj, ni=ni)
        w_spec = pl.BlockSpec(memory_space=pltpu.HBM)
        scratch = [pltpu.VMEM((2, k, tn), BF16), pltpu.VMEM((2, 1, kc, tn), F32),
                   pltpu.SemaphoreType.DMA((2, 1))]
        est += 2 * k * tn * 2 + 2 * kc * tn * 4 + 2 * kc * tn * 4
    else:
        stream = None
        w_spec = pl.BlockSpec((None, k, tn), lambda j, i: (layer, 0, j))
        scratch = [pltpu.VMEM((k, tn), BF16)]
        est += 2 * k * tn * 4 + k * tn * 2
    in_specs = [pl.BlockSpec((tm, k), lambda j, i: (i, 0)), w_spec]
    args = [x, w]
    res_fn = None
    tile_mn = pl.BlockSpec((tm, tn), lambda j, i: (i, j))
    if residual is not None:
        kind = residual[0]
        est += 2 * tm * tn * 4
        if kind == "normed":
            _, t, stats, g, b = residual
            res_fn = _res_normed
            vec = pl.BlockSpec((1, tn), lambda j, i: (0, j))
            in_specs += [tile_mn, pl.BlockSpec((tm, 2 * LANES), lambda j, i: (i, 0)), vec, vec]
            args += [t, stats, g.reshape(1, n), b.reshape(1, n)]
            est += 2 * tm * 2 * LANES * 4
        else:
            _, p, sec = residual
            mp, ms = p.shape[0], sec.shape[0]
            tile, off = divmod(mp, tm)
            assert kind == "split" and off + ms <= tm and mp + ms == m
            last_p = (mp - 1) // tm
            res_fn = functools.partial(_res_split, tile=tile, off=off, axis=1)
            in_specs += [pl.BlockSpec((tm, tn), lambda j, i: (jnp.minimum(i, last_p), j)),
                         pl.BlockSpec((ms, tn), lambda j, i: (0, j))]
            args += [p, sec]
            est += 2 * ms * tn * 4
    return pl.pallas_call(
        functools.partial(_mm_body, n_res=len(args) - 2, res_fn=res_fn, res_scale=res_scale,
                          stream=stream),
        grid=(nj, ni),
        in_specs=in_specs,
        out_specs=tile_mn,
        out_shape=jax.ShapeDtypeStruct((m, n), out_dtype),
        scratch_shapes=scratch,
        compiler_params=pltpu.CompilerParams(
            dimension_semantics=("arbitrary", "arbitrary"), vmem_limit_bytes=_vmem_limit(est)),
        name=name,
    )(*args)


def _merge_body(za_ref, zb_ref, ga_ref, gb_ref, wa_ref, wb_ref, o_ref, wabf_ref, wbbf_ref):
    @pl.when(pl.program_id(1) == 0)
    def _():
        wabf_ref[...] = wa_ref[...].astype(BF16)
        wbbf_ref[...] = wb_ref[...].astype(BF16)

    a = jnp.dot(za_ref[...], wabf_ref[...], preferred_element_type=F32)
    b = jnp.dot(zb_ref[...], wbbf_ref[...], preferred_element_type=F32)
    ga = jax.nn.sigmoid(ga_ref[...].astype(F32))
    gb = jax.nn.sigmoid(gb_ref[...].astype(F32))
    o_ref[...] = (ga * a + gb * b).astype(o_ref.dtype)


def _merge(za, zb, h, w_a, w_b, layer, *, ga_off, gb_off, tm, tn):
    m, ka = za.shape
    kb = zb.shape[1]
    n = w_a.shape[2]
    ga_blk, gb_blk = ga_off // tn, gb_off // tn
    est = (2 * tm * (ka + kb) * 2 + 2 * (ka + kb) * tn * 4 + (ka + kb) * tn * 2
           + 2 * 2 * tm * tn * h.dtype.itemsize + 2 * tm * tn * 2 + 4 * tm * tn * 4)
    return pl.pallas_call(
        _merge_body,
        grid=(n // tn, m // tm),
        in_specs=[pl.BlockSpec((tm, ka), lambda j, i: (i, 0)),
                  pl.BlockSpec((tm, kb), lambda j, i: (i, 0)),
                  pl.BlockSpec((tm, tn), lambda j, i: (i, ga_blk + j)),
                  pl.BlockSpec((tm, tn), lambda j, i: (i, gb_blk + j)),
                  pl.BlockSpec((None, ka, tn), lambda j, i: (layer, 0, j)),
                  pl.BlockSpec((None, kb, tn), lambda j, i: (layer, 0, j))],
        out_specs=pl.BlockSpec((tm, tn), lambda j, i: (i, j)),
        out_shape=jax.ShapeDtypeStruct((m, n), BF16),
        scratch_shapes=[pltpu.VMEM((ka, tn), BF16), pltpu.VMEM((kb, tn), BF16)],
        compiler_params=pltpu.CompilerParams(
            dimension_semantics=("arbitrary", "arbitrary"), vmem_limit_bytes=_vmem_limit(est)),
        name="merge_mm",
    )(za, zb, h, h, w_a, w_b)


def _ln_body(t_ref, g_ref, b_ref, *out_refs, emit_f32):
    t = t_ref[...]
    mu = jnp.mean(t, axis=-1, keepdims=True)
    d = t - mu
    var = jnp.mean(d * d, axis=-1, keepdims=True)
    rstd = lax.rsqrt(var + LN_EPS)
    out = d * rstd * g_ref[...] + b_ref[...]
    if emit_f32:
        out_refs[0][...] = out
    else:
        obf_ref, stats_ref = out_refs
        obf_ref[...] = out.astype(BF16)
        rows = t.shape[0]
        stats_ref[:, :LANES] = jnp.broadcast_to(mu, (rows, LANES))
        stats_ref[:, LANES:] = jnp.broadcast_to(rstd, (rows, LANES))


def _layer_norm(t, g, b, *, row0, nrows, tm, emit_f32):
    d = t.shape[1]
    rb0 = row0 // tm
    assert row0 % tm == 0 and nrows % tm == 0
    est = 2 * tm * d * 4 + 2 * tm * d * 4 + 6 * tm * d * 4
    row = pl.BlockSpec((tm, d), lambda i: (i, 0))
    vec = pl.BlockSpec((1, d), lambda i: (0, 0))
    if emit_f32:
        out_specs = [row]
        out_shape = [jax.ShapeDtypeStruct((nrows, d), F32)]
    else:
        out_specs = [row, pl.BlockSpec((tm, 2 * LANES), lambda i: (i, 0))]
        out_shape = [jax.ShapeDtypeStruct((nrows, d), BF16),
                     jax.ShapeDtypeStruct((nrows, 2 * LANES), F32)]
    return pl.pallas_call(
        functools.partial(_ln_body, emit_f32=emit_f32),
        grid=(nrows // tm,),
        in_specs=[pl.BlockSpec((tm, d), lambda i: (rb0 + i, 0)), vec, vec],
        out_specs=out_specs,
        out_shape=out_shape,
        compiler_params=pltpu.CompilerParams(
            dimension_semantics=("arbitrary",), vmem_limit_bytes=_vmem_limit(est)),
        name="layer_norm",
    )(t, g.reshape(1, d), b.reshape(1, d))


def _ret_body(q_ref, k_ref, v_ref, g_ref, cos_ref, sin_ref, lg_ref, s0_ref, za_ref, sout_ref,
              s_ref, d_ref, qd_ref, kd_ref, *, c, hb):
    b = pl.program_id(1)
    ci = pl.program_id(2)

    @pl.when((b == 0) & (ci == 0))
    def _():
        ri = lax.broadcasted_iota(jnp.int32, (c, c), 0)
        cj = lax.broadcasted_iota(jnp.int32, (c, c), 1)
        diff = (ri - cj).astype(F32)
        r = lax.broadcasted_iota(jnp.int32, (c, RET_DK), 0).astype(F32)
        for hh in range(hb):
            lg = lg_ref[hh]
            d_ref[hh] = jnp.where(diff >= 0, jnp.exp(jnp.maximum(diff, 0.0) * lg[:, :c]), 0.0)
            qd_ref[hh] = jnp.exp((r + 1.0) * lg)
            kd_ref[hh] = jnp.exp((c - 1.0 - r) * lg)

    @pl.when(ci == 0)
    def _():
        s_ref[...] = s0_ref[...]

    cos = cos_ref[...]
    sin = sin_ref[...]
    half = RET_DK // 2

    def rope(t):
        t1, t2 = t[:, :half], t[:, half:]
        return jnp.concatenate([t1 * cos - t2 * sin, t1 * sin + t2 * cos], axis=-1)

    for hh in range(hb):
        qc = slice(hh * RET_DK, (hh + 1) * RET_DK)
        vc = slice(hh * RET_DV, (hh + 1) * RET_DV)
        q = rope(q_ref[:, qc].astype(F32))
        k = rope(k_ref[:, qc].astype(F32)) * (RET_DK ** -0.5)
        vb = v_ref[:, vc].astype(BF16)
        s = s_ref[hh]
        scores = lax.dot_general(q.astype(BF16), k.astype(BF16), (((1,), (1,)), ((), ())),
                                 preferred_element_type=F32) * d_ref[hh]
        o = (jnp.dot(scores.astype(BF16), vb, preferred_element_type=F32)
             + jnp.dot((q * qd_ref[hh]).astype(BF16), s.astype(BF16), preferred_element_type=F32))
        kv = lax.dot_general((k * kd_ref[hh]).astype(BF16), vb, (((0,), (0,)), ((), ())),
                             preferred_element_type=F32)
        s_ref[hh] = s * jnp.exp(float(c) * lg_ref[hh]) + kv

        mu = jnp.mean(o, axis=-1, keepdims=True)
        od = o - mu
        var = jnp.mean(od * od, axis=-1, keepdims=True)
        on = od * lax.rsqrt(var + GN_EPS)
        za_ref[:, vc] = (jax.nn.silu(g_ref[:, vc].astype(F32)) * on).astype(BF16)

    @pl.when(ci == pl.num_programs(2) - 1)
    def _():
        sout_ref[...] = s_ref[...]


def _drop_leading(n, body):
    def wrapped(*refs):
        return body(*refs[n:])
    return wrapped


def _retention(h, s0, s0_layer, *, row0, nseq, seqlen, pos0, hb, za_into, st_into, st_layer,
               depth):
    m = h.shape[0]
    c = min(RET_BLOCK, seqlen)
    nc = seqlen // c
    rb0 = row0 // c
    hq = RET_HEADS // hb
    half = RET_DK // 2
    pos = pos0 + jnp.arange(seqlen, dtype=F32)
    freqs = ROPE_BASE ** (-jnp.arange(half, dtype=F32) / half)
    ang = pos[:, None] * freqs[None, :]
    cos, sin = jnp.cos(ang), jnp.sin(ang)
    log_gamma = jnp.log(1.0 - 2.0 ** (-5.0 - jnp.arange(RET_HEADS, dtype=F32)))
    lg = jnp.broadcast_to(log_gamma[:, None, None], (RET_HEADS, 1, RET_DK))

    def hcol(off):
        return pl.BlockSpec((c, hb * RET_DK), lambda hh, b, ci: (rb0 + b * nc + ci, off + hh))

    tab = pl.BlockSpec((c, half), lambda hh, b, ci: (ci, 0))

    def state(layer):
        return pl.BlockSpec((None, None, hb, RET_DK, RET_DV),
                            lambda hh, b, ci: (layer, b, hh, 0, 0))

    donated = [za_into, st_into]
    assert za_into.shape == (m, RET_V) and st_into.shape == (depth, nseq, RET_HEADS, RET_DK, RET_DV)
    est = (2 * 4 * c * hb * RET_DK * 4 + 2 * 2 * hb * RET_DK * RET_DV * 4 + hb * RET_DK * RET_DV * 4
           + hb * (c * c + 2 * c * RET_DK) * 4 + 2 * c * hb * RET_DV * 2 + 16 * c * RET_DK * 4)
    return pl.pallas_call(
        _drop_leading(len(donated), functools.partial(_ret_body, c=c, hb=hb)),
        grid=(hq, nseq, nc),
        in_specs=[pl.BlockSpec(memory_space=pltpu.HBM)] * len(donated)
        + [hcol(0), hcol(hq), hcol(2 * hq), hcol(3 * hq), tab, tab,
           pl.BlockSpec((hb, 1, RET_DK), lambda hh, b, ci: (hh, 0, 0)), state(s0_layer)],
        out_specs=[pl.BlockSpec((c, hb * RET_DV), lambda hh, b, ci: (rb0 + b * nc + ci, hh)),
                   state(st_layer)],
        out_shape=[jax.ShapeDtypeStruct((m, RET_V), BF16),
                   jax.ShapeDtypeStruct((depth, nseq, RET_HEADS, RET_DK, RET_DV), F32)],
        input_output_aliases={0: 0, 1: 1},
        scratch_shapes=[pltpu.VMEM((hb, RET_DK, RET_DV), F32), pltpu.VMEM((hb, c, c), F32),
                        pltpu.VMEM((hb, c, RET_DK), F32), pltpu.VMEM((hb, c, RET_DK), F32)],
        compiler_params=pltpu.CompilerParams(
            dimension_semantics=("arbitrary", "arbitrary", "arbitrary"),
            vmem_limit_bytes=_vmem_limit(est)),
        name="retention",
    )(*donated, h, h, h, h, cos, sin, lg, s0)


def _sgu_body(su_ref, sv_ref, lng_ref, lnb_ref, w_ref, bias_ref, zb_ref, *rest, nvalid, **kw):
    if nvalid is None:
        _sgu_tile(su_ref, sv_ref, lng_ref, lnb_ref, w_ref, bias_ref, zb_ref, *rest, **kw)
        return

    @pl.when(pl.program_id(0) < nvalid)
    def _():
        _sgu_tile(su_ref, sv_ref, lng_ref, lnb_ref, w_ref, bias_ref, zb_ref, *rest, **kw)

    @pl.when(pl.program_id(0) >= nvalid)
    def _():
        zb_ref[...] = jnp.zeros(zb_ref.shape, zb_ref.dtype)


def _sgu_tile(su_ref, sv_ref, lng_ref, lnb_ref, w_ref, bias_ref, zb_ref, *rest, c, tm, gdim, emit_v):
    sv = jax.nn.gelu(sv_ref[...].astype(F32))
    mu = jnp.mean(sv, axis=-1, keepdims=True)
    d = sv - mu
    var = jnp.mean(d * d, axis=-1, keepdims=True)
    svn = d * lax.rsqrt(var + LN_EPS) * lng_ref[...] + lnb_ref[...]
    if emit_v:
        rest[0][...] = svn
    svb = svn.astype(BF16)
    ri = lax.broadcasted_iota(jnp.int32, (SGU_CHUNK, SGU_CHUNK), 0)
    cj = lax.broadcasted_iota(jnp.int32, (SGU_CHUNK, SGU_CHUNK), 1)
    mask = (ri >= cj) & ((ri // c) == (cj // c))
    for g in range(SGU_GROUPS):
        wm = jnp.where(mask, w_ref[g], 0.0).astype(BF16)
        cols = slice(g * gdim, (g + 1) * gdim)
        for r in range(tm // SGU_CHUNK):
            rows = slice(r * SGU_CHUNK, (r + 1) * SGU_CHUNK)
            mixed = jnp.dot(wm, svb[rows, cols], preferred_element_type=F32) + bias_ref[:, cols]
            zb_ref[rows, cols] = (jax.nn.gelu(su_ref[rows, cols].astype(F32)) * mixed).astype(BF16)


def _spatial_gating(h, ln_g, ln_b, w_s, b_s, *, row0, nrows, seqlen, su_off, width, emit_v,
                    zb_into):
    m = h.shape[0]
    c = min(SGU_CHUNK, seqlen)
    rep = SGU_CHUNK // c
    gdim = width // SGU_GROUPS
    w_blk = jnp.tile(w_s[:, :c, :c], (1, rep, rep))
    bias = jnp.repeat(jnp.tile(b_s[:, :c], (1, rep)).T, gdim, axis=1)
    tm = _largest_divisor(math.gcd(nrows, m), 256, SGU_CHUNK)
    rb0 = row0 // tm
    su_blk = su_off // width
    assert row0 % tm == 0
    nvalid = nrows // tm
    if zb_into is None:
        assert row0 == 0 and not emit_v
        donated, steps, fill = [], m // tm, nvalid
    else:
        assert zb_into.shape == (m, width)
        donated, steps, fill = [zb_into], nvalid, None
    last = rb0 + nvalid - 1
    vec = pl.BlockSpec((1, width), lambda i: (0, 0))
    out_specs = [pl.BlockSpec((tm, width), lambda i: (rb0 + i, 0))]
    out_shape = [jax.ShapeDtypeStruct((m, width), BF16)]
    if emit_v:
        out_specs.append(pl.BlockSpec((tm, width), lambda i: (i, 0)))
        out_shape.append(jax.ShapeDtypeStruct((nrows, width), F32))
    est = 2 * 2 * tm * width * 4 + 2 * tm * width * (2 + 4) + 8 * tm * width * 4
    res = pl.pallas_call(
        _drop_leading(len(donated), functools.partial(_sgu_body, nvalid=fill, c=c, tm=tm,
                                                      gdim=gdim, emit_v=emit_v)),
        grid=(steps,),
        in_specs=[pl.BlockSpec(memory_space=pltpu.HBM)] * len(donated)
        + [pl.BlockSpec((tm, width), lambda i: (jnp.minimum(rb0 + i, last), su_blk)),
           pl.BlockSpec((tm, width), lambda i: (jnp.minimum(rb0 + i, last), su_blk + 1)),
           vec, vec,
           pl.BlockSpec((SGU_GROUPS, SGU_CHUNK, SGU_CHUNK), lambda i: (0, 0, 0)),
           pl.BlockSpec((SGU_CHUNK, width), lambda i: (0, 0))],
        out_specs=out_specs,
        out_shape=out_shape,
        input_output_aliases={0: 0} if donated else {},
        compiler_params=pltpu.CompilerParams(
            dimension_semantics=("arbitrary",), vmem_limit_bytes=_vmem_limit(est)),
        name="spatial_gating",
    )(*donated, h, h, ln_g.reshape(1, width), ln_b.reshape(1, width), w_blk, bias)
    return res if emit_v else (res[0], None)


def _rows(ref, slab, start, n, stride):
    if n == 1:
        return ref[slab, start:start + 1, :]
    return ref[slab, pl.ds(start, n, stride=stride), :]


def _set_rows(ref, slab, start, n, stride, value):
    if n == 1:
        ref[slab, start:start + 1, :] = value
    else:
        ref[slab, pl.ds(start, n, stride=stride), :] = value


def _upgate_body(x_ref, w_hbm, cw_ref, cb_ref, st_ref, z_ref, tail_ref,
                 wbf_ref, stage_ref, sem_ref, buf_ref, conv_ref, val_ref,
                 *, tm, tn, starts, ends, stream):
    i = pl.program_id(1)
    slabs = tn // LANES
    w_tile = _streamed_weight_tile(w_hbm, wbf_ref, stage_ref, sem_ref, **stream)

    @pl.when(i == 0)
    def _():
        buf_ref[:, 0:8, :] = jnp.zeros((slabs, 8, LANES), F32)

    @pl.when(i > 0)
    def _():
        buf_ref[:, 0:8, :] = buf_ref[:, tm:tm + 8, :]

    gv = jnp.dot(x_ref[...], w_tile, preferred_element_type=F32)
    val_ref[...] = gv[:, tn:]
    for s in range(slabs):
        lanes = slice(s * LANES, (s + 1) * LANES)
        buf_ref[s, 8:tm + 8, :] = gv[:, lanes]
        conv_ref[s] = (cb_ref[:, lanes] + buf_ref[s, 6:tm + 6, :] * cw_ref[0:1, lanes]
                       + buf_ref[s, 7:tm + 7, :] * cw_ref[1:2, lanes]
                       + gv[:, lanes] * cw_ref[2:3, lanes])

    for tile, off, n, stride, slot in starts:
        @pl.when(i == tile)
        def _(off=off, n=n, stride=stride, slot=slot):
            for s in range(slabs):
                lanes = slice(s * LANES, (s + 1) * LANES)
                w0, w1, w2 = cw_ref[0:1, lanes], cw_ref[1:2, lanes], cw_ref[2:3, lanes]
                cb = cb_ref[:, lanes]
                p0 = _rows(buf_ref, s, 8 + off, n, stride)
                p1 = _rows(buf_ref, s, 8 + off + 1, n, stride)
                if slot is None:
                    c0 = cb + p0 * w2
                    c1 = cb + p0 * w1 + p1 * w2
                else:
                    s0 = st_ref[0, slot:slot + n, lanes]
                    s1 = st_ref[1, slot:slot + n, lanes]
                    c0 = cb + s0 * w0 + s1 * w1 + p0 * w2
                    c1 = cb + s1 * w0 + p0 * w1 + p1 * w2
                _set_rows(conv_ref, s, off, n, stride, c0)
                _set_rows(conv_ref, s, off + 1, n, stride, c1)

    for tile, off, n, stride, slot in ends:
        @pl.when(i == tile)
        def _(off=off, n=n, stride=stride, slot=slot):
            for s in range(slabs):
                lanes = slice(s * LANES, (s + 1) * LANES)
                tail_ref[0, slot:slot + n, lanes] = _rows(buf_ref, s, 8 + off - 1, n, stride)
                tail_ref[1, slot:slot + n, lanes] = _rows(buf_ref, s, 8 + off, n, stride)

    for s in range(slabs):
        lanes = slice(s * LANES, (s + 1) * LANES)
        z_ref[:, lanes] = (jax.nn.gelu(conv_ref[s]) * val_ref[:, lanes]).astype(BF16)


def _seq_groups(seq_rows, seqlen, tm, within):
    groups = []
    for idx, r0 in enumerate(seq_rows):
        row = r0 + within
        tile, off = divmod(row, tm)
        if groups and groups[-1][0] == tile and seq_rows[idx - 1] + seqlen == r0:
            t, o, n, st, first = groups[-1]
            groups[-1] = (t, o, n + 1, st, first)
        else:
            groups.append((tile, off, 1, seqlen, idx))
    return groups


def _up_gate(xb, w_up, layer, conv_w, conv_b, conv_s0_s, *, bp, lp, bs, ls, tm, tn):
    m, k = xb.shape
    dff = w_up.shape[2] // 2
    nseq = bp + bs
    prompt_rows = [b * lp for b in range(bp)]
    sample_rows = [bp * lp + b * ls for b in range(bs)]
    starts, ends = [], []
    for rows, sl, has_state, slot0 in ((sample_rows, ls, True, 0), (prompt_rows, lp, False, bs)):
        for t, o, n, st, first in _seq_groups(rows, sl, tm, 0):
            assert o + (n - 1) * st + 1 < tm, "a sequence's first two rows must share a row tile"
            starts.append((t, o, n, st, first if has_state else None))
        for t, o, n, st, first in _seq_groups(rows, sl, tm, sl - 1):
            assert o >= 1, "a sequence's last two rows must share a row tile"
            ends.append((t, o, n, st, slot0 + first))
    st_in = jnp.transpose(conv_s0_s, (1, 0, 2))
    nj, ni = dff // tn, m // tm
    kc = k // ni
    assert k == kc * ni and kc % 16 == 0 and ni % 2 == 0, (k, ni)
    stream = dict(layer=layer, kc=kc, tn=tn, nj=nj, ni=ni, col_offsets=(0, dff))
    est = (2 * tm * k * 2 + 2 * k * 2 * tn * 2 + 2 * 2 * kc * tn * 4 + 2 * (tm + 8) * tn * 4
           + 2 * tm * tn * 2 + 2 * 2 * (bs + nseq) * tn * 4 + 8 * tm * tn * 4)
    return pl.pallas_call(
        functools.partial(_upgate_body, tm=tm, tn=tn, starts=tuple(starts), ends=tuple(ends),
                          stream=stream),
        grid=(nj, ni),
        in_specs=[pl.BlockSpec((tm, k), lambda j, i: (i, 0)),
                  pl.BlockSpec(memory_space=pltpu.HBM),
                  pl.BlockSpec((CONV_W, tn), lambda j, i: (0, j)),
                  pl.BlockSpec((1, tn), lambda j, i: (0, j)),
                  pl.BlockSpec((CONV_W - 1, bs, tn), lambda j, i: (0, 0, j))],
        out_specs=[pl.BlockSpec((tm, tn), lambda j, i: (i, j)),
                   pl.BlockSpec((CONV_W - 1, nseq, tn), lambda j, i: (0, 0, j))],
        out_shape=[jax.ShapeDtypeStruct((m, dff), BF16),
                   jax.ShapeDtypeStruct((CONV_W - 1, nseq, dff), F32)],
        scratch_shapes=[pltpu.VMEM((2, k, 2 * tn), BF16), pltpu.VMEM((2, 2, kc, tn), F32),
                        pltpu.SemaphoreType.DMA((2, 2)),
                        pltpu.VMEM((tn // LANES, tm + 8, LANES), F32),
                        pltpu.VMEM((tn // LANES, tm, LANES), F32), pltpu.VMEM((tm, tn), F32)],
        compiler_params=pltpu.CompilerParams(
            dimension_semantics=("arbitrary", "arbitrary"), vmem_limit_bytes=_vmem_limit(est)),
        name="up_gate",
    )(xb, w_up, conv_w, conv_b.reshape(1, dff), st_in)


def _layer(layer, depth, x_res, xb, state_ret, conv_s0_s, ret_bufs, big, small, *, bp, lp, bs,
           ls, alpha, last):
    w_in, w_a, w_b, w_o, w_up, w_down = big
    ln1_g, ln1_b, sgu_ln_g, sgu_ln_b, sgu_w, sgu_b, conv_w, conv_b, ln2_g, ln2_b = small
    m, d = xb.shape
    mp = bp * lp
    ms = bs * ls
    width = d // 2
    dff = w_down.shape[1]
    su_off = 2 * RET_QK + 2 * RET_V
    ga_off = su_off + 2 * width
    gb_off = ga_off + d
    tm_big = _largest_divisor(m, 1088, 16)

    h = _matmul(xb, w_in, layer, tm=tm_big, tn=1024, out_dtype=BF16, name="mm_in",
                stream_weights=True)

    zero_ret = jnp.zeros((1, bp, RET_HEADS, RET_DK, RET_DV), F32)
    za, ret_p = _retention(h, zero_ret, 0, row0=0, nseq=bp, seqlen=lp, pos0=0.0, hb=4,
                           za_into=jnp.zeros((m, RET_V), BF16), st_into=ret_bufs[0],
                           st_layer=layer, depth=depth)
    za, ret_s = _retention(h, state_ret, layer, row0=mp, nseq=bs, seqlen=ls,
                           pos0=float(PAST_LEN), hb=RET_HEADS,
                           za_into=za, st_into=ret_bufs[1], st_layer=layer, depth=depth)
    zb, _ = _spatial_gating(h, sgu_ln_g, sgu_ln_b, sgu_w, sgu_b, row0=0, nrows=mp, seqlen=lp,
                            su_off=su_off, width=width, emit_v=False, zb_into=None)
    zb, sv_s = _spatial_gating(h, sgu_ln_g, sgu_ln_b, sgu_w, sgu_b, row0=mp, nrows=ms, seqlen=ls,
                               su_off=su_off, width=width, emit_v=True, zb_into=zb)

    merged = _merge(za, zb, h, w_a, w_b, layer, ga_off=ga_off, gb_off=gb_off, tm=tm_big, tn=512)
    t1 = _matmul(merged, w_o, layer, tm=tm_big, tn=512, out_dtype=F32, name="mm_o",
                 residual=x_res, res_scale=alpha)
    tm_ln = _largest_divisor(math.gcd(mp, ms), 512, 8)
    x1b, stats1 = _layer_norm(t1, ln1_g, ln1_b, row0=0, nrows=m, tm=tm_ln, emit_f32=False)

    z, tails = _up_gate(x1b, w_up, layer, conv_w, conv_b, conv_s0_s, bp=bp, lp=lp, bs=bs, ls=ls,
                        tm=tm_big, tn=_largest_divisor(dff, 256, LANES))
    t2 = _matmul(z, w_down, layer, tm=_largest_divisor(m, 544, 16), tn=512, out_dtype=F32,
                 name="mm_down", stream_weights=True,
                 residual=("normed", t1, stats1, ln1_g, ln1_b), res_scale=alpha)
    if last:
        (y_p,) = _layer_norm(t2, ln2_g, ln2_b, row0=0, nrows=mp, tm=tm_ln, emit_f32=True)
        (y_s,) = _layer_norm(t2, ln2_g, ln2_b, row0=mp, nrows=ms, tm=tm_ln, emit_f32=True)
        x2_res, x2b = (y_p, y_s), None
    else:
        x2b, stats2 = _layer_norm(t2, ln2_g, ln2_b, row0=0, nrows=m, tm=tm_ln, emit_f32=False)
        x2_res = ("normed", t2, stats2, ln2_g, ln2_b)

    conv_s = jnp.transpose(tails[:, :bs], (1, 0, 2))
    conv_p = jnp.transpose(tails[:, bs:], (1, 0, 2))
    return x2_res, x2b, (ret_p, ret_s), conv_p, conv_s, sv_s.reshape(bs, ls, width)


def kernel(x_prompt, x_sample, state_ret, state_conv, w_in, w_a, w_b, w_o, ln1_g, ln1_b,
           sgu_ln_g, sgu_ln_b, sgu_w, sgu_b, w_up, conv_w, conv_b, w_down, ln2_g, ln2_b):
    bp, lp, d = x_prompt.shape
    bs, ls, _ = x_sample.shape
    depth = w_in.shape[0]
    alpha = float((2 * depth) ** 0.25)
    xp, xs = x_prompt.reshape(bp * lp, d), x_sample.reshape(bs * ls, d)
    xb = _concat_cast(xp, xs, tm=_largest_divisor(math.gcd(bp * lp, bs * ls), 512, 16), dtype=BF16)
    x = ("split", xp, xs)
    big = (w_in, w_a, w_b, w_o, w_up, w_down)
    ret_bufs = tuple(jnp.zeros((depth, n, RET_HEADS, RET_DK, RET_DV), F32) for n in (bp, bs))
    conv_p, conv_s, sgu_v = [], [], []
    for l in range(depth):
        small = (ln1_g[l], ln1_b[l], sgu_ln_g[l], sgu_ln_b[l], sgu_w[l], sgu_b[l],
                 conv_w[l], conv_b[l], ln2_g[l], ln2_b[l])
        x, xb, ret_bufs, cp, cs, sv = _layer(
            l, depth, x, xb, state_ret, state_conv[l], ret_bufs, big, small,
            bp=bp, lp=lp, bs=bs, ls=ls, alpha=alpha, last=l == depth - 1)
        conv_p.append(cp)
        conv_s.append(cs)
        sgu_v.append(sv)
    y_p, y_s = x
    return (y_p.reshape(bp, lp, d), y_s.reshape(bs, ls, d), ret_bufs[0], jnp.stack(conv_p),
            ret_bufs[1], jnp.stack(conv_s), jnp.stack(sgu_v))
```

```python
import functools
import math

import jax
import jax.numpy as jnp
from jax import lax
from jax.experimental import pallas as pl
from jax.experimental.pallas import tpu as pltpu

F32 = jnp.float32
BF16 = jnp.bfloat16

RET_HEADS = 8
RET_DK = 256
RET_DV = 256
RET_QK = RET_HEADS * RET_DK
RET_V = RET_HEADS * RET_DV
SGU_GROUPS = 8
SGU_CHUNK = 128
CONV_W = 3
PAST_LEN = 1024
ROPE_BASE = 10000.0
LN_EPS = 1e-5
GN_EPS = 1e-6

V7X_VMEM_BYTES = 64 * 1024 * 1024
LANES = 128
RET_BLOCK = 256


def _vmem_limit(estimate_bytes):
    return int(min(V7X_VMEM_BYTES - (4 << 20), max(estimate_bytes * 5 // 4 + (2 << 20), 16 << 20)))


def _largest_divisor(n, limit, multiple):
    best = None
    d = multiple
    while d <= min(n, limit):
        if n % d == 0:
            best = d
        d += multiple
    assert best is not None, (n, limit, multiple)
    return best


def _mm_body(x_ref, w_ref, *rest, n_res, res_fn, res_scale, stream):
    o_ref, wbf_ref = rest[n_res], rest[n_res + 1]
    if stream is None:
        @pl.when(pl.program_id(1) == 0)
        def _():
            wbf_ref[...] = w_ref[...].astype(BF16)

        w_tile = wbf_ref[...]
    else:
        w_tile = _streamed_weight_tile(w_ref, wbf_ref, rest[n_res + 2], rest[n_res + 3], **stream)

    acc = jnp.dot(x_ref[...], w_tile, preferred_element_type=F32)
    if n_res:
        acc = res_scale * res_fn(*rest[:n_res]) + acc
    o_ref[...] = acc.astype(o_ref.dtype)


def _streamed_weight_tile(w_hbm, wbf_ref, stage_ref, sem_ref, *, layer, kc, tn, nj, ni):
    j = pl.program_id(0)
    i = pl.program_id(1)
    cur = j % 2

    def chunk_copy(tile, c, slot):
        return pltpu.make_async_copy(
            w_hbm.at[layer, pl.ds(c * kc, kc), pl.ds(tile * tn, tn)],
            stage_ref.at[slot], sem_ref.at[slot])

    def land(tile, c, slot, wslot):
        chunk_copy(tile, c, slot).wait()
        row0 = c * kc if isinstance(c, int) else pl.multiple_of(c * kc, kc)
        wbf_ref[wslot, pl.ds(row0, kc), :] = stage_ref[slot].astype(BF16)

    @pl.when((j == 0) & (i == 0))
    def _():
        chunk_copy(0, 0, 0).start()
        for c in range(ni):
            if c + 1 < ni:
                chunk_copy(0, c + 1, (c + 1) % 2).start()
            land(0, c, c % 2, 0)

    @pl.when((j > 0) & (i == 0))
    def _():
        land(j, ni - 1, (ni - 1) % 2, cur)

    @pl.when(j + 1 < nj)
    def _():
        chunk_copy(j + 1, i, i % 2).start()

        @pl.when(i > 0)
        def _():
            land(j + 1, i - 1, (i - 1) % 2, 1 - cur)

    return wbf_ref[cur]


def _res_normed(t_ref, stats_ref, g_ref, b_ref):
    reps = t_ref.shape[1] // LANES
    mu = jnp.concatenate([stats_ref[:, :LANES]] * reps, axis=1)
    rstd = jnp.concatenate([stats_ref[:, LANES:]] * reps, axis=1)
    return (t_ref[...] - mu) * rstd * g_ref[...] + b_ref[...]


def _res_split(p_ref, s_ref, *, tile, off, axis):
    p = p_ref[...]
    n = s_ref.shape[0]
    parts = [p[:off], s_ref[...], p[off + n:]]
    mixed = jnp.concatenate([q for q in parts if q.shape[0]], axis=0)
    return jnp.where(pl.program_id(axis) == tile, mixed, p)


def _cast_body(p_ref, s_ref, o_ref, *, tile, off):
    o_ref[...] = _res_split(p_ref, s_ref, tile=tile, off=off, axis=0).astype(o_ref.dtype)


def _concat_cast(p, sec, *, tm, dtype):
    mp, d = p.shape
    ms = sec.shape[0]
    tile, off = divmod(mp, tm)
    assert off + ms <= tm and (mp + ms) % tm == 0
    last_p = (mp - 1) // tm
    est = 2 * 2 * tm * d * 4 + 2 * tm * d * jnp.dtype(dtype).itemsize + 2 * tm * d * 4
    return pl.pallas_call(
        functools.partial(_cast_body, tile=tile, off=off),
        grid=((mp + ms) // tm,),
        in_specs=[pl.BlockSpec((tm, d), lambda i: (jnp.minimum(i, last_p), 0)),
                  pl.BlockSpec((ms, d), lambda i: (0, 0))],
        out_specs=pl.BlockSpec((tm, d), lambda i: (i, 0)),
        out_shape=jax.ShapeDtypeStruct((mp + ms, d), dtype),
        compiler_params=pltpu.CompilerParams(
            dimension_semantics=("arbitrary",), vmem_limit_bytes=_vmem_limit(est)),
        name="concat_cast",
    )(p, sec)


def _matmul(x, w, layer, *, tm, tn, out_dtype, name, stream_weights=False, residual=None,
            res_scale=None):
    m, k = x.shape
    n = w.shape[2]
    nj, ni = n // tn, m // tm
    est = 2 * tm * k * 2 + 2 * tm * tn * jnp.dtype(out_dtype).itemsize
    if stream_weights:
        kc = k // ni
        assert k == kc * ni and kc % 16 == 0 and ni % 2 == 0, (k, ni)
        stream = dict(layer=layer, kc=kc, tn=tn, nj=nj, ni=ni)
        w_spec = pl.BlockSpec(memory_space=pltpu.HBM)
        scratch = [pltpu.VMEM((2, k, tn), BF16), pltpu.VMEM((2, kc, tn), F32),
                   pltpu.SemaphoreType.DMA((2,))]
        est += 2 * k * tn * 2 + 2 * kc * tn * 4 + 2 * kc * tn * 4
    else:
        stream = None
        w_spec = pl.BlockSpec((None, k, tn), lambda j, i: (layer, 0, j))
        scratch = [pltpu.VMEM((k, tn), BF16)]
        est += 2 * k * tn * 4 + k * tn * 2
    in_specs = [pl.BlockSpec((tm, k), lambda j, i: (i, 0)), w_spec]
    args = [x, w]
    res_fn = None
    tile_mn = pl.BlockSpec((tm, tn), lambda j, i: (i, j))
    if residual is not None:
        kind = residual[0]
        est += 2 * tm * tn * 4
        if kind == "normed":
            _, t, stats, g, b = residual
            res_fn = _res_normed
            vec = pl.BlockSpec((1, tn), lambda j, i: (0, j))
            in_specs += [tile_mn, pl.BlockSpec((tm, 2 * LANES), lambda j, i: (i, 0)), vec, vec]
            args += [t, stats, g.reshape(1, n), b.reshape(1, n)]
            est += 2 * tm * 2 * LANES * 4
        else:
            _, p, sec = residual
            mp, ms = p.shape[0], sec.shape[0]
            tile, off = divmod(mp, tm)
            assert kind == "split" and off + ms <= tm and mp + ms == m
            last_p = (mp - 1) // tm
            res_fn = functools.partial(_res_split, tile=tile, off=off, axis=1)
            in_specs += [pl.BlockSpec((tm, tn), lambda j, i: (jnp.minimum(i, last_p), j)),
                         pl.BlockSpec((ms, tn), lambda j, i: (0, j))]
            args += [p, sec]
            est += 2 * ms * tn * 4
    return pl.pallas_call(
        functools.partial(_mm_body, n_res=len(args) - 2, res_fn=res_fn, res_scale=res_scale,
                          stream=stream),
        grid=(nj, ni),
        in_specs=in_specs,
        out_specs=tile_mn,
        out_shape=jax.ShapeDtypeStruct((m, n), out_dtype),
        scratch_shapes=scratch,
        compiler_params=pltpu.CompilerParams(
            dimension_semantics=("arbitrary", "arbitrary"), vmem_limit_bytes=_vmem_limit(est)),
        name=name,
    )(*args)


def _merge_body(za_ref, zb_ref, ga_ref, gb_ref, wa_ref, wb_ref, o_ref, wabf_ref, wbbf_ref):
    @pl.when(pl.program_id(1) == 0)
    def _():
        wabf_ref[...] = wa_ref[...].astype(BF16)
        wbbf_ref[...] = wb_ref[...].astype(BF16)

    a = jnp.dot(za_ref[...], wabf_ref[...], preferred_element_type=F32)
    b = jnp.dot(zb_ref[...], wbbf_ref[...], preferred_element_type=F32)
    ga = jax.nn.sigmoid(ga_ref[...].astype(F32))
    gb = jax.nn.sigmoid(gb_ref[...].astype(F32))
    o_ref[...] = (ga * a + gb * b).astype(o_ref.dtype)


def _merge(za, zb, h, w_a, w_b, layer, *, ga_off, gb_off, tm, tn):
    m, ka = za.shape
    kb = zb.shape[1]
    n = w_a.shape[2]
    ga_blk, gb_blk = ga_off // tn, gb_off // tn
    est = (2 * tm * (ka + kb) * 2 + 2 * (ka + kb) * tn * 4 + (ka + kb) * tn * 2
           + 2 * 2 * tm * tn * h.dtype.itemsize + 2 * tm * tn * 2 + 4 * tm * tn * 4)
    return pl.pallas_call(
        _merge_body,
        grid=(n // tn, m // tm),
        in_specs=[pl.BlockSpec((tm, ka), lambda j, i: (i, 0)),
                  pl.BlockSpec((tm, kb), lambda j, i: (i, 0)),
                  pl.BlockSpec((tm, tn), lambda j, i: (i, ga_blk + j)),
                  pl.BlockSpec((tm, tn), lambda j, i: (i, gb_blk + j)),
                  pl.BlockSpec((None, ka, tn), lambda j, i: (layer, 0, j)),
                  pl.BlockSpec((None, kb, tn), lambda j, i: (layer, 0, j))],
        out_specs=pl.BlockSpec((tm, tn), lambda j, i: (i, j)),
        out_shape=jax.ShapeDtypeStruct((m, n), BF16),
        scratch_shapes=[pltpu.VMEM((ka, tn), BF16), pltpu.VMEM((kb, tn), BF16)],
        compiler_params=pltpu.CompilerParams(
            dimension_semantics=("arbitrary", "arbitrary"), vmem_limit_bytes=_vmem_limit(est)),
        name="merge_mm",
    )(za, zb, h, h, w_a, w_b)


def _ln_body(t_ref, g_ref, b_ref, *out_refs, emit_f32):
    t = t_ref[...]
    mu = jnp.mean(t, axis=-1, keepdims=True)
    d = t - mu
    var = jnp.mean(d * d, axis=-1, keepdims=True)
    rstd = lax.rsqrt(var + LN_EPS)
    out = d * rstd * g_ref[...] + b_ref[...]
    if emit_f32:
        out_refs[0][...] = out
    else:
        obf_ref, stats_ref = out_refs
        obf_ref[...] = out.astype(BF16)
        rows = t.shape[0]
        stats_ref[:, :LANES] = jnp.broadcast_to(mu, (rows, LANES))
        stats_ref[:, LANES:] = jnp.broadcast_to(rstd, (rows, LANES))


def _layer_norm(t, g, b, *, row0, nrows, tm, emit_f32):
    d = t.shape[1]
    rb0 = row0 // tm
    assert row0 % tm == 0 and nrows % tm == 0
    est = 2 * tm * d * 4 + 2 * tm * d * 4 + 6 * tm * d * 4
    row = pl.BlockSpec((tm, d), lambda i: (i, 0))
    vec = pl.BlockSpec((1, d), lambda i: (0, 0))
    if emit_f32:
        out_specs = [row]
        out_shape = [jax.ShapeDtypeStruct((nrows, d), F32)]
    else:
        out_specs = [row, pl.BlockSpec((tm, 2 * LANES), lambda i: (i, 0))]
        out_shape = [jax.ShapeDtypeStruct((nrows, d), BF16),
                     jax.ShapeDtypeStruct((nrows, 2 * LANES), F32)]
    return pl.pallas_call(
        functools.partial(_ln_body, emit_f32=emit_f32),
        grid=(nrows // tm,),
        in_specs=[pl.BlockSpec((tm, d), lambda i: (rb0 + i, 0)), vec, vec],
        out_specs=out_specs,
        out_shape=out_shape,
        compiler_params=pltpu.CompilerParams(
            dimension_semantics=("arbitrary",), vmem_limit_bytes=_vmem_limit(est)),
        name="layer_norm",
    )(t, g.reshape(1, d), b.reshape(1, d))


def _ret_body(q_ref, k_ref, v_ref, g_ref, cos_ref, sin_ref, lg_ref, s0_ref, za_ref, sout_ref,
              s_ref, d_ref, qd_ref, kd_ref, *, c, hb, fill_slot):
    b = pl.program_id(1)
    ci = pl.program_id(2)

    @pl.when((b == 0) & (ci == 0))
    def _():
        ri = lax.broadcasted_iota(jnp.int32, (c, c), 0)
        cj = lax.broadcasted_iota(jnp.int32, (c, c), 1)
        diff = (ri - cj).astype(F32)
        r = lax.broadcasted_iota(jnp.int32, (c, RET_DK), 0).astype(F32)
        for hh in range(hb):
            lg = lg_ref[hh]
            d_ref[hh] = jnp.where(diff >= 0, jnp.exp(jnp.maximum(diff, 0.0) * lg[:, :c]), 0.0)
            qd_ref[hh] = jnp.exp((r + 1.0) * lg)
            kd_ref[hh] = jnp.exp((c - 1.0 - r) * lg)

    @pl.when(ci == 0)
    def _():
        s_ref[...] = s0_ref[...]

    cos = cos_ref[...]
    sin = sin_ref[...]
    half = RET_DK // 2

    def rope(t):
        t1, t2 = t[:, :half], t[:, half:]
        return jnp.concatenate([t1 * cos - t2 * sin, t1 * sin + t2 * cos], axis=-1)

    for hh in range(hb):
        qc = slice(hh * RET_DK, (hh + 1) * RET_DK)
        vc = slice(hh * RET_DV, (hh + 1) * RET_DV)
        q = rope(q_ref[:, qc].astype(F32))
        k = rope(k_ref[:, qc].astype(F32)) * (RET_DK ** -0.5)
        vb = v_ref[:, vc].astype(BF16)
        s = s_ref[hh]
        scores = lax.dot_general(q.astype(BF16), k.astype(BF16), (((1,), (1,)), ((), ())),
                                 preferred_element_type=F32) * d_ref[hh]
        o = (jnp.dot(scores.astype(BF16), vb, preferred_element_type=F32)
             + jnp.dot((q * qd_ref[hh]).astype(BF16), s.astype(BF16), preferred_element_type=F32))
        kv = lax.dot_general((k * kd_ref[hh]).astype(BF16), vb, (((0,), (0,)), ((), ())),
                             preferred_element_type=F32)
        s_ref[hh] = s * jnp.exp(float(c) * lg_ref[hh]) + kv

        mu = jnp.mean(o, axis=-1, keepdims=True)
        od = o - mu
        var = jnp.mean(od * od, axis=-1, keepdims=True)
        on = od * lax.rsqrt(var + GN_EPS)
        za_ref[:, vc] = (jax.nn.silu(g_ref[:, vc].astype(F32)) * on).astype(BF16)

    @pl.when(ci == pl.num_programs(2) - 1)
    def _():
        if fill_slot is None:
            sout_ref[...] = s_ref[...]
        else:
            for slot in range(sout_ref.shape[0]):
                sout_ref[slot] = s_ref[...] if slot == fill_slot else jnp.zeros(s_ref.shape, F32)


def _drop_leading(n, body):
    def wrapped(*refs):
        return body(*refs[n:])
    return wrapped


def _retention(h, s0, s0_layer, *, row0, nseq, seqlen, pos0, hb, za_into, st_into, st_layer,
               depth):
    m = h.shape[0]
    c = min(RET_BLOCK, seqlen)
    nc = seqlen // c
    rb0 = row0 // c
    hq = RET_HEADS // hb
    half = RET_DK // 2
    pos = pos0 + jnp.arange(seqlen, dtype=F32)
    freqs = ROPE_BASE ** (-jnp.arange(half, dtype=F32) / half)
    ang = pos[:, None] * freqs[None, :]
    cos, sin = jnp.cos(ang), jnp.sin(ang)
    log_gamma = jnp.log(1.0 - 2.0 ** (-5.0 - jnp.arange(RET_HEADS, dtype=F32)))
    lg = jnp.broadcast_to(log_gamma[:, None, None], (RET_HEADS, 1, RET_DK))

    def hcol(off):
        return pl.BlockSpec((c, hb * RET_DK), lambda hh, b, ci: (rb0 + b * nc + ci, off + hh))

    tab = pl.BlockSpec((c, half), lambda hh, b, ci: (ci, 0))

    def state(layer):
        return pl.BlockSpec((None, None, hb, RET_DK, RET_DV),
                            lambda hh, b, ci: (layer, b, hh, 0, 0))

    assert za_into.shape == (m, RET_V)
    st_shape = (depth, nseq, RET_HEADS, RET_DK, RET_DV)
    if st_into is None:
        donated, aliases, fill_slot = [za_into], {0: 0}, st_layer
        st_spec = pl.BlockSpec((depth, None, hb, RET_DK, RET_DV),
                               lambda hh, b, ci: (0, b, hh, 0, 0))
    else:
        assert st_into.shape == st_shape
        donated, aliases, fill_slot = [za_into, st_into], {0: 0, 1: 1}, None
        st_spec = state(st_layer)
    est = (2 * 4 * c * hb * RET_DK * 2 + 2 * (1 + depth) * hb * RET_DK * RET_DV * 4
           + hb * RET_DK * RET_DV * 4 + hb * (c * c + 2 * c * RET_DK) * 4
           + 2 * c * hb * RET_DV * 2 + 16 * c * RET_DK * 4)
    return pl.pallas_call(
        _drop_leading(len(donated),
                      functools.partial(_ret_body, c=c, hb=hb, fill_slot=fill_slot)),
        grid=(hq, nseq, nc),
        in_specs=[pl.BlockSpec(memory_space=pltpu.HBM)] * len(donated)
        + [hcol(0), hcol(hq), hcol(2 * hq), hcol(3 * hq), tab, tab,
           pl.BlockSpec((hb, 1, RET_DK), lambda hh, b, ci: (hh, 0, 0)), state(s0_layer)],
        out_specs=[pl.BlockSpec((c, hb * RET_DV), lambda hh, b, ci: (rb0 + b * nc + ci, hh)),
                   st_spec],
        out_shape=[jax.ShapeDtypeStruct((m, RET_V), BF16), jax.ShapeDtypeStruct(st_shape, F32)],
        input_output_aliases=aliases,
        scratch_shapes=[pltpu.VMEM((hb, RET_DK, RET_DV), F32), pltpu.VMEM((hb, c, c), F32),
                        pltpu.VMEM((hb, c, RET_DK), F32), pltpu.VMEM((hb, c, RET_DK), F32)],
        compiler_params=pltpu.CompilerParams(
            dimension_semantics=("arbitrary", "arbitrary", "arbitrary"),
            vmem_limit_bytes=_vmem_limit(est)),
        name="retention",
    )(*donated, h, h, h, h, cos, sin, lg, s0)


def _sgu_body(su_ref, sv_ref, lng_ref, lnb_ref, w_ref, bias_ref, zb_ref, *rest, nvalid, **kw):
    if nvalid is None:
        _sgu_tile(su_ref, sv_ref, lng_ref, lnb_ref, w_ref, bias_ref, zb_ref, *rest, **kw)
        return

    @pl.when(pl.program_id(0) < nvalid)
    def _():
        _sgu_tile(su_ref, sv_ref, lng_ref, lnb_ref, w_ref, bias_ref, zb_ref, *rest, **kw)

    @pl.when(pl.program_id(0) >= nvalid)
    def _():
        zb_ref[...] = jnp.zeros(zb_ref.shape, zb_ref.dtype)


def _sgu_tile(su_ref, sv_ref, lng_ref, lnb_ref, w_ref, bias_ref, zb_ref, *rest, c, tm, gdim, emit_v):
    sv = jax.nn.gelu(sv_ref[...].astype(F32))
    mu = jnp.mean(sv, axis=-1, keepdims=True)
    d = sv - mu
    var = jnp.mean(d * d, axis=-1, keepdims=True)
    svn = d * lax.rsqrt(var + LN_EPS) * lng_ref[...] + lnb_ref[...]
    if emit_v:
        rest[0][...] = svn
    svb = svn.astype(BF16)
    ri = lax.broadcasted_iota(jnp.int32, (SGU_CHUNK, SGU_CHUNK), 0)
    cj = lax.broadcasted_iota(jnp.int32, (SGU_CHUNK, SGU_CHUNK), 1)
    mask = (ri >= cj) & ((ri // c) == (cj // c))
    for g in range(SGU_GROUPS):
        wm = jnp.where(mask, w_ref[g], 0.0).astype(BF16)
        cols = slice(g * gdim, (g + 1) * gdim)
        for r in range(tm // SGU_CHUNK):
            rows = slice(r * SGU_CHUNK, (r + 1) * SGU_CHUNK)
            mixed = jnp.dot(wm, svb[rows, cols], preferred_element_type=F32) + bias_ref[:, cols]
            zb_ref[rows, cols] = (jax.nn.gelu(su_ref[rows, cols].astype(F32)) * mixed).astype(BF16)


def _spatial_gating(h, ln_g, ln_b, w_s, b_s, *, row0, nrows, seqlen, su_off, width, emit_v,
                    zb_into):
    m = h.shape[0]
    c = min(SGU_CHUNK, seqlen)
    rep = SGU_CHUNK // c
    gdim = width // SGU_GROUPS
    w_blk = jnp.tile(w_s[:, :c, :c], (1, rep, rep))
    bias = jnp.repeat(jnp.tile(b_s[:, :c], (1, rep)).T, gdim, axis=1)
    tm = _largest_divisor(math.gcd(nrows, m), 256, SGU_CHUNK)
    rb0 = row0 // tm
    su_blk = su_off // width
    assert row0 % tm == 0
    nvalid = nrows // tm
    if zb_into is None:
        assert row0 == 0 and not emit_v
        donated, steps, fill = [], m // tm, nvalid
    else:
        assert zb_into.shape == (m, width)
        donated, steps, fill = [zb_into], nvalid, None
    last = rb0 + nvalid - 1
    vec = pl.BlockSpec((1, width), lambda i: (0, 0))
    out_specs = [pl.BlockSpec((tm, width), lambda i: (rb0 + i, 0))]
    out_shape = [jax.ShapeDtypeStruct((m, width), BF16)]
    if emit_v:
        out_specs.append(pl.BlockSpec((tm, width), lambda i: (i, 0)))
        out_shape.append(jax.ShapeDtypeStruct((nrows, width), F32))
    est = 2 * 2 * tm * width * 4 + 2 * tm * width * (2 + 4) + 8 * tm * width * 4
    res = pl.pallas_call(
        _drop_leading(len(donated), functools.partial(_sgu_body, nvalid=fill, c=c, tm=tm,
                                                      gdim=gdim, emit_v=emit_v)),
        grid=(steps,),
        in_specs=[pl.BlockSpec(memory_space=pltpu.HBM)] * len(donated)
        + [pl.BlockSpec((tm, width), lambda i: (jnp.minimum(rb0 + i, last), su_blk)),
           pl.BlockSpec((tm, width), lambda i: (jnp.minimum(rb0 + i, last), su_blk + 1)),
           vec, vec,
           pl.BlockSpec((SGU_GROUPS, SGU_CHUNK, SGU_CHUNK), lambda i: (0, 0, 0)),
           pl.BlockSpec((SGU_CHUNK, width), lambda i: (0, 0))],
        out_specs=out_specs,
        out_shape=out_shape,
        input_output_aliases={0: 0} if donated else {},
        compiler_params=pltpu.CompilerParams(
            dimension_semantics=("arbitrary",), vmem_limit_bytes=_vmem_limit(est)),
        name="spatial_gating",
    )(*donated, h, h, ln_g.reshape(1, width), ln_b.reshape(1, width), w_blk, bias)
    return res if emit_v else (res[0], None)


def _rows(ref, slab, start, n, stride):
    if n == 1:
        return ref[slab, start:start + 1, :]
    return ref[slab, pl.ds(start, n, stride=stride), :]


def _set_rows(ref, slab, start, n, stride, value):
    if n == 1:
        ref[slab, start:start + 1, :] = value
    else:
        ref[slab, pl.ds(start, n, stride=stride), :] = value


def _upgate_body(x_ref, wg_ref, wv_ref, cw_ref, cb_ref, st_ref, z_ref, tail_ref,
                 wbf_ref, buf_ref, conv_ref, val_ref, *, tm, tn, starts, ends):
    i = pl.program_id(1)
    slabs = tn // LANES

    @pl.when(i == 0)
    def _():
        wbf_ref[:, :tn] = wg_ref[...].astype(BF16)
        wbf_ref[:, tn:] = wv_ref[...].astype(BF16)
        buf_ref[:, 0:8, :] = jnp.zeros((slabs, 8, LANES), F32)

    @pl.when(i > 0)
    def _():
        buf_ref[:, 0:8, :] = buf_ref[:, tm:tm + 8, :]

    gv = jnp.dot(x_ref[...], wbf_ref[...], preferred_element_type=F32)
    val_ref[...] = gv[:, tn:]
    for s in range(slabs):
        lanes = slice(s * LANES, (s + 1) * LANES)
        buf_ref[s, 8:tm + 8, :] = gv[:, lanes]
        conv_ref[s] = (cb_ref[:, lanes] + buf_ref[s, 6:tm + 6, :] * cw_ref[0:1, lanes]
                       + buf_ref[s, 7:tm + 7, :] * cw_ref[1:2, lanes]
                       + gv[:, lanes] * cw_ref[2:3, lanes])

    for tile, off, n, stride, slot in starts:
        @pl.when(i == tile)
        def _(off=off, n=n, stride=stride, slot=slot):
            for s in range(slabs):
                lanes = slice(s * LANES, (s + 1) * LANES)
                w0, w1, w2 = cw_ref[0:1, lanes], cw_ref[1:2, lanes], cw_ref[2:3, lanes]
                cb = cb_ref[:, lanes]
                p0 = _rows(buf_ref, s, 8 + off, n, stride)
                p1 = _rows(buf_ref, s, 8 + off + 1, n, stride)
                if slot is None:
                    c0 = cb + p0 * w2
                    c1 = cb + p0 * w1 + p1 * w2
                else:
                    s0 = st_ref[0, slot:slot + n, lanes]
                    s1 = st_ref[1, slot:slot + n, lanes]
                    c0 = cb + s0 * w0 + s1 * w1 + p0 * w2
                    c1 = cb + s1 * w0 + p0 * w1 + p1 * w2
                _set_rows(conv_ref, s, off, n, stride, c0)
                _set_rows(conv_ref, s, off + 1, n, stride, c1)

    for tile, off, n, stride, slot in ends:
        @pl.when(i == tile)
        def _(off=off, n=n, stride=stride, slot=slot):
            for s in range(slabs):
                lanes = slice(s * LANES, (s + 1) * LANES)
                tail_ref[0, slot:slot + n, lanes] = _rows(buf_ref, s, 8 + off - 1, n, stride)
                tail_ref[1, slot:slot + n, lanes] = _rows(buf_ref, s, 8 + off, n, stride)

    for s in range(slabs):
        lanes = slice(s * LANES, (s + 1) * LANES)
        z_ref[:, lanes] = (jax.nn.gelu(conv_ref[s]) * val_ref[:, lanes]).astype(BF16)


def _seq_groups(seq_rows, seqlen, tm, within):
    groups = []
    for idx, r0 in enumerate(seq_rows):
        row = r0 + within
        tile, off = divmod(row, tm)
        if groups and groups[-1][0] == tile and seq_rows[idx - 1] + seqlen == r0:
            t, o, n, st, first = groups[-1]
            groups[-1] = (t, o, n + 1, st, first)
        else:
            groups.append((tile, off, 1, seqlen, idx))
    return groups


def _up_gate(xb, w_up, layer, conv_w, conv_b, conv_s0_s, *, bp, lp, bs, ls, tm, tn):
    m, k = xb.shape
    dff = w_up.shape[2] // 2
    vblk = dff // tn
    nseq = bp + bs
    prompt_rows = [b * lp for b in range(bp)]
    sample_rows = [bp * lp + b * ls for b in range(bs)]
    starts, ends = [], []
    for rows, sl, has_state, slot0 in ((sample_rows, ls, True, 0), (prompt_rows, lp, False, bs)):
        for t, o, n, st, first in _seq_groups(rows, sl, tm, 0):
            assert o + (n - 1) * st + 1 < tm, "a sequence's first two rows must share a row tile"
            starts.append((t, o, n, st, first if has_state else None))
        for t, o, n, st, first in _seq_groups(rows, sl, tm, sl - 1):
            assert o >= 1, "a sequence's last two rows must share a row tile"
            ends.append((t, o, n, st, slot0 + first))
    st_in = jnp.transpose(conv_s0_s, (1, 0, 2))
    est = (2 * tm * k * 2 + 2 * 2 * k * tn * 4 + 2 * k * tn * 2 + 2 * (tm + 8) * tn * 4
           + 2 * tm * tn * 2 + 2 * 2 * (bs + nseq) * tn * 4 + 6 * tm * tn * 4)
    return pl.pallas_call(
        functools.partial(_upgate_body, tm=tm, tn=tn, starts=tuple(starts), ends=tuple(ends)),
        grid=(dff // tn, m // tm),
        in_specs=[pl.BlockSpec((tm, k), lambda j, i: (i, 0)),
                  pl.BlockSpec((None, k, tn), lambda j, i: (layer, 0, j)),
                  pl.BlockSpec((None, k, tn), lambda j, i: (layer, 0, vblk + j)),
                  pl.BlockSpec((CONV_W, tn), lambda j, i: (0, j)),
                  pl.BlockSpec((1, tn), lambda j, i: (0, j)),
                  pl.BlockSpec((CONV_W - 1, bs, tn), lambda j, i: (0, 0, j))],
        out_specs=[pl.BlockSpec((tm, tn), lambda j, i: (i, j)),
                   pl.BlockSpec((CONV_W - 1, nseq, tn), lambda j, i: (0, 0, j))],
        out_shape=[jax.ShapeDtypeStruct((m, dff), BF16),
                   jax.ShapeDtypeStruct((CONV_W - 1, nseq, dff), F32)],
        scratch_shapes=[pltpu.VMEM((k, 2 * tn), BF16),
                        pltpu.VMEM((tn // LANES, tm + 8, LANES), F32),
                        pltpu.VMEM((tn // LANES, tm, LANES), F32), pltpu.VMEM((tm, tn), F32)],
        compiler_params=pltpu.CompilerParams(
            dimension_semantics=("arbitrary", "arbitrary"), vmem_limit_bytes=_vmem_limit(est)),
        name="up_gate",
    )(xb, w_up, w_up, conv_w, conv_b.reshape(1, dff), st_in)


def _layer(layer, depth, x_res, xb, state_ret, conv_s0_s, ret_bufs, big, small, *, bp, lp, bs,
           ls, alpha, last):
    w_in, w_a, w_b, w_o, w_up, w_down = big
    ln1_g, ln1_b, sgu_ln_g, sgu_ln_b, sgu_w, sgu_b, conv_w, conv_b, ln2_g, ln2_b = small
    m, d = xb.shape
    mp = bp * lp
    ms = bs * ls
    width = d // 2
    dff = w_down.shape[1]
    su_off = 2 * RET_QK + 2 * RET_V
    ga_off = su_off + 2 * width
    gb_off = ga_off + d
    tm_big = _largest_divisor(m, 1088, 16)

    h = _matmul(xb, w_in, layer, tm=tm_big, tn=1024, out_dtype=BF16, name="mm_in",
                stream_weights=True)

    zero_ret = jnp.zeros((1, bp, RET_HEADS, RET_DK, RET_DV), F32)
    za, ret_p = _retention(h, zero_ret, 0, row0=0, nseq=bp, seqlen=lp, pos0=0.0, hb=RET_HEADS,
                           za_into=jnp.zeros((m, RET_V), BF16), st_into=ret_bufs[0],
                           st_layer=layer, depth=depth)
    za, ret_s = _retention(h, state_ret, layer, row0=mp, nseq=bs, seqlen=ls,
                           pos0=float(PAST_LEN), hb=RET_HEADS,
                           za_into=za, st_into=ret_bufs[1], st_layer=layer, depth=depth)
    zb, _ = _spatial_gating(h, sgu_ln_g, sgu_ln_b, sgu_w, sgu_b, row0=0, nrows=mp, seqlen=lp,
                            su_off=su_off, width=width, emit_v=False, zb_into=None)
    zb, sv_s = _spatial_gating(h, sgu_ln_g, sgu_ln_b, sgu_w, sgu_b, row0=mp, nrows=ms, seqlen=ls,
                               su_off=su_off, width=width, emit_v=True, zb_into=zb)

    merged = _merge(za, zb, h, w_a, w_b, layer, ga_off=ga_off, gb_off=gb_off, tm=tm_big, tn=512)
    t1 = _matmul(merged, w_o, layer, tm=tm_big, tn=512, out_dtype=F32, name="mm_o",
                 residual=x_res, res_scale=alpha)
    tm_ln = _largest_divisor(math.gcd(mp, ms), 512, 8)
    x1b, stats1 = _layer_norm(t1, ln1_g, ln1_b, row0=0, nrows=m, tm=tm_ln, emit_f32=False)

    z, tails = _up_gate(x1b, w_up, layer, conv_w, conv_b, conv_s0_s, bp=bp, lp=lp, bs=bs, ls=ls,
                        tm=tm_big, tn=_largest_divisor(dff, 256, LANES))
    t2 = _matmul(z, w_down, layer, tm=_largest_divisor(m, 544, 16), tn=512, out_dtype=F32,
                 name="mm_down", stream_weights=True,
                 residual=("normed", t1, stats1, ln1_g, ln1_b), res_scale=alpha)
    if last:
        (y_p,) = _layer_norm(t2, ln2_g, ln2_b, row0=0, nrows=mp, tm=tm_ln, emit_f32=True)
        (y_s,) = _layer_norm(t2, ln2_g, ln2_b, row0=mp, nrows=ms, tm=tm_ln, emit_f32=True)
        x2_res, x2b = (y_p, y_s), None
    else:
        x2b, stats2 = _layer_norm(t2, ln2_g, ln2_b, row0=0, nrows=m, tm=tm_ln, emit_f32=False)
        x2_res = ("normed", t2, stats2, ln2_g, ln2_b)

    conv_s = jnp.transpose(tails[:, :bs], (1, 0, 2))
    conv_p = jnp.transpose(tails[:, bs:], (1, 0, 2))
    return x2_res, x2b, (ret_p, ret_s), conv_p, conv_s, sv_s.reshape(bs, ls, width)


def kernel(x_prompt, x_sample, state_ret, state_conv, w_in, w_a, w_b, w_o, ln1_g, ln1_b,
           sgu_ln_g, sgu_ln_b, sgu_w, sgu_b, w_up, conv_w, conv_b, w_down, ln2_g, ln2_b):
    bp, lp, d = x_prompt.shape
    bs, ls, _ = x_sample.shape
    depth = w_in.shape[0]
    alpha = float((2 * depth) ** 0.25)
    xp, xs = x_prompt.reshape(bp * lp, d), x_sample.reshape(bs * ls, d)
    xb = _concat_cast(xp, xs, tm=_largest_divisor(math.gcd(bp * lp, bs * ls), 512, 16), dtype=BF16)
    x = ("split", xp, xs)
    big = (w_in, w_a, w_b, w_o, w_up, w_down)
    ret_bufs = (None, None)
    conv_p, conv_s, sgu_v = [], [], []
    for l in range(depth):
        small = (ln1_g[l], ln1_b[l], sgu_ln_g[l], sgu_ln_b[l], sgu_w[l], sgu_b[l],
                 conv_w[l], conv_b[l], ln2_g[l], ln2_b[l])
        x, xb, ret_bufs, cp, cs, sv = _layer(
            l, depth, x, xb, state_ret, state_conv[l], ret_bufs, big, small,
            bp=bp, lp=lp, bs=bs, ls=ls, alpha=alpha, last=l == depth - 1)
        conv_p.append(cp)
        conv_s.append(cs)
        sgu_v.append(sv)
    y_p, y_s = x
    return (y_p.reshape(bp, lp, d), y_s.reshape(bs, ls, d), ret_bufs[0], jnp.stack(conv_p),
            ret_bufs[1], jnp.stack(conv_s), jnp.stack(sgu_v))
```

```python
import functools
import math

import jax
import jax.numpy as jnp
from jax import lax
from jax.experimental import pallas as pl
from jax.experimental.pallas import tpu as pltpu

F32 = jnp.float32
BF16 = jnp.bfloat16

RET_HEADS = 8
RET_DK = 256
RET_DV = 256
RET_QK = RET_HEADS * RET_DK
RET_V = RET_HEADS * RET_DV
SGU_GROUPS = 8
SGU_CHUNK = 128
CONV_W = 3
PAST_LEN = 1024
ROPE_BASE = 10000.0
LN_EPS = 1e-5
GN_EPS = 1e-6

V7X_VMEM_BYTES = 64 * 1024 * 1024
LANES = 128
RET_BLOCK = 256


def _vmem_limit(estimate_bytes):
    return int(min(V7X_VMEM_BYTES - (4 << 20), max(estimate_bytes * 5 // 4 + (2 << 20), 16 << 20)))


def _largest_divisor(n, limit, multiple):
    best = None
    d = multiple
    while d <= min(n, limit):
        if n % d == 0:
            best = d
        d += multiple
    assert best is not None, (n, limit, multiple)
    return best


def _mm_body(x_ref, w_ref, *rest, n_res, res_fn, res_scale, stream):
    o_ref, wbf_ref = rest[n_res], rest[n_res + 1]
    if stream is None:
        @pl.when(pl.program_id(1) == 0)
        def _():
            wbf_ref[...] = w_ref[...].astype(BF16)

        w_tile = wbf_ref[...]
    else:
        w_tile, prefetch_next = _streamed_weight_tile(
            w_ref, wbf_ref, rest[n_res + 2], rest[n_res + 3], **stream)

    acc = jnp.dot(x_ref[...], w_tile, preferred_element_type=F32)
    if n_res:
        acc = res_scale * res_fn(*rest[:n_res]) + acc
    o_ref[...] = acc.astype(o_ref.dtype)
    if stream is not None:
        prefetch_next()


def _streamed_weight_tile(w_hbm, wbf_ref, stage_ref, sem_ref, *, layer, kc, tn, nj, ni):
    j = pl.program_id(0)
    i = pl.program_id(1)
    cur = j % 2

    def chunk_copy(tile, c, slot):
        return pltpu.make_async_copy(
            w_hbm.at[layer, pl.ds(c * kc, kc), pl.ds(tile * tn, tn)],
            stage_ref.at[slot], sem_ref.at[slot])

    def land(tile, c, slot, wslot):
        chunk_copy(tile, c, slot).wait()
        row0 = c * kc if isinstance(c, int) else pl.multiple_of(c * kc, kc)
        wbf_ref[wslot, pl.ds(row0, kc), :] = stage_ref[slot].astype(BF16)

    @pl.when((j == 0) & (i == 0))
    def _():
        chunk_copy(0, 0, 0).start()
        for c in range(ni):
            if c + 1 < ni:
                chunk_copy(0, c + 1, (c + 1) % 2).start()
            land(0, c, c % 2, 0)

    @pl.when((j > 0) & (i == 0))
    def _():
        land(j, ni - 1, (ni - 1) % 2, cur)

    def prefetch_next():
        @pl.when(j + 1 < nj)
        def _():
            chunk_copy(j + 1, i, i % 2).start()

            @pl.when(i > 0)
            def _():
                land(j + 1, i - 1, (i - 1) % 2, 1 - cur)

    return wbf_ref[cur], prefetch_next


def _res_normed(t_ref, stats_ref, g_ref, b_ref):
    reps = t_ref.shape[1] // LANES
    mu = jnp.concatenate([stats_ref[:, :LANES]] * reps, axis=1)
    rstd = jnp.concatenate([stats_ref[:, LANES:]] * reps, axis=1)
    return (t_ref[...] - mu) * rstd * g_ref[...] + b_ref[...]


def _res_split(p_ref, s_ref, *, tile, off, axis):
    p = p_ref[...]
    n = s_ref.shape[0]
    parts = [p[:off], s_ref[...], p[off + n:]]
    mixed = jnp.concatenate([q for q in parts if q.shape[0]], axis=0)
    return jnp.where(pl.program_id(axis) == tile, mixed, p)


def _cast_body(p_ref, s_ref, o_ref, *, tile, off):
    o_ref[...] = _res_split(p_ref, s_ref, tile=tile, off=off, axis=0).astype(o_ref.dtype)


def _concat_cast(p, sec, *, tm, dtype):
    mp, d = p.shape
    ms = sec.shape[0]
    tile, off = divmod(mp, tm)
    assert off + ms <= tm and (mp + ms) % tm == 0
    last_p = (mp - 1) // tm
    est = 2 * 2 * tm * d * 4 + 2 * tm * d * jnp.dtype(dtype).itemsize + 2 * tm * d * 4
    return pl.pallas_call(
        functools.partial(_cast_body, tile=tile, off=off),
        grid=((mp + ms) // tm,),
        in_specs=[pl.BlockSpec((tm, d), lambda i: (jnp.minimum(i, last_p), 0)),
                  pl.BlockSpec((ms, d), lambda i: (0, 0))],
        out_specs=pl.BlockSpec((tm, d), lambda i: (i, 0)),
        out_shape=jax.ShapeDtypeStruct((mp + ms, d), dtype),
        compiler_params=pltpu.CompilerParams(
            dimension_semantics=("arbitrary",), vmem_limit_bytes=_vmem_limit(est)),
        name="concat_cast",
    )(p, sec)


def _matmul(x, w, layer, *, tm, tn, out_dtype, name, stream_weights=False, residual=None,
            res_scale=None):
    m, k = x.shape
    n = w.shape[2]
    nj, ni = n // tn, m // tm
    est = 2 * tm * k * 2 + 2 * tm * tn * jnp.dtype(out_dtype).itemsize
    if stream_weights:
        kc = k // ni
        assert k == kc * ni and kc % 16 == 0 and ni % 2 == 0, (k, ni)
        stream = dict(layer=layer, kc=kc, tn=tn, nj=nj, ni=ni)
        w_spec = pl.BlockSpec(memory_space=pltpu.HBM)
        scratch = [pltpu.VMEM((2, k, tn), BF16), pltpu.VMEM((2, kc, tn), F32),
                   pltpu.SemaphoreType.DMA((2,))]
        est += 2 * k * tn * 2 + 2 * kc * tn * 4 + 2 * kc * tn * 4
    else:
        stream = None
        w_spec = pl.BlockSpec((None, k, tn), lambda j, i: (layer, 0, j))
        scratch = [pltpu.VMEM((k, tn), BF16)]
        est += 2 * k * tn * 4 + k * tn * 2
    in_specs = [pl.BlockSpec((tm, k), lambda j, i: (i, 0)), w_spec]
    args = [x, w]
    res_fn = None
    tile_mn = pl.BlockSpec((tm, tn), lambda j, i: (i, j))
    if residual is not None:
        kind = residual[0]
        est += 2 * tm * tn * 4
        if kind == "normed":
            _, t, stats, g, b = residual
            res_fn = _res_normed
            vec = pl.BlockSpec((1, tn), lambda j, i: (0, j))
            in_specs += [tile_mn, pl.BlockSpec((tm, 2 * LANES), lambda j, i: (i, 0)), vec, vec]
            args += [t, stats, g.reshape(1, n), b.reshape(1, n)]
            est += 2 * tm * 2 * LANES * 4
        else:
            _, p, sec = residual
            mp, ms = p.shape[0], sec.shape[0]
            tile, off = divmod(mp, tm)
            assert kind == "split" and off + ms <= tm and mp + ms == m
            last_p = (mp - 1) // tm
            res_fn = functools.partial(_res_split, tile=tile, off=off, axis=1)
            in_specs += [pl.BlockSpec((tm, tn), lambda j, i: (jnp.minimum(i, last_p), j)),
                         pl.BlockSpec((ms, tn), lambda j, i: (0, j))]
            args += [p, sec]
            est += 2 * ms * tn * 4
    return pl.pallas_call(
        functools.partial(_mm_body, n_res=len(args) - 2, res_fn=res_fn, res_scale=res_scale,
                          stream=stream),
        grid=(nj, ni),
        in_specs=in_specs,
        out_specs=tile_mn,
        out_shape=jax.ShapeDtypeStruct((m, n), out_dtype),
        scratch_shapes=scratch,
        compiler_params=pltpu.CompilerParams(
            dimension_semantics=("arbitrary", "arbitrary"), vmem_limit_bytes=_vmem_limit(est)),
        name=name,
    )(*args)


def _merge_body(za_ref, zb_ref, ga_ref, gb_ref, wa_ref, wb_ref, o_ref, wabf_ref, wbbf_ref):
    @pl.when(pl.program_id(1) == 0)
    def _():
        wabf_ref[...] = wa_ref[...].astype(BF16)
        wbbf_ref[...] = wb_ref[...].astype(BF16)

    a = jnp.dot(za_ref[...], wabf_ref[...], preferred_element_type=F32)
    b = jnp.dot(zb_ref[...], wbbf_ref[...], preferred_element_type=F32)
    ga = jax.nn.sigmoid(ga_ref[...].astype(F32))
    gb = jax.nn.sigmoid(gb_ref[...].astype(F32))
    o_ref[...] = (ga * a + gb * b).astype(o_ref.dtype)


def _merge(za, zb, h, w_a, w_b, layer, *, ga_off, gb_off, tm, tn):
    m, ka = za.shape
    kb = zb.shape[1]
    n = w_a.shape[2]
    ga_blk, gb_blk = ga_off // tn, gb_off // tn
    est = (2 * tm * (ka + kb) * 2 + 2 * (ka + kb) * tn * 4 + (ka + kb) * tn * 2
           + 2 * 2 * tm * tn * h.dtype.itemsize + 2 * tm * tn * 2 + 4 * tm * tn * 4)
    return pl.pallas_call(
        _merge_body,
        grid=(n // tn, m // tm),
        in_specs=[pl.BlockSpec((tm, ka), lambda j, i: (i, 0)),
                  pl.BlockSpec((tm, kb), lambda j, i: (i, 0)),
                  pl.BlockSpec((tm, tn), lambda j, i: (i, ga_blk + j)),
                  pl.BlockSpec((tm, tn), lambda j, i: (i, gb_blk + j)),
                  pl.BlockSpec((None, ka, tn), lambda j, i: (layer, 0, j)),
                  pl.BlockSpec((None, kb, tn), lambda j, i: (layer, 0, j))],
        out_specs=pl.BlockSpec((tm, tn), lambda j, i: (i, j)),
        out_shape=jax.ShapeDtypeStruct((m, n), BF16),
        scratch_shapes=[pltpu.VMEM((ka, tn), BF16), pltpu.VMEM((kb, tn), BF16)],
        compiler_params=pltpu.CompilerParams(
            dimension_semantics=("arbitrary", "arbitrary"), vmem_limit_bytes=_vmem_limit(est)),
        name="merge_mm",
    )(za, zb, h, h, w_a, w_b)


def _ln_body(t_ref, g_ref, b_ref, *out_refs, emit_f32):
    t = t_ref[...]
    mu = jnp.mean(t, axis=-1, keepdims=True)
    d = t - mu
    var = jnp.mean(d * d, axis=-1, keepdims=True)
    rstd = lax.rsqrt(var + LN_EPS)
    out = d * rstd * g_ref[...] + b_ref[...]
    if emit_f32:
        out_refs[0][...] = out
    else:
        obf_ref, stats_ref = out_refs
        obf_ref[...] = out.astype(BF16)
        rows = t.shape[0]
        stats_ref[:, :LANES] = jnp.broadcast_to(mu, (rows, LANES))
        stats_ref[:, LANES:] = jnp.broadcast_to(rstd, (rows, LANES))


def _layer_norm(t, g, b, *, row0, nrows, tm, emit_f32):
    d = t.shape[1]
    rb0 = row0 // tm
    assert row0 % tm == 0 and nrows % tm == 0
    est = 2 * tm * d * 4 + 2 * tm * d * 4 + 6 * tm * d * 4
    row = pl.BlockSpec((tm, d), lambda i: (i, 0))
    vec = pl.BlockSpec((1, d), lambda i: (0, 0))
    if emit_f32:
        out_specs = [row]
        out_shape = [jax.ShapeDtypeStruct((nrows, d), F32)]
    else:
        out_specs = [row, pl.BlockSpec((tm, 2 * LANES), lambda i: (i, 0))]
        out_shape = [jax.ShapeDtypeStruct((nrows, d), BF16),
                     jax.ShapeDtypeStruct((nrows, 2 * LANES), F32)]
    return pl.pallas_call(
        functools.partial(_ln_body, emit_f32=emit_f32),
        grid=(nrows // tm,),
        in_specs=[pl.BlockSpec((tm, d), lambda i: (rb0 + i, 0)), vec, vec],
        out_specs=out_specs,
        out_shape=out_shape,
        compiler_params=pltpu.CompilerParams(
            dimension_semantics=("arbitrary",), vmem_limit_bytes=_vmem_limit(est)),
        name="layer_norm",
    )(t, g.reshape(1, d), b.reshape(1, d))


def _ret_body(q_ref, k_ref, v_ref, g_ref, cos_ref, sin_ref, lg_ref, s0_ref, za_ref, sout_ref,
              s_ref, d_ref, qd_ref, kd_ref, *, c, hb, fill_slot):
    b = pl.program_id(1)
    ci = pl.program_id(2)

    @pl.when((b == 0) & (ci == 0))
    def _():
        ri = lax.broadcasted_iota(jnp.int32, (c, c), 0)
        cj = lax.broadcasted_iota(jnp.int32, (c, c), 1)
        diff = (ri - cj).astype(F32)
        r = lax.broadcasted_iota(jnp.int32, (c, RET_DK), 0).astype(F32)
        for hh in range(hb):
            lg = lg_ref[hh]
            d_ref[hh] = jnp.where(diff >= 0, jnp.exp(jnp.maximum(diff, 0.0) * lg[:, :c]), 0.0)
            qd_ref[hh] = jnp.exp((r + 1.0) * lg)
            kd_ref[hh] = jnp.exp((c - 1.0 - r) * lg)

    @pl.when(ci == 0)
    def _():
        s_ref[...] = s0_ref[...]

    cos = cos_ref[...]
    sin = sin_ref[...]
    half = RET_DK // 2

    def rope(t):
        t1, t2 = t[:, :half], t[:, half:]
        return jnp.concatenate([t1 * cos - t2 * sin, t1 * sin + t2 * cos], axis=-1)

    for hh in range(hb):
        qc = slice(hh * RET_DK, (hh + 1) * RET_DK)
        vc = slice(hh * RET_DV, (hh + 1) * RET_DV)
        q = rope(q_ref[:, qc].astype(F32))
        k = rope(k_ref[:, qc].astype(F32)) * (RET_DK ** -0.5)
        vb = v_ref[:, vc].astype(BF16)
        s = s_ref[hh]
        scores = lax.dot_general(q.astype(BF16), k.astype(BF16), (((1,), (1,)), ((), ())),
                                 preferred_element_type=F32) * d_ref[hh]
        o = (jnp.dot(scores.astype(BF16), vb, preferred_element_type=F32)
             + jnp.dot((q * qd_ref[hh]).astype(BF16), s.astype(BF16), preferred_element_type=F32))
        kv = lax.dot_general((k * kd_ref[hh]).astype(BF16), vb, (((0,), (0,)), ((), ())),
                             preferred_element_type=F32)
        s_ref[hh] = s * jnp.exp(float(c) * lg_ref[hh]) + kv

        mu = jnp.mean(o, axis=-1, keepdims=True)
        od = o - mu
        var = jnp.mean(od * od, axis=-1, keepdims=True)
        on = od * lax.rsqrt(var + GN_EPS)
        za_ref[:, vc] = (jax.nn.silu(g_ref[:, vc].astype(F32)) * on).astype(BF16)

    @pl.when(ci == pl.num_programs(2) - 1)
    def _():
        if fill_slot is None:
            sout_ref[...] = s_ref[...]
        else:
            for slot in range(sout_ref.shape[0]):
                sout_ref[slot] = s_ref[...] if slot == fill_slot else jnp.zeros(s_ref.shape, F32)


def _drop_leading(n, body):
    def wrapped(*refs):
        return body(*refs[n:])
    return wrapped


def _retention(h, s0, s0_layer, *, row0, nseq, seqlen, pos0, hb, za_into, st_into, st_layer,
               depth):
    m = h.shape[0]
    c = min(RET_BLOCK, seqlen)
    nc = seqlen // c
    rb0 = row0 // c
    hq = RET_HEADS // hb
    half = RET_DK // 2
    pos = pos0 + jnp.arange(seqlen, dtype=F32)
    freqs = ROPE_BASE ** (-jnp.arange(half, dtype=F32) / half)
    ang = pos[:, None] * freqs[None, :]
    cos, sin = jnp.cos(ang), jnp.sin(ang)
    log_gamma = jnp.log(1.0 - 2.0 ** (-5.0 - jnp.arange(RET_HEADS, dtype=F32)))
    lg = jnp.broadcast_to(log_gamma[:, None, None], (RET_HEADS, 1, RET_DK))

    def hcol(off):
        return pl.BlockSpec((c, hb * RET_DK), lambda hh, b, ci: (rb0 + b * nc + ci, off + hh))

    tab = pl.BlockSpec((c, half), lambda hh, b, ci: (ci, 0))

    def state(layer):
        return pl.BlockSpec((None, None, hb, RET_DK, RET_DV),
                            lambda hh, b, ci: (layer, b, hh, 0, 0))

    assert za_into.shape == (m, RET_V)
    st_shape = (depth, nseq, RET_HEADS, RET_DK, RET_DV)
    if st_into is None:
        donated, aliases, fill_slot = [za_into], {0: 0}, st_layer
        st_spec = pl.BlockSpec((depth, None, hb, RET_DK, RET_DV),
                               lambda hh, b, ci: (0, b, hh, 0, 0))
    else:
        assert st_into.shape == st_shape
        donated, aliases, fill_slot = [za_into, st_into], {0: 0, 1: 1}, None
        st_spec = state(st_layer)
    est = (2 * 4 * c * hb * RET_DK * 2 + 2 * (1 + depth) * hb * RET_DK * RET_DV * 4
           + hb * RET_DK * RET_DV * 4 + hb * (c * c + 2 * c * RET_DK) * 4
           + 2 * c * hb * RET_DV * 2 + 16 * c * RET_DK * 4)
    return pl.pallas_call(
        _drop_leading(len(donated),
                      functools.partial(_ret_body, c=c, hb=hb, fill_slot=fill_slot)),
        grid=(hq, nseq, nc),
        in_specs=[pl.BlockSpec(memory_space=pltpu.HBM)] * len(donated)
        + [hcol(0), hcol(hq), hcol(2 * hq), hcol(3 * hq), tab, tab,
           pl.BlockSpec((hb, 1, RET_DK), lambda hh, b, ci: (hh, 0, 0)), state(s0_layer)],
        out_specs=[pl.BlockSpec((c, hb * RET_DV), lambda hh, b, ci: (rb0 + b * nc + ci, hh)),
                   st_spec],
        out_shape=[jax.ShapeDtypeStruct((m, RET_V), BF16), jax.ShapeDtypeStruct(st_shape, F32)],
        input_output_aliases=aliases,
        scratch_shapes=[pltpu.VMEM((hb, RET_DK, RET_DV), F32), pltpu.VMEM((hb, c, c), F32),
                        pltpu.VMEM((hb, c, RET_DK), F32), pltpu.VMEM((hb, c, RET_DK), F32)],
        compiler_params=pltpu.CompilerParams(
            dimension_semantics=("arbitrary", "arbitrary", "arbitrary"),
            vmem_limit_bytes=_vmem_limit(est)),
        name="retention",
    )(*donated, h, h, h, h, cos, sin, lg, s0)


def _sgu_body(su_ref, sv_ref, lng_ref, lnb_ref, w_ref, bias_ref, zb_ref, *rest, nvalid, **kw):
    if nvalid is None:
        _sgu_tile(su_ref, sv_ref, lng_ref, lnb_ref, w_ref, bias_ref, zb_ref, *rest, **kw)
        return

    @pl.when(pl.program_id(0) < nvalid)
    def _():
        _sgu_tile(su_ref, sv_ref, lng_ref, lnb_ref, w_ref, bias_ref, zb_ref, *rest, **kw)

    @pl.when(pl.program_id(0) >= nvalid)
    def _():
        zb_ref[...] = jnp.zeros(zb_ref.shape, zb_ref.dtype)


def _sgu_tile(su_ref, sv_ref, lng_ref, lnb_ref, w_ref, bias_ref, zb_ref, *rest, c, tm, gdim, emit_v):
    sv = jax.nn.gelu(sv_ref[...].astype(F32))
    mu = jnp.mean(sv, axis=-1, keepdims=True)
    d = sv - mu
    var = jnp.mean(d * d, axis=-1, keepdims=True)
    svn = d * lax.rsqrt(var + LN_EPS) * lng_ref[...] + lnb_ref[...]
    if emit_v:
        rest[0][...] = svn
    svb = svn.astype(BF16)
    ri = lax.broadcasted_iota(jnp.int32, (SGU_CHUNK, SGU_CHUNK), 0)
    cj = lax.broadcasted_iota(jnp.int32, (SGU_CHUNK, SGU_CHUNK), 1)
    mask = (ri >= cj) & ((ri // c) == (cj // c))
    for g in range(SGU_GROUPS):
        wm = jnp.where(mask, w_ref[g], 0.0).astype(BF16)
        cols = slice(g * gdim, (g + 1) * gdim)
        for r in range(tm // SGU_CHUNK):
            rows = slice(r * SGU_CHUNK, (r + 1) * SGU_CHUNK)
            mixed = jnp.dot(wm, svb[rows, cols], preferred_element_type=F32) + bias_ref[:, cols]
            zb_ref[rows, cols] = (jax.nn.gelu(su_ref[rows, cols].astype(F32)) * mixed).astype(BF16)


def _spatial_gating(h, ln_g, ln_b, w_s, b_s, *, row0, nrows, seqlen, su_off, width, emit_v,
                    zb_into):
    m = h.shape[0]
    c = min(SGU_CHUNK, seqlen)
    rep = SGU_CHUNK // c
    gdim = width // SGU_GROUPS
    w_blk = jnp.tile(w_s[:, :c, :c], (1, rep, rep))
    bias = jnp.repeat(jnp.tile(b_s[:, :c], (1, rep)).T, gdim, axis=1)
    tm = _largest_divisor(math.gcd(nrows, m), 256, SGU_CHUNK)
    rb0 = row0 // tm
    su_blk = su_off // width
    assert row0 % tm == 0
    nvalid = nrows // tm
    if zb_into is None:
        assert row0 == 0 and not emit_v
        donated, steps, fill = [], m // tm, nvalid
    else:
        assert zb_into.shape == (m, width)
        donated, steps, fill = [zb_into], nvalid, None
    last = rb0 + nvalid - 1
    vec = pl.BlockSpec((1, width), lambda i: (0, 0))
    out_specs = [pl.BlockSpec((tm, width), lambda i: (rb0 + i, 0))]
    out_shape = [jax.ShapeDtypeStruct((m, width), BF16)]
    if emit_v:
        out_specs.append(pl.BlockSpec((tm, width), lambda i: (i, 0)))
        out_shape.append(jax.ShapeDtypeStruct((nrows, width), F32))
    est = 2 * 2 * tm * width * 4 + 2 * tm * width * (2 + 4) + 8 * tm * width * 4
    res = pl.pallas_call(
        _drop_leading(len(donated), functools.partial(_sgu_body, nvalid=fill, c=c, tm=tm,
                                                      gdim=gdim, emit_v=emit_v)),
        grid=(steps,),
        in_specs=[pl.BlockSpec(memory_space=pltpu.HBM)] * len(donated)
        + [pl.BlockSpec((tm, width), lambda i: (jnp.minimum(rb0 + i, last), su_blk)),
           pl.BlockSpec((tm, width), lambda i: (jnp.minimum(rb0 + i, last), su_blk + 1)),
           vec, vec,
           pl.BlockSpec((SGU_GROUPS, SGU_CHUNK, SGU_CHUNK), lambda i: (0, 0, 0)),
           pl.BlockSpec((SGU_CHUNK, width), lambda i: (0, 0))],
        out_specs=out_specs,
        out_shape=out_shape,
        input_output_aliases={0: 0} if donated else {},
        compiler_params=pltpu.CompilerParams(
            dimension_semantics=("arbitrary",), vmem_limit_bytes=_vmem_limit(est)),
        name="spatial_gating",
    )(*donated, h, h, ln_g.reshape(1, width), ln_b.reshape(1, width), w_blk, bias)
    return res if emit_v else (res[0], None)


def _rows(ref, slab, start, n, stride):
    if n == 1:
        return ref[slab, start:start + 1, :]
    return ref[slab, pl.ds(start, n, stride=stride), :]


def _set_rows(ref, slab, start, n, stride, value):
    if n == 1:
        ref[slab, start:start + 1, :] = value
    else:
        ref[slab, pl.ds(start, n, stride=stride), :] = value


def _upgate_body(x_ref, wg_ref, wv_ref, cw_ref, cb_ref, st_ref, z_ref, tail_ref,
                 wbf_ref, buf_ref, conv_ref, val_ref, *, tm, tn, starts, ends):
    i = pl.program_id(1)
    slabs = tn // LANES

    @pl.when(i == 0)
    def _():
        wbf_ref[:, :tn] = wg_ref[...].astype(BF16)
        wbf_ref[:, tn:] = wv_ref[...].astype(BF16)
        buf_ref[:, 0:8, :] = jnp.zeros((slabs, 8, LANES), F32)

    @pl.when(i > 0)
    def _():
        buf_ref[:, 0:8, :] = buf_ref[:, tm:tm + 8, :]

    gv = jnp.dot(x_ref[...], wbf_ref[...], preferred_element_type=F32)
    val_ref[...] = gv[:, tn:]
    for s in range(slabs):
        lanes = slice(s * LANES, (s + 1) * LANES)
        buf_ref[s, 8:tm + 8, :] = gv[:, lanes]
        conv_ref[s] = (cb_ref[:, lanes] + buf_ref[s, 6:tm + 6, :] * cw_ref[0:1, lanes]
                       + buf_ref[s, 7:tm + 7, :] * cw_ref[1:2, lanes]
                       + gv[:, lanes] * cw_ref[2:3, lanes])

    for tile, off, n, stride, slot in starts:
        @pl.when(i == tile)
        def _(off=off, n=n, stride=stride, slot=slot):
            for s in range(slabs):
                lanes = slice(s * LANES, (s + 1) * LANES)
                w0, w1, w2 = cw_ref[0:1, lanes], cw_ref[1:2, lanes], cw_ref[2:3, lanes]
                cb = cb_ref[:, lanes]
                p0 = _rows(buf_ref, s, 8 + off, n, stride)
                p1 = _rows(buf_ref, s, 8 + off + 1, n, stride)
                if slot is None:
                    c0 = cb + p0 * w2
                    c1 = cb + p0 * w1 + p1 * w2
                else:
                    s0 = st_ref[0, slot:slot + n, lanes]
                    s1 = st_ref[1, slot:slot + n, lanes]
                    c0 = cb + s0 * w0 + s1 * w1 + p0 * w2
                    c1 = cb + s1 * w0 + p0 * w1 + p1 * w2
                _set_rows(conv_ref, s, off, n, stride, c0)
                _set_rows(conv_ref, s, off + 1, n, stride, c1)

    for tile, off, n, stride, slot in ends:
        @pl.when(i == tile)
        def _(off=off, n=n, stride=stride, slot=slot):
            for s in range(slabs):
                lanes = slice(s * LANES, (s + 1) * LANES)
                tail_ref[0, slot:slot + n, lanes] = _rows(buf_ref, s, 8 + off - 1, n, stride)
                tail_ref[1, slot:slot + n, lanes] = _rows(buf_ref, s, 8 + off, n, stride)

    for s in range(slabs):
        lanes = slice(s * LANES, (s + 1) * LANES)
        z_ref[:, lanes] = (jax.nn.gelu(conv_ref[s]) * val_ref[:, lanes]).astype(BF16)


def _seq_groups(seq_rows, seqlen, tm, within):
    groups = []
    for idx, r0 in enumerate(seq_rows):
        row = r0 + within
        tile, off = divmod(row, tm)
        if groups and groups[-1][0] == tile and seq_rows[idx - 1] + seqlen == r0:
            t, o, n, st, first = groups[-1]
            groups[-1] = (t, o, n + 1, st, first)
        else:
            groups.append((tile, off, 1, seqlen, idx))
    return groups


def _up_gate(xb, w_up, layer, conv_w, conv_b, conv_s0_s, *, bp, lp, bs, ls, tm, tn):
    m, k = xb.shape
    dff = w_up.shape[2] // 2
    vblk = dff // tn
    nseq = bp + bs
    prompt_rows = [b * lp for b in range(bp)]
    sample_rows = [bp * lp + b * ls for b in range(bs)]
    starts, ends = [], []
    for rows, sl, has_state, slot0 in ((sample_rows, ls, True, 0), (prompt_rows, lp, False, bs)):
        for t, o, n, st, first in _seq_groups(rows, sl, tm, 0):
            assert o + (n - 1) * st + 1 < tm, "a sequence's first two rows must share a row tile"
            starts.append((t, o, n, st, first if has_state else None))
        for t, o, n, st, first in _seq_groups(rows, sl, tm, sl - 1):
            assert o >= 1, "a sequence's last two rows must share a row tile"
            ends.append((t, o, n, st, slot0 + first))
    st_in = jnp.transpose(conv_s0_s, (1, 0, 2))
    est = (2 * tm * k * 2 + 2 * 2 * k * tn * 4 + 2 * k * tn * 2 + 2 * (tm + 8) * tn * 4
           + 2 * tm * tn * 2 + 2 * 2 * (bs + nseq) * tn * 4 + 6 * tm * tn * 4)
    return pl.pallas_call(
        functools.partial(_upgate_body, tm=tm, tn=tn, starts=tuple(starts), ends=tuple(ends)),
        grid=(dff // tn, m // tm),
        in_specs=[pl.BlockSpec((tm, k), lambda j, i: (i, 0)),
                  pl.BlockSpec((None, k, tn), lambda j, i: (layer, 0, j)),
                  pl.BlockSpec((None, k, tn), lambda j, i: (layer, 0, vblk + j)),
                  pl.BlockSpec((CONV_W, tn), lambda j, i: (0, j)),
                  pl.BlockSpec((1, tn), lambda j, i: (0, j)),
                  pl.BlockSpec((CONV_W - 1, bs, tn), lambda j, i: (0, 0, j))],
        out_specs=[pl.BlockSpec((tm, tn), lambda j, i: (i, j)),
                   pl.BlockSpec((CONV_W - 1, nseq, tn), lambda j, i: (0, 0, j))],
        out_shape=[jax.ShapeDtypeStruct((m, dff), BF16),
                   jax.ShapeDtypeStruct((CONV_W - 1, nseq, dff), F32)],
        scratch_shapes=[pltpu.VMEM((k, 2 * tn), BF16),
                        pltpu.VMEM((tn // LANES, tm + 8, LANES), F32),
                        pltpu.VMEM((tn // LANES, tm, LANES), F32), pltpu.VMEM((tm, tn), F32)],
        compiler_params=pltpu.CompilerParams(
            dimension_semantics=("arbitrary", "arbitrary"), vmem_limit_bytes=_vmem_limit(est)),
        name="up_gate",
    )(xb, w_up, w_up, conv_w, conv_b.reshape(1, dff), st_in)


def _layer(layer, depth, x_res, xb, state_ret, conv_s0_s, ret_bufs, za_buf, big, small, *, bp,
           lp, bs, ls, alpha, last):
    w_in, w_a, w_b, w_o, w_up, w_down = big
    ln1_g, ln1_b, sgu_ln_g, sgu_ln_b, sgu_w, sgu_b, conv_w, conv_b, ln2_g, ln2_b = small
    m, d = xb.shape
    mp = bp * lp
    ms = bs * ls
    width = d // 2
    dff = w_down.shape[1]
    su_off = 2 * RET_QK + 2 * RET_V
    ga_off = su_off + 2 * width
    gb_off = ga_off + d
    tm_big = _largest_divisor(m, 1088, 16)

    h = _matmul(xb, w_in, layer, tm=tm_big, tn=1024, out_dtype=BF16, name="mm_in",
                stream_weights=True)

    zero_ret = jnp.zeros((1, bp, RET_HEADS, RET_DK, RET_DV), F32)
    za, ret_p = _retention(h, zero_ret, 0, row0=0, nseq=bp, seqlen=lp, pos0=0.0, hb=RET_HEADS,
                           za_into=za_buf, st_into=ret_bufs[0], st_layer=layer, depth=depth)
    za, ret_s = _retention(h, state_ret, layer, row0=mp, nseq=bs, seqlen=ls,
                           pos0=float(PAST_LEN), hb=RET_HEADS,
                           za_into=za, st_into=ret_bufs[1], st_layer=layer, depth=depth)
    zb, _ = _spatial_gating(h, sgu_ln_g, sgu_ln_b, sgu_w, sgu_b, row0=0, nrows=mp, seqlen=lp,
                            su_off=su_off, width=width, emit_v=False, zb_into=None)
    zb, sv_s = _spatial_gating(h, sgu_ln_g, sgu_ln_b, sgu_w, sgu_b, row0=mp, nrows=ms, seqlen=ls,
                               su_off=su_off, width=width, emit_v=True, zb_into=zb)

    merged = _merge(za, zb, h, w_a, w_b, layer, ga_off=ga_off, gb_off=gb_off, tm=tm_big, tn=512)
    t1 = _matmul(merged, w_o, layer, tm=tm_big, tn=512, out_dtype=F32, name="mm_o",
                 residual=x_res, res_scale=alpha)
    tm_ln = _largest_divisor(math.gcd(mp, ms), 512, 8)
    x1b, stats1 = _layer_norm(t1, ln1_g, ln1_b, row0=0, nrows=m, tm=tm_ln, emit_f32=False)

    z, tails = _up_gate(x1b, w_up, layer, conv_w, conv_b, conv_s0_s, bp=bp, lp=lp, bs=bs, ls=ls,
                        tm=tm_big, tn=_largest_divisor(dff, 256, LANES))
    t2 = _matmul(z, w_down, layer, tm=_largest_divisor(m, 544, 16), tn=512, out_dtype=F32,
                 name="mm_down", stream_weights=True,
                 residual=("normed", t1, stats1, ln1_g, ln1_b), res_scale=alpha)
    if last:
        (y_p,) = _layer_norm(t2, ln2_g, ln2_b, row0=0, nrows=mp, tm=tm_ln, emit_f32=True)
        (y_s,) = _layer_norm(t2, ln2_g, ln2_b, row0=mp, nrows=ms, tm=tm_ln, emit_f32=True)
        x2_res, x2b = (y_p, y_s), None
    else:
        x2b, stats2 = _layer_norm(t2, ln2_g, ln2_b, row0=0, nrows=m, tm=tm_ln, emit_f32=False)
        x2_res = ("normed", t2, stats2, ln2_g, ln2_b)

    conv_s = jnp.transpose(tails[:, :bs], (1, 0, 2))
    conv_p = jnp.transpose(tails[:, bs:], (1, 0, 2))
    return x2_res, x2b, (ret_p, ret_s), conv_p, conv_s, sv_s.reshape(bs, ls, width), za


def kernel(x_prompt, x_sample, state_ret, state_conv, w_in, w_a, w_b, w_o, ln1_g, ln1_b,
           sgu_ln_g, sgu_ln_b, sgu_w, sgu_b, w_up, conv_w, conv_b, w_down, ln2_g, ln2_b):
    bp, lp, d = x_prompt.shape
    bs, ls, _ = x_sample.shape
    depth = w_in.shape[0]
    alpha = float((2 * depth) ** 0.25)
    xp, xs = x_prompt.reshape(bp * lp, d), x_sample.reshape(bs * ls, d)
    xb = _concat_cast(xp, xs, tm=_largest_divisor(math.gcd(bp * lp, bs * ls), 512, 16), dtype=BF16)
    x = ("split", xp, xs)
    big = (w_in, w_a, w_b, w_o, w_up, w_down)
    ret_bufs = (None, None)
    za_buf = jnp.zeros((xb.shape[0], RET_V), BF16)
    conv_p, conv_s, sgu_v = [], [], []
    for l in range(depth):
        small = (ln1_g[l], ln1_b[l], sgu_ln_g[l], sgu_ln_b[l], sgu_w[l], sgu_b[l],
                 conv_w[l], conv_b[l], ln2_g[l], ln2_b[l])
        x, xb, ret_bufs, cp, cs, sv, za_buf = _layer(
            l, depth, x, xb, state_ret, state_conv[l], ret_bufs, za_buf, big, small,
            bp=bp, lp=lp, bs=bs, ls=ls, alpha=alpha, last=l == depth - 1)
        conv_p.append(cp)
        conv_s.append(cs)
        sgu_v.append(sv)
    y_p, y_s = x
    return (y_p.reshape(bp, lp, d), y_s.reshape(bs, ls, d), ret_bufs[0], jnp.stack(conv_p),
            ret_bufs[1], jnp.stack(conv_s), jnp.stack(sgu_v))
```

```python
import functools
import math

import jax
import jax.numpy as jnp
from jax import lax
from jax.experimental import pallas as pl
from jax.experimental.pallas import tpu as pltpu

F32 = jnp.float32
BF16 = jnp.bfloat16

RET_HEADS = 8
RET_DK = 256
RET_DV = 256
RET_QK = RET_HEADS * RET_DK
RET_V = RET_HEADS * RET_DV
SGU_GROUPS = 8
SGU_CHUNK = 128
CONV_W = 3
PAST_LEN = 1024
ROPE_BASE = 10000.0
LN_EPS = 1e-5
GN_EPS = 1e-6

V7X_VMEM_BYTES = 64 * 1024 * 1024
LANES = 128
RET_BLOCK = 256


def _vmem_limit(estimate_bytes):
    return int(min(V7X_VMEM_BYTES - (4 << 20), max(estimate_bytes * 5 // 4 + (2 << 20), 16 << 20)))


def _largest_divisor(n, limit, multiple):
    best = None
    d = multiple
    while d <= min(n, limit):
        if n % d == 0:
            best = d
        d += multiple
    assert best is not None, (n, limit, multiple)
    return best


def _mm_body(x_ref, w_ref, *rest, n_res, res_fn, res_scale, stream):
    o_ref, wbf_ref = rest[n_res], rest[n_res + 1]
    if stream is None:
        @pl.when(pl.program_id(1) == 0)
        def _():
            wbf_ref[...] = w_ref[...].astype(BF16)

        w_tile = wbf_ref[...]
    else:
        w_tile = _streamed_weight_tile(w_ref, wbf_ref, rest[n_res + 2], rest[n_res + 3], **stream)

    acc = jnp.dot(x_ref[...], w_tile, preferred_element_type=F32)
    if n_res:
        acc = res_scale * res_fn(*rest[:n_res]) + acc
    o_ref[...] = acc.astype(o_ref.dtype)


def _streamed_weight_tile(w_hbm, wbf_ref, stage_ref, sem_ref, *, layer, kc, tn, nj, ni):
    j = pl.program_id(0)
    i = pl.program_id(1)
    cur = j % 2

    def chunk_copy(tile, c, slot):
        return pltpu.make_async_copy(
            w_hbm.at[layer, pl.ds(c * kc, kc), pl.ds(tile * tn, tn)],
            stage_ref.at[slot], sem_ref.at[slot])

    def land(tile, c, slot, wslot):
        chunk_copy(tile, c, slot).wait()
        row0 = c * kc if isinstance(c, int) else pl.multiple_of(c * kc, kc)
        wbf_ref[wslot, pl.ds(row0, kc), :] = stage_ref[slot].astype(BF16)

    @pl.when((j == 0) & (i == 0))
    def _():
        chunk_copy(0, 0, 0).start()
        for c in range(ni):
            if c + 1 < ni:
                chunk_copy(0, c + 1, (c + 1) % 2).start()
            land(0, c, c % 2, 0)

    @pl.when((j > 0) & (i == 0))
    def _():
        land(j, ni - 1, (ni - 1) % 2, cur)

    @pl.when(j + 1 < nj)
    def _():
        chunk_copy(j + 1, i, i % 2).start()

        @pl.when(i > 0)
        def _():
            land(j + 1, i - 1, (i - 1) % 2, 1 - cur)

    return wbf_ref[cur]


def _res_normed(t_ref, stats_ref, g_ref, b_ref):
    reps = t_ref.shape[1] // LANES
    mu = jnp.concatenate([stats_ref[:, :LANES]] * reps, axis=1)
    rstd = jnp.concatenate([stats_ref[:, LANES:]] * reps, axis=1)
    return (t_ref[...] - mu) * rstd * g_ref[...] + b_ref[...]


def _res_split(p_ref, s_ref, *, tile, off, axis):
    p = p_ref[...]
    n = s_ref.shape[0]
    parts = [p[:off], s_ref[...], p[off + n:]]
    mixed = jnp.concatenate([q for q in parts if q.shape[0]], axis=0)
    return jnp.where(pl.program_id(axis) == tile, mixed, p)


def _cast_body(p_ref, s_ref, o_ref, *, tile, off):
    o_ref[...] = _res_split(p_ref, s_ref, tile=tile, off=off, axis=0).astype(o_ref.dtype)


def _concat_cast(p, sec, *, tm, dtype):
    mp, d = p.shape
    ms = sec.shape[0]
    tile, off = divmod(mp, tm)
    assert off + ms <= tm and (mp + ms) % tm == 0
    last_p = (mp - 1) // tm
    est = 2 * 2 * tm * d * 4 + 2 * tm * d * jnp.dtype(dtype).itemsize + 2 * tm * d * 4
    return pl.pallas_call(
        functools.partial(_cast_body, tile=tile, off=off),
        grid=((mp + ms) // tm,),
        in_specs=[pl.BlockSpec((tm, d), lambda i: (jnp.minimum(i, last_p), 0)),
                  pl.BlockSpec((ms, d), lambda i: (0, 0))],
        out_specs=pl.BlockSpec((tm, d), lambda i: (i, 0)),
        out_shape=jax.ShapeDtypeStruct((mp + ms, d), dtype),
        compiler_params=pltpu.CompilerParams(
            dimension_semantics=("arbitrary",), vmem_limit_bytes=_vmem_limit(est)),
        name="concat_cast",
    )(p, sec)


def _matmul(x, w, layer, *, tm, tn, out_dtype, name, stream_weights=False, residual=None,
            res_scale=None):
    m, k = x.shape
    n = w.shape[2]
    nj, ni = n // tn, m // tm
    est = 2 * tm * k * 2 + 2 * tm * tn * jnp.dtype(out_dtype).itemsize
    if stream_weights:
        kc = k // ni
        assert k == kc * ni and kc % 16 == 0 and ni % 2 == 0, (k, ni)
        stream = dict(layer=layer, kc=kc, tn=tn, nj=nj, ni=ni)
        w_spec = pl.BlockSpec(memory_space=pltpu.HBM)
        scratch = [pltpu.VMEM((2, k, tn), BF16), pltpu.VMEM((2, kc, tn), F32),
                   pltpu.SemaphoreType.DMA((2,))]
        est += 2 * k * tn * 2 + 2 * kc * tn * 4 + 2 * kc * tn * 4
    else:
        stream = None
        w_spec = pl.BlockSpec((None, k, tn), lambda j, i: (layer, 0, j))
        scratch = [pltpu.VMEM((k, tn), BF16)]
        est += 2 * k * tn * 4 + k * tn * 2
    in_specs = [pl.BlockSpec((tm, k), lambda j, i: (i, 0)), w_spec]
    args = [x, w]
    res_fn = None
    tile_mn = pl.BlockSpec((tm, tn), lambda j, i: (i, j))
    if residual is not None:
        kind = residual[0]
        est += 2 * tm * tn * 4
        if kind == "normed":
            _, t, stats, g, b = residual
            res_fn = _res_normed
            vec = pl.BlockSpec((1, tn), lambda j, i: (0, j))
            in_specs += [tile_mn, pl.BlockSpec((tm, 2 * LANES), lambda j, i: (i, 0)), vec, vec]
            args += [t, stats, g.reshape(1, n), b.reshape(1, n)]
            est += 2 * tm * 2 * LANES * 4
        else:
            _, p, sec = residual
            mp, ms = p.shape[0], sec.shape[0]
            tile, off = divmod(mp, tm)
            assert kind == "split" and off + ms <= tm and mp + ms == m
            last_p = (mp - 1) // tm
            res_fn = functools.partial(_res_split, tile=tile, off=off, axis=1)
            in_specs += [pl.BlockSpec((tm, tn), lambda j, i: (jnp.minimum(i, last_p), j)),
                         pl.BlockSpec((ms, tn), lambda j, i: (0, j))]
            args += [p, sec]
            est += 2 * ms * tn * 4
    return pl.pallas_call(
        functools.partial(_mm_body, n_res=len(args) - 2, res_fn=res_fn, res_scale=res_scale,
                          stream=stream),
        grid=(nj, ni),
        in_specs=in_specs,
        out_specs=tile_mn,
        out_shape=jax.ShapeDtypeStruct((m, n), out_dtype),
        scratch_shapes=scratch,
        compiler_params=pltpu.CompilerParams(
            dimension_semantics=("arbitrary", "arbitrary"), vmem_limit_bytes=_vmem_limit(est)),
        name=name,
    )(*args)


def _merge_body(za_ref, zb_ref, ga_ref, gb_ref, wa_ref, wb_ref, o_ref, wabf_ref, wbbf_ref):
    @pl.when(pl.program_id(1) == 0)
    def _():
        wabf_ref[...] = wa_ref[...].astype(BF16)
        wbbf_ref[...] = wb_ref[...].astype(BF16)

    a = jnp.dot(za_ref[...], wabf_ref[...], preferred_element_type=F32)
    b = jnp.dot(zb_ref[...], wbbf_ref[...], preferred_element_type=F32)
    ga = jax.nn.sigmoid(ga_ref[...].astype(F32))
    gb = jax.nn.sigmoid(gb_ref[...].astype(F32))
    o_ref[...] = (ga * a + gb * b).astype(o_ref.dtype)


def _merge(za, zb, h, w_a, w_b, layer, *, ga_off, gb_off, tm, tn):
    m, ka = za.shape
    kb = zb.shape[1]
    n = w_a.shape[2]
    ga_blk, gb_blk = ga_off // tn, gb_off // tn
    est = (2 * tm * (ka + kb) * 2 + 2 * (ka + kb) * tn * 4 + (ka + kb) * tn * 2
           + 2 * 2 * tm * tn * h.dtype.itemsize + 2 * tm * tn * 2 + 4 * tm * tn * 4)
    return pl.pallas_call(
        _merge_body,
        grid=(n // tn, m // tm),
        in_specs=[pl.BlockSpec((tm, ka), lambda j, i: (i, 0)),
                  pl.BlockSpec((tm, kb), lambda j, i: (i, 0)),
                  pl.BlockSpec((tm, tn), lambda j, i: (i, ga_blk + j)),
                  pl.BlockSpec((tm, tn), lambda j, i: (i, gb_blk + j)),
                  pl.BlockSpec((None, ka, tn), lambda j, i: (layer, 0, j)),
                  pl.BlockSpec((None, kb, tn), lambda j, i: (layer, 0, j))],
        out_specs=pl.BlockSpec((tm, tn), lambda j, i: (i, j)),
        out_shape=jax.ShapeDtypeStruct((m, n), BF16),
        scratch_shapes=[pltpu.VMEM((ka, tn), BF16), pltpu.VMEM((kb, tn), BF16)],
        compiler_params=pltpu.CompilerParams(
            dimension_semantics=("arbitrary", "arbitrary"), vmem_limit_bytes=_vmem_limit(est)),
        name="merge_mm",
    )(za, zb, h, h, w_a, w_b)


def _ln_body(t_ref, g_ref, b_ref, *out_refs, emit_f32):
    t = t_ref[...]
    mu = jnp.mean(t, axis=-1, keepdims=True)
    d = t - mu
    var = jnp.mean(d * d, axis=-1, keepdims=True)
    rstd = lax.rsqrt(var + LN_EPS)
    out = d * rstd * g_ref[...] + b_ref[...]
    if emit_f32:
        out_refs[0][...] = out
    else:
        obf_ref, stats_ref = out_refs
        obf_ref[...] = out.astype(BF16)
        rows = t.shape[0]
        stats_ref[:, :LANES] = jnp.broadcast_to(mu, (rows, LANES))
        stats_ref[:, LANES:] = jnp.broadcast_to(rstd, (rows, LANES))


def _layer_norm(t, g, b, *, row0, nrows, tm, emit_f32):
    d = t.shape[1]
    rb0 = row0 // tm
    assert row0 % tm == 0 and nrows % tm == 0
    est = 2 * tm * d * 4 + 2 * tm * d * 4 + 6 * tm * d * 4
    row = pl.BlockSpec((tm, d), lambda i: (i, 0))
    vec = pl.BlockSpec((1, d), lambda i: (0, 0))
    if emit_f32:
        out_specs = [row]
        out_shape = [jax.ShapeDtypeStruct((nrows, d), F32)]
    else:
        out_specs = [row, pl.BlockSpec((tm, 2 * LANES), lambda i: (i, 0))]
        out_shape = [jax.ShapeDtypeStruct((nrows, d), BF16),
                     jax.ShapeDtypeStruct((nrows, 2 * LANES), F32)]
    return pl.pallas_call(
        functools.partial(_ln_body, emit_f32=emit_f32),
        grid=(nrows // tm,),
        in_specs=[pl.BlockSpec((tm, d), lambda i: (rb0 + i, 0)), vec, vec],
        out_specs=out_specs,
        out_shape=out_shape,
        compiler_params=pltpu.CompilerParams(
            dimension_semantics=("arbitrary",), vmem_limit_bytes=_vmem_limit(est)),
        name="layer_norm",
    )(t, g.reshape(1, d), b.reshape(1, d))


def _ret_body(q_ref, k_ref, v_ref, g_ref, cos_ref, sin_ref, lg_ref, s0_ref, za_ref, sout_ref,
              s_ref, d_ref, qd_ref, kd_ref, *, c, hb, fill_slot):
    b = pl.program_id(1)
    ci = pl.program_id(2)

    @pl.when((b == 0) & (ci == 0))
    def _():
        ri = lax.broadcasted_iota(jnp.int32, (c, c), 0)
        cj = lax.broadcasted_iota(jnp.int32, (c, c), 1)
        diff = (ri - cj).astype(F32)
        r = lax.broadcasted_iota(jnp.int32, (c, RET_DK), 0).astype(F32)
        for hh in range(hb):
            lg = lg_ref[hh]
            d_ref[hh] = jnp.where(diff >= 0, jnp.exp(jnp.maximum(diff, 0.0) * lg[:, :c]), 0.0)
            qd_ref[hh] = jnp.exp((r + 1.0) * lg)
            kd_ref[hh] = jnp.exp((c - 1.0 - r) * lg)

    @pl.when(ci == 0)
    def _():
        s_ref[...] = s0_ref[...]

    cos = cos_ref[...]
    sin = sin_ref[...]
    half = RET_DK // 2

    def rope(t):
        t1, t2 = t[:, :half], t[:, half:]
        return jnp.concatenate([t1 * cos - t2 * sin, t1 * sin + t2 * cos], axis=-1)

    for hh in range(hb):
        qc = slice(hh * RET_DK, (hh + 1) * RET_DK)
        vc = slice(hh * RET_DV, (hh + 1) * RET_DV)
        q = rope(q_ref[:, qc].astype(F32))
        k = rope(k_ref[:, qc].astype(F32)) * (RET_DK ** -0.5)
        vb = v_ref[:, vc].astype(BF16)
        s = s_ref[hh]
        scores = lax.dot_general(q.astype(BF16), k.astype(BF16), (((1,), (1,)), ((), ())),
                                 preferred_element_type=F32) * d_ref[hh]
        o = (jnp.dot(scores.astype(BF16), vb, preferred_element_type=F32)
             + jnp.dot((q * qd_ref[hh]).astype(BF16), s.astype(BF16), preferred_element_type=F32))
        kv = lax.dot_general((k * kd_ref[hh]).astype(BF16), vb, (((0,), (0,)), ((), ())),
                             preferred_element_type=F32)
        s_ref[hh] = s * jnp.exp(float(c) * lg_ref[hh]) + kv

        mu = jnp.mean(o, axis=-1, keepdims=True)
        od = o - mu
        var = jnp.mean(od * od, axis=-1, keepdims=True)
        on = od * lax.rsqrt(var + GN_EPS)
        za_ref[:, vc] = (jax.nn.silu(g_ref[:, vc].astype(F32)) * on).astype(BF16)

    @pl.when(ci == pl.num_programs(2) - 1)
    def _():
        if fill_slot is None:
            sout_ref[...] = s_ref[...]
        else:
            for slot in range(sout_ref.shape[0]):
                sout_ref[slot] = s_ref[...] if slot == fill_slot else jnp.zeros(s_ref.shape, F32)


def _drop_leading(n, body):
    def wrapped(*refs):
        return body(*refs[n:])
    return wrapped


def _retention(h, s0, s0_layer, *, row0, nseq, seqlen, pos0, hb, za_into, st_into, st_layer,
               depth):
    m = h.shape[0]
    c = min(RET_BLOCK, seqlen)
    nc = seqlen // c
    rb0 = row0 // c
    hq = RET_HEADS // hb
    half = RET_DK // 2
    pos = pos0 + jnp.arange(seqlen, dtype=F32)
    freqs = ROPE_BASE ** (-jnp.arange(half, dtype=F32) / half)
    ang = pos[:, None] * freqs[None, :]
    cos, sin = jnp.cos(ang), jnp.sin(ang)
    log_gamma = jnp.log(1.0 - 2.0 ** (-5.0 - jnp.arange(RET_HEADS, dtype=F32)))
    lg = jnp.broadcast_to(log_gamma[:, None, None], (RET_HEADS, 1, RET_DK))

    def hcol(off):
        return pl.BlockSpec((c, hb * RET_DK), lambda hh, b, ci: (rb0 + b * nc + ci, off + hh))

    tab = pl.BlockSpec((c, half), lambda hh, b, ci: (ci, 0))

    def state(layer):
        return pl.BlockSpec((None, None, hb, RET_DK, RET_DV),
                            lambda hh, b, ci: (layer, b, hh, 0, 0))

    assert za_into.shape == (m, RET_V)
    st_shape = (depth, nseq, RET_HEADS, RET_DK, RET_DV)
    if st_into is None:
        donated, aliases, fill_slot = [za_into], {0: 0}, st_layer
        st_spec = pl.BlockSpec((depth, None, hb, RET_DK, RET_DV),
                               lambda hh, b, ci: (0, b, hh, 0, 0))
    else:
        assert st_into.shape == st_shape
        donated, aliases, fill_slot = [za_into, st_into], {0: 0, 1: 1}, None
        st_spec = state(st_layer)
    est = (2 * 4 * c * hb * RET_DK * 2 + 2 * (1 + depth) * hb * RET_DK * RET_DV * 4
           + hb * RET_DK * RET_DV * 4 + hb * (c * c + 2 * c * RET_DK) * 4
           + 2 * c * hb * RET_DV * 2 + 16 * c * RET_DK * 4)
    return pl.pallas_call(
        _drop_leading(len(donated),
                      functools.partial(_ret_body, c=c, hb=hb, fill_slot=fill_slot)),
        grid=(hq, nseq, nc),
        in_specs=[pl.BlockSpec(memory_space=pltpu.HBM)] * len(donated)
        + [hcol(0), hcol(hq), hcol(2 * hq), hcol(3 * hq), tab, tab,
           pl.BlockSpec((hb, 1, RET_DK), lambda hh, b, ci: (hh, 0, 0)), state(s0_layer)],
        out_specs=[pl.BlockSpec((c, hb * RET_DV), lambda hh, b, ci: (rb0 + b * nc + ci, hh)),
                   st_spec],
        out_shape=[jax.ShapeDtypeStruct((m, RET_V), BF16), jax.ShapeDtypeStruct(st_shape, F32)],
        input_output_aliases=aliases,
        scratch_shapes=[pltpu.VMEM((hb, RET_DK, RET_DV), F32), pltpu.VMEM((hb, c, c), F32),
                        pltpu.VMEM((hb, c, RET_DK), F32), pltpu.VMEM((hb, c, RET_DK), F32)],
        compiler_params=pltpu.CompilerParams(
            dimension_semantics=("arbitrary", "arbitrary", "arbitrary"),
            vmem_limit_bytes=_vmem_limit(est)),
        name="retention",
    )(*donated, h, h, h, h, cos, sin, lg, s0)


def _sgu_body(su_ref, sv_ref, lng_ref, lnb_ref, w_ref, bias_ref, zb_ref, *rest, nvalid, **kw):
    if nvalid is None:
        _sgu_tile(su_ref, sv_ref, lng_ref, lnb_ref, w_ref, bias_ref, zb_ref, *rest, **kw)
        return

    @pl.when(pl.program_id(0) < nvalid)
    def _():
        _sgu_tile(su_ref, sv_ref, lng_ref, lnb_ref, w_ref, bias_ref, zb_ref, *rest, **kw)

    @pl.when(pl.program_id(0) >= nvalid)
    def _():
        zb_ref[...] = jnp.zeros(zb_ref.shape, zb_ref.dtype)


def _sgu_tile(su_ref, sv_ref, lng_ref, lnb_ref, w_ref, bias_ref, zb_ref, *rest, c, tm, gdim, emit_v):
    sv = jax.nn.gelu(sv_ref[...].astype(F32))
    mu = jnp.mean(sv, axis=-1, keepdims=True)
    d = sv - mu
    var = jnp.mean(d * d, axis=-1, keepdims=True)
    svn = d * lax.rsqrt(var + LN_EPS) * lng_ref[...] + lnb_ref[...]
    if emit_v:
        rest[0][...] = svn
    svb = svn.astype(BF16)
    ri = lax.broadcasted_iota(jnp.int32, (SGU_CHUNK, SGU_CHUNK), 0)
    cj = lax.broadcasted_iota(jnp.int32, (SGU_CHUNK, SGU_CHUNK), 1)
    mask = (ri >= cj) & ((ri // c) == (cj // c))
    for g in range(SGU_GROUPS):
        wm = jnp.where(mask, w_ref[g], 0.0).astype(BF16)
        cols = slice(g * gdim, (g + 1) * gdim)
        for r in range(tm // SGU_CHUNK):
            rows = slice(r * SGU_CHUNK, (r + 1) * SGU_CHUNK)
            mixed = jnp.dot(wm, svb[rows, cols], preferred_element_type=F32) + bias_ref[:, cols]
            zb_ref[rows, cols] = (jax.nn.gelu(su_ref[rows, cols].astype(F32)) * mixed).astype(BF16)


def _spatial_gating(h, ln_g, ln_b, w_s, b_s, *, row0, nrows, seqlen, su_off, width, emit_v,
                    zb_into):
    m = h.shape[0]
    c = min(SGU_CHUNK, seqlen)
    rep = SGU_CHUNK // c
    gdim = width // SGU_GROUPS
    w_blk = jnp.tile(w_s[:, :c, :c], (1, rep, rep))
    bias = jnp.repeat(jnp.tile(b_s[:, :c], (1, rep)).T, gdim, axis=1)
    tm = _largest_divisor(math.gcd(nrows, m), 256, SGU_CHUNK)
    rb0 = row0 // tm
    su_blk = su_off // width
    assert row0 % tm == 0
    nvalid = nrows // tm
    if zb_into is None:
        assert row0 == 0 and not emit_v
        donated, steps, fill = [], m // tm, nvalid
    else:
        assert zb_into.shape == (m, width)
        donated, steps, fill = [zb_into], nvalid, None
    last = rb0 + nvalid - 1
    vec = pl.BlockSpec((1, width), lambda i: (0, 0))
    out_specs = [pl.BlockSpec((tm, width), lambda i: (rb0 + i, 0))]
    out_shape = [jax.ShapeDtypeStruct((m, width), BF16)]
    if emit_v:
        out_specs.append(pl.BlockSpec((tm, width), lambda i: (i, 0)))
        out_shape.append(jax.ShapeDtypeStruct((nrows, width), F32))
    est = 2 * 2 * tm * width * 4 + 2 * tm * width * (2 + 4) + 8 * tm * width * 4
    res = pl.pallas_call(
        _drop_leading(len(donated), functools.partial(_sgu_body, nvalid=fill, c=c, tm=tm,
                                                      gdim=gdim, emit_v=emit_v)),
        grid=(steps,),
        in_specs=[pl.BlockSpec(memory_space=pltpu.HBM)] * len(donated)
        + [pl.BlockSpec((tm, width), lambda i: (jnp.minimum(rb0 + i, last), su_blk)),
           pl.BlockSpec((tm, width), lambda i: (jnp.minimum(rb0 + i, last), su_blk + 1)),
           vec, vec,
           pl.BlockSpec((SGU_GROUPS, SGU_CHUNK, SGU_CHUNK), lambda i: (0, 0, 0)),
           pl.BlockSpec((SGU_CHUNK, width), lambda i: (0, 0))],
        out_specs=out_specs,
        out_shape=out_shape,
        input_output_aliases={0: 0} if donated else {},
        compiler_params=pltpu.CompilerParams(
            dimension_semantics=("arbitrary",), vmem_limit_bytes=_vmem_limit(est)),
        name="spatial_gating",
    )(*donated, h, h, ln_g.reshape(1, width), ln_b.reshape(1, width), w_blk, bias)
    return res if emit_v else (res[0], None)


def _rows(ref, slab, start, n, stride):
    if n == 1:
        return ref[slab, start:start + 1, :]
    return ref[slab, pl.ds(start, n, stride=stride), :]


def _set_rows(ref, slab, start, n, stride, value):
    if n == 1:
        ref[slab, start:start + 1, :] = value
    else:
        ref[slab, pl.ds(start, n, stride=stride), :] = value


def _upgate_body(x_ref, wg_ref, wv_ref, cw_ref, cb_ref, st_ref, z_ref, tail_ref,
                 wbf_ref, buf_ref, conv_ref, val_ref, *, tm, tn, starts, ends):
    i = pl.program_id(1)
    slabs = tn // LANES

    @pl.when(i == 0)
    def _():
        wbf_ref[:, :tn] = wg_ref[...].astype(BF16)
        wbf_ref[:, tn:] = wv_ref[...].astype(BF16)
        buf_ref[:, 0:8, :] = jnp.zeros((slabs, 8, LANES), F32)

    @pl.when(i > 0)
    def _():
        buf_ref[:, 0:8, :] = buf_ref[:, tm:tm + 8, :]

    gv = jnp.dot(x_ref[...], wbf_ref[...], preferred_element_type=F32)
    val_ref[...] = gv[:, tn:]
    for s in range(slabs):
        lanes = slice(s * LANES, (s + 1) * LANES)
        buf_ref[s, 8:tm + 8, :] = gv[:, lanes]
        conv_ref[s] = (cb_ref[:, lanes] + buf_ref[s, 6:tm + 6, :] * cw_ref[0:1, lanes]
                       + buf_ref[s, 7:tm + 7, :] * cw_ref[1:2, lanes]
                       + gv[:, lanes] * cw_ref[2:3, lanes])

    for tile, off, n, stride, slot in starts:
        @pl.when(i == tile)
        def _(off=off, n=n, stride=stride, slot=slot):
            for s in range(slabs):
                lanes = slice(s * LANES, (s + 1) * LANES)
                w0, w1, w2 = cw_ref[0:1, lanes], cw_ref[1:2, lanes], cw_ref[2:3, lanes]
                cb = cb_ref[:, lanes]
                p0 = _rows(buf_ref, s, 8 + off, n, stride)
                p1 = _rows(buf_ref, s, 8 + off + 1, n, stride)
                if slot is None:
                    c0 = cb + p0 * w2
                    c1 = cb + p0 * w1 + p1 * w2
                else:
                    s0 = st_ref[0, slot:slot + n, lanes]
                    s1 = st_ref[1, slot:slot + n, lanes]
                    c0 = cb + s0 * w0 + s1 * w1 + p0 * w2
                    c1 = cb + s1 * w0 + p0 * w1 + p1 * w2
                _set_rows(conv_ref, s, off, n, stride, c0)
                _set_rows(conv_ref, s, off + 1, n, stride, c1)

    for tile, off, n, stride, slot in ends:
        @pl.when(i == tile)
        def _(off=off, n=n, stride=stride, slot=slot):
            for s in range(slabs):
                lanes = slice(s * LANES, (s + 1) * LANES)
                tail_ref[0, slot:slot + n, lanes] = _rows(buf_ref, s, 8 + off - 1, n, stride)
                tail_ref[1, slot:slot + n, lanes] = _rows(buf_ref, s, 8 + off, n, stride)

    for s in range(slabs):
        lanes = slice(s * LANES, (s + 1) * LANES)
        z_ref[:, lanes] = (jax.nn.gelu(conv_ref[s]) * val_ref[:, lanes]).astype(BF16)


def _seq_groups(seq_rows, seqlen, tm, within):
    groups = []
    for idx, r0 in enumerate(seq_rows):
        row = r0 + within
        tile, off = divmod(row, tm)
        if groups and groups[-1][0] == tile and seq_rows[idx - 1] + seqlen == r0:
            t, o, n, st, first = groups[-1]
            groups[-1] = (t, o, n + 1, st, first)
        else:
            groups.append((tile, off, 1, seqlen, idx))
    return groups


def _up_gate(xb, w_up, layer, conv_w, conv_b, conv_s0_s, *, bp, lp, bs, ls, tm, tn):
    m, k = xb.shape
    dff = w_up.shape[2] // 2
    vblk = dff // tn
    nseq = bp + bs
    prompt_rows = [b * lp for b in range(bp)]
    sample_rows = [bp * lp + b * ls for b in range(bs)]
    starts, ends = [], []
    for rows, sl, has_state, slot0 in ((sample_rows, ls, True, 0), (prompt_rows, lp, False, bs)):
        for t, o, n, st, first in _seq_groups(rows, sl, tm, 0):
            assert o + (n - 1) * st + 1 < tm, "a sequence's first two rows must share a row tile"
            starts.append((t, o, n, st, first if has_state else None))
        for t, o, n, st, first in _seq_groups(rows, sl, tm, sl - 1):
            assert o >= 1, "a sequence's last two rows must share a row tile"
            ends.append((t, o, n, st, slot0 + first))
    st_in = jnp.transpose(conv_s0_s, (1, 0, 2))
    est = (2 * tm * k * 2 + 2 * 2 * k * tn * 4 + 2 * k * tn * 2 + 2 * (tm + 8) * tn * 4
           + 2 * tm * tn * 2 + 2 * 2 * (bs + nseq) * tn * 4 + 6 * tm * tn * 4)
    return pl.pallas_call(
        functools.partial(_upgate_body, tm=tm, tn=tn, starts=tuple(starts), ends=tuple(ends)),
        grid=(dff // tn, m // tm),
        in_specs=[pl.BlockSpec((tm, k), lambda j, i: (i, 0)),
                  pl.BlockSpec((None, k, tn), lambda j, i: (layer, 0, j)),
                  pl.BlockSpec((None, k, tn), lambda j, i: (layer, 0, vblk + j)),
                  pl.BlockSpec((CONV_W, tn), lambda j, i: (0, j)),
                  pl.BlockSpec((1, tn), lambda j, i: (0, j)),
                  pl.BlockSpec((CONV_W - 1, bs, tn), lambda j, i: (0, 0, j))],
        out_specs=[pl.BlockSpec((tm, tn), lambda j, i: (i, j)),
                   pl.BlockSpec((CONV_W - 1, nseq, tn), lambda j, i: (0, 0, j))],
        out_shape=[jax.ShapeDtypeStruct((m, dff), BF16),
                   jax.ShapeDtypeStruct((CONV_W - 1, nseq, dff), F32)],
        scratch_shapes=[pltpu.VMEM((k, 2 * tn), BF16),
                        pltpu.VMEM((tn // LANES, tm + 8, LANES), F32),
                        pltpu.VMEM((tn // LANES, tm, LANES), F32), pltpu.VMEM((tm, tn), F32)],
        compiler_params=pltpu.CompilerParams(
            dimension_semantics=("arbitrary", "arbitrary"), vmem_limit_bytes=_vmem_limit(est)),
        name="up_gate",
    )(xb, w_up, w_up, conv_w, conv_b.reshape(1, dff), st_in)


def _layer(layer, depth, x_res, xb, state_ret, conv_s0_s, ret_bufs, za_buf, big, small, *, bp,
           lp, bs, ls, alpha, last):
    w_in, w_a, w_b, w_o, w_up, w_down = big
    ln1_g, ln1_b, sgu_ln_g, sgu_ln_b, sgu_w, sgu_b, conv_w, conv_b, ln2_g, ln2_b = small
    m, d = xb.shape
    mp = bp * lp
    ms = bs * ls
    width = d // 2
    dff = w_down.shape[1]
    su_off = 2 * RET_QK + 2 * RET_V
    ga_off = su_off + 2 * width
    gb_off = ga_off + d
    tm_big = _largest_divisor(m, 1088, 16)

    h = _matmul(xb, w_in, layer, tm=tm_big, tn=1024, out_dtype=BF16, name="mm_in",
                stream_weights=True)

    zero_ret = jnp.zeros((1, bp, RET_HEADS, RET_DK, RET_DV), F32)
    za, ret_p = _retention(h, zero_ret, 0, row0=0, nseq=bp, seqlen=lp, pos0=0.0, hb=RET_HEADS,
                           za_into=za_buf, st_into=ret_bufs[0], st_layer=layer, depth=depth)
    za, ret_s = _retention(h, state_ret, layer, row0=mp, nseq=bs, seqlen=ls,
                           pos0=float(PAST_LEN), hb=RET_HEADS,
                           za_into=za, st_into=ret_bufs[1], st_layer=layer, depth=depth)
    zb, _ = _spatial_gating(h, sgu_ln_g, sgu_ln_b, sgu_w, sgu_b, row0=0, nrows=mp, seqlen=lp,
                            su_off=su_off, width=width, emit_v=False, zb_into=None)
    zb, sv_s = _spatial_gating(h, sgu_ln_g, sgu_ln_b, sgu_w, sgu_b, row0=mp, nrows=ms, seqlen=ls,
                               su_off=su_off, width=width, emit_v=True, zb_into=zb)

    merged = _merge(za, zb, h, w_a, w_b, layer, ga_off=ga_off, gb_off=gb_off, tm=tm_big, tn=512)
    t1 = _matmul(merged, w_o, layer, tm=tm_big, tn=512, out_dtype=F32, name="mm_o",
                 residual=x_res, res_scale=alpha)
    tm_ln = _largest_divisor(math.gcd(mp, ms), 512, 8)
    x1b, stats1 = _layer_norm(t1, ln1_g, ln1_b, row0=0, nrows=m, tm=tm_ln, emit_f32=False)

    z, tails = _up_gate(x1b, w_up, layer, conv_w, conv_b, conv_s0_s, bp=bp, lp=lp, bs=bs, ls=ls,
                        tm=tm_big, tn=_largest_divisor(dff, 256, LANES))
    t2 = _matmul(z, w_down, layer, tm=_largest_divisor(m, 544, 16), tn=512, out_dtype=F32,
                 name="mm_down", stream_weights=True,
                 residual=("normed", t1, stats1, ln1_g, ln1_b), res_scale=alpha)
    if last:
        (y_p,) = _layer_norm(t2, ln2_g, ln2_b, row0=0, nrows=mp, tm=tm_ln, emit_f32=True)
        (y_s,) = _layer_norm(t2, ln2_g, ln2_b, row0=mp, nrows=ms, tm=tm_ln, emit_f32=True)
        x2_res, x2b = (y_p, y_s), None
    else:
        x2b, stats2 = _layer_norm(t2, ln2_g, ln2_b, row0=0, nrows=m, tm=tm_ln, emit_f32=False)
        x2_res = ("normed", t2, stats2, ln2_g, ln2_b)

    conv_s = jnp.transpose(tails[:, :bs], (1, 0, 2))
    conv_p = jnp.transpose(tails[:, bs:], (1, 0, 2))
    return x2_res, x2b, (ret_p, ret_s), conv_p, conv_s, sv_s.reshape(bs, ls, width), za


def kernel(x_prompt, x_sample, state_ret, state_conv, w_in, w_a, w_b, w_o, ln1_g, ln1_b,
           sgu_ln_g, sgu_ln_b, sgu_w, sgu_b, w_up, conv_w, conv_b, w_down, ln2_g, ln2_b):
    bp, lp, d = x_prompt.shape
    bs, ls, _ = x_sample.shape
    depth = w_in.shape[0]
    alpha = float((2 * depth) ** 0.25)
    xp, xs = x_prompt.reshape(bp * lp, d), x_sample.reshape(bs * ls, d)
    xb = _concat_cast(xp, xs, tm=_largest_divisor(math.gcd(bp * lp, bs * ls), 512, 16), dtype=BF16)
    x = ("split", xp, xs)
    big = (w_in, w_a, w_b, w_o, w_up, w_down)
    ret_bufs = (None, None)
    za_buf = jnp.zeros((xb.shape[0], RET_V), BF16)
    conv_p, conv_s, sgu_v = [], [], []
    for l in range(depth):
        small = (ln1_g[l], ln1_b[l], sgu_ln_g[l], sgu_ln_b[l], sgu_w[l], sgu_b[l],
                 conv_w[l], conv_b[l], ln2_g[l], ln2_b[l])
        x, xb, ret_bufs, cp, cs, sv, za_buf = _layer(
            l, depth, x, xb, state_ret, state_conv[l], ret_bufs, za_buf, big, small,
            bp=bp, lp=lp, bs=bs, ls=ls, alpha=alpha, last=l == depth - 1)
        conv_p.append(cp)
        conv_s.append(cs)
        sgu_v.append(sv)
    y_p, y_s = x
    return (y_p.reshape(bp, lp, d), y_s.reshape(bs, ls, d), ret_bufs[0], jnp.stack(conv_p),
            ret_bufs[1], jnp.stack(conv_s), jnp.stack(sgu_v))
```
